```python
import jax, jax.numpy as jnp
from jax import lax
import numpy as np

D_MODEL = 1024
BATCH = 1
SEQ = 16384
DEPTH = 2

MIX_WIDTH = D_MODEL
CONV_WIDTH = MIX_WIDTH // 2
ATT_WIDTH = MIX_WIDTH - CONV_WIDTH
HEAD_DIM = 64
N_ATT_HEADS = ATT_WIDTH // HEAD_DIM
N_CONV_GROUPS = CONV_WIDTH // HEAD_DIM
CONV_K = 3
N_MEM = 256
N_XHEADS = 4
XHEAD_DIM = D_MODEL // N_XHEADS
D_FF = 2816
Q_BLOCK = 128
EPS = 1e-6
IN_COLS = 3 * CONV_WIDTH + 3 * ATT_WIDTH + N_ATT_HEADS
SPLITS = [CONV_WIDTH, 2 * CONV_WIDTH, 3 * CONV_WIDTH,
          3 * CONV_WIDTH + ATT_WIDTH, 3 * CONV_WIDTH + 2 * ATT_WIDTH,
          3 * CONV_WIDTH + 3 * ATT_WIDTH]

kernel_name = "hybrid_conv_fox_macaron_memxattn"


def rmsnorm(x, g):
    xf = x.astype(jnp.float32)
    y = xf * lax.rsqrt(jnp.mean(xf * xf, axis=-1, keepdims=True) + EPS)
    return (y * g.astype(jnp.float32)).astype(x.dtype)


def swiglu(h, w_gu, w_down):
    gate, up = jnp.split(h @ w_gu, 2, axis=-1)
    return (jax.nn.silu(gate) * up) @ w_down


def causal_depthwise_conv(u, w):
    kern = w[:, None, :].astype(u.dtype)
    return lax.conv_general_dilated(
        u, kern, window_strides=(1,), padding=[(CONV_K - 1, 0)],
        dimension_numbers=('NWC', 'WIO', 'NWC'), feature_group_count=u.shape[-1])


def forgetting_attention(q, k, v, log_f):
    B, S, H, Dh = q.shape
    nb = S // Q_BLOCK
    scale = Dh ** -0.5
    c = jnp.cumsum(log_f, axis=1).transpose(0, 2, 1)
    qb = q.reshape(B, nb, Q_BLOCK, H, Dh).transpose(1, 0, 3, 2, 4)
    cqb = c.reshape(B, H, nb, Q_BLOCK).transpose(2, 0, 1, 3)
    kpos = jnp.arange(S)

    def block(args):
        i, q_i, cq_i = args
        s = jnp.einsum('bhqd,bkhd->bhqk', q_i, k,
                       preferred_element_type=jnp.float32) * scale
        s = s + cq_i[..., None] - c[:, :, None, :]
        qpos = i * Q_BLOCK + jnp.arange(Q_BLOCK)
        mask = kpos[None, :] <= qpos[:, None]
        s = jnp.where(mask, s, -jnp.inf)
        p = jax.nn.softmax(s, axis=-1)
        return jnp.einsum('bhqk,bkhd->bqhd', p.astype(v.dtype), v)

    out = lax.map(block, (jnp.arange(nb), qb, cqb))
    return out.transpose(1, 0, 2, 3, 4).reshape(B, S, H * Dh)


def memory_cross_attention(h, m, w_q, w_kv, w_o):
    B, S, _ = h.shape
    M = m.shape[1]
    q = (h @ w_q).reshape(B, S, N_XHEADS, XHEAD_DIM)
    k, v = jnp.split(m @ w_kv, 2, axis=-1)
    k = k.reshape(B, M, N_XHEADS, XHEAD_DIM)
    v = v.reshape(B, M, N_XHEADS, XHEAD_DIM)
    s = jnp.einsum('bshd,bmhd->bhsm', q, k,
                   preferred_element_type=jnp.float32) * (XHEAD_DIM ** -0.5)
    p = jax.nn.softmax(s, axis=-1)
    o = jnp.einsum('bhsm,bmhd->bshd', p.astype(v.dtype), v).reshape(B, S, D_MODEL)
    return o @ w_o


def setup_inputs(seed: int = 0) -> dict:
    key = jax.random.key(seed)
    ks = jax.random.split(key, 24)
    f32 = jnp.float32

    def w(k, shape, fan_in):
        return jax.random.normal(k, shape, f32) * (fan_in ** -0.5)

    def gain(k, shape):
        return 1.0 + 0.1 * jax.random.normal(k, shape, f32)

    return {
        "x": jax.random.normal(ks[0], (BATCH, SEQ, D_MODEL), f32),
        "mem": jax.random.normal(ks[1], (BATCH, N_MEM, D_MODEL), f32),
        "g_ffn1": gain(ks[2], (DEPTH, D_MODEL)),
        "w_ffn1_gu": w(ks[3], (DEPTH, D_MODEL, 2 * D_FF), D_MODEL),
        "w_ffn1_down": w(ks[4], (DEPTH, D_FF, D_MODEL), D_FF),
        "g_mix": gain(ks[5], (DEPTH, D_MODEL)),
        "w_mix_in": w(ks[6], (DEPTH, D_MODEL, IN_COLS), D_MODEL),
        "w_conv": w(ks[7], (DEPTH, CONV_K, CONV_WIDTH), CONV_K),
        "b_f": 2.0 + 0.5 * jax.random.normal(ks[8], (DEPTH, N_ATT_HEADS), f32),
        "g_conv_out": gain(ks[9], (DEPTH, CONV_WIDTH)),
        "g_att_out": gain(ks[10], (DEPTH, ATT_WIDTH)),
        "w_mix_out": w(ks[11], (DEPTH, MIX_WIDTH, D_MODEL), MIX_WIDTH),
        "g_xattn": gain(ks[12], (DEPTH, D_MODEL)),
        "g_mem": gain(ks[13], (DEPTH, D_MODEL)),
        "w_xq": w(ks[14], (DEPTH, D_MODEL, D_MODEL), D_MODEL),
        "w_xkv": w(ks[15], (DEPTH, D_MODEL, 2 * D_MODEL), D_MODEL),
        "w_xo": w(ks[16], (DEPTH, D_MODEL, D_MODEL), D_MODEL),
        "g_ffn2": gain(ks[17], (DEPTH, D_MODEL)),
        "w_ffn2_gu": w(ks[18], (DEPTH, D_MODEL, 2 * D_FF), D_MODEL),
        "w_ffn2_down": w(ks[19], (DEPTH, D_FF, D_MODEL), D_FF),
        "g_final": gain(ks[20], (D_MODEL,)),
    }


def reference(x, mem, g_ffn1, w_ffn1_gu, w_ffn1_down, g_mix, w_mix_in, w_conv, b_f,
              g_conv_out, g_att_out, w_mix_out, g_xattn, g_mem, w_xq, w_xkv, w_xo,
              g_ffn2, w_ffn2_gu, w_ffn2_down, g_final):
    B, S, _ = x.shape
    for l in range(DEPTH):
        x = x + 0.5 * swiglu(rmsnorm(x, g_ffn1[l]), w_ffn1_gu[l], w_ffn1_down[l])

        h = rmsnorm(x, g_mix[l])
        z = h @ w_mix_in[l]
        zb, zc, zv, zq, zk, zval, zf = jnp.split(z, SPLITS, axis=-1)

        y_conv = zb * causal_depthwise_conv(zc * zv, w_conv[l])

        log_f = jax.nn.log_sigmoid((zf + b_f[l]).astype(jnp.float32))
        q = zq.reshape(B, S, N_ATT_HEADS, HEAD_DIM)
        k = zk.reshape(B, S, N_ATT_HEADS, HEAD_DIM)
        v = zval.reshape(B, S, N_ATT_HEADS, HEAD_DIM)
        y_att = forgetting_attention(q, k, v, log_f)

        y = jnp.concatenate([rmsnorm(y_conv, g_conv_out[l]),
                             rmsnorm(y_att, g_att_out[l])], axis=-1)
        x = x + y @ w_mix_out[l]

        x = x + memory_cross_attention(rmsnorm(x, g_xattn[l]), rmsnorm(mem, g_mem[l]),
                                       w_xq[l], w_xkv[l], w_xo[l])

        x = x + 0.5 * swiglu(rmsnorm(x, g_ffn2[l]), w_ffn2_gu[l], w_ffn2_down[l])
    return rmsnorm(x, g_final)
```

```python
import functools

import numpy as np
import jax
import jax.numpy as jnp
from jax import lax
from jax.experimental import pallas as pl
from jax.experimental.pallas import tpu as pltpu

F32 = jnp.float32
BF16 = jnp.bfloat16

EPS = 1e-6
LANES = 128
HEAD_DIM = 64
N_HEADS = 8
CONV_WIDTH = 512
ATT_WIDTH = 512
CONV_K = 3
N_XHEADS = 4
N_SPLIT = 3
ONE_LANE = LANES - 1
VMEM_LIMIT = 48 * 1024 * 1024

ROW_TILE = 512
ATT_TILE = 512
FF_TILE = 1408


def _rms(x, g):
    ms = jnp.mean(x * x, axis=-1, keepdims=True)
    return x * lax.rsqrt(ms + EPS) * g


def _split3(x):
    hi = x.astype(BF16).astype(F32)
    r = x - hi
    mid = r.astype(BF16).astype(F32)
    lo = (r - mid).astype(BF16).astype(F32)
    return hi, mid, lo


def _ffn_kernel(x_ref, g_ref, wg_ref, wu_ref, wd_ref, gf_ref, o_ref, h_scr, acc_scr,
                *, n_ff, final_norm):
    f = pl.program_id(1)

    @pl.when(f == 0)
    def _():
        h_scr[...] = _rms(x_ref[...], g_ref[...]).astype(BF16)
        acc_scr[...] = jnp.zeros_like(acc_scr)

    h = h_scr[...]
    gate = jnp.dot(h, wg_ref[...], preferred_element_type=F32)
    up = jnp.dot(h, wu_ref[...], preferred_element_type=F32)
    act = (gate * (1.0 / (1.0 + jnp.exp(-gate))) * up).astype(BF16)
    acc_scr[...] += jnp.dot(act, wd_ref[...], preferred_element_type=F32)

    @pl.when(f == n_ff - 1)
    def _():
        y = x_ref[...] + 0.5 * acc_scr[...]
        if final_norm:
            y = _rms(y, gf_ref[...])
        o_ref[...] = y


def _ffn(x, g, w_gu, w_down, g_final, *, final_norm):
    s, d = x.shape
    d_ff = w_down.shape[0]
    tm, tf = ROW_TILE, FF_TILE
    n_ff = d_ff // tf
    assert s % tm == 0 and d_ff % tf == 0
    return pl.pallas_call(
        functools.partial(_ffn_kernel, n_ff=n_ff, final_norm=final_norm),
        grid=(s // tm, n_ff),
        in_specs=[
            pl.BlockSpec((tm, d), lambda i, f: (i, 0)),
            pl.BlockSpec((1, d), lambda i, f: (0, 0)),
            pl.BlockSpec((d, tf), lambda i, f: (0, f)),
            pl.BlockSpec((d, tf), lambda i, f: (0, f + n_ff)),
            pl.BlockSpec((tf, d), lambda i, f: (f, 0)),
            pl.BlockSpec((1, d), lambda i, f: (0, 0)),
        ],
        out_specs=pl.BlockSpec((tm, d), lambda i, f: (i, 0)),
        out_shape=jax.ShapeDtypeStruct((s, d), F32),
        scratch_shapes=[pltpu.VMEM((tm, d), BF16), pltpu.VMEM((tm, d), F32)],
        compiler_params=pltpu.CompilerParams(
            dimension_semantics=("arbitrary", "arbitrary"), vmem_limit_bytes=VMEM_LIMIT),
        name="ffn",
    )(x, g, w_gu, w_gu, w_down, g_final)


def _mix_kernel(x_ref, g_ref, wbcv_ref, wq_ref, wk_ref, wv_ref, wf_ref, bf_ref, wconv_ref,
                gconv_ref, tri_ref, eq_ref, ek_ref, vone_ref,
                ycn_ref, q_ref, k_ref, v_ref, c_ref, u_scr, carry_scr):
    i = pl.program_id(0)
    tm = x_ref.shape[0]
    cw = CONV_WIDTH

    @pl.when(i == 0)
    def _():
        u_scr[0:8, :] = jnp.zeros((8, cw), F32)
        carry_scr[...] = jnp.zeros_like(carry_scr)

    h = _rms(x_ref[...], g_ref[...]).astype(BF16)

    z = jnp.dot(h, wbcv_ref[...], preferred_element_type=F32)
    u = z[:, cw:2 * cw] * z[:, 2 * cw:3 * cw]
    u_scr[8:tm + 8, :] = u
    u1 = u_scr[7:tm + 7, :]
    u2 = u_scr[6:tm + 6, :]
    w = wconv_ref[...]
    conv = w[2:3, :] * u + w[1:2, :] * u1 + w[0:1, :] * u2
    u_scr[0:8, :] = u[tm - 8:tm, :]
    ycn_ref[...] = _rms(z[:, 0:cw] * conv, gconv_ref[...]).astype(BF16)

    zf = jnp.dot(h, wf_ref[...], preferred_element_type=F32) + bf_ref[...]
    logf = jnp.minimum(zf, 0.0) - jnp.log1p(jnp.exp(-jnp.abs(zf)))
    tri = tri_ref[...]
    c = carry_scr[7:8, :]
    for piece in _split3(logf):
        c = c + jnp.dot(tri, piece.astype(BF16), preferred_element_type=F32)
    carry_scr[...] = c[tm - 8:tm, :]
    c_ref[...] = c

    hi, mid, lo = _split3(c)
    lane = lax.broadcasted_iota(jnp.int32, c.shape, 1)
    sel = lane % N_SPLIT
    pieces = jnp.where(sel == 0, hi, jnp.where(sel == 1, mid, lo))
    pieces = jnp.where(lane == ONE_LANE, 1.0, pieces).astype(BF16)

    scale = HEAD_DIM ** -0.5
    q = jnp.dot(h, wq_ref[...], preferred_element_type=F32) * scale
    q_ref[...] = (q + jnp.dot(pieces, eq_ref[...], preferred_element_type=F32)).astype(BF16)
    k = jnp.dot(h, wk_ref[...], preferred_element_type=F32)
    k_ref[...] = (k + jnp.dot(pieces, ek_ref[...], preferred_element_type=F32)).astype(BF16)
    v = jnp.dot(h, wv_ref[...], preferred_element_type=F32)
    v_ref[...] = (v + vone_ref[...]).astype(BF16)


def _bias_placement():
    eq = np.zeros((LANES, N_HEADS * LANES), np.float32)
    ek = np.zeros((LANES, N_HEADS * LANES), np.float32)
    vone = np.zeros((1, N_HEADS * LANES), np.float32)
    for h in range(N_HEADS):
        base = h * LANES + HEAD_DIM
        for i in range(N_SPLIT):
            eq[N_SPLIT * h + i, base + i] = 1.0
            ek[ONE_LANE, base + i] = 1.0
            eq[ONE_LANE, base + N_SPLIT + i] = 1.0
            ek[N_SPLIT * h + i, base + N_SPLIT + i] = -1.0
        vone[0, base] = 1.0
    return jnp.asarray(eq, BF16), jnp.asarray(ek, BF16), jnp.asarray(vone, F32)


def _mix(x, g, wbcv, wq, wk, wv, wf, bfr, wconv, gconv):
    s, d = x.shape
    tm = ROW_TILE
    hp = N_HEADS * LANES
    tri = jnp.asarray(np.tril(np.ones((tm, tm), np.float32)), BF16)
    eq, ek, vone = _bias_placement()
    const = lambda shape: pl.BlockSpec(shape, lambda i: (0, 0))
    row = lambda width: pl.BlockSpec((tm, width), lambda i: (i, 0))
    return pl.pallas_call(
        _mix_kernel,
        grid=(s // tm,),
        in_specs=[
            row(d), const((1, d)), const((d, 3 * CONV_WIDTH)), const((d, hp)), const((d, hp)),
            const((d, hp)), const((d, LANES)), const((1, LANES)), const((CONV_K, CONV_WIDTH)),
            const((1, CONV_WIDTH)), const((tm, tm)), const((LANES, hp)), const((LANES, hp)),
            const((1, hp)),
        ],
        out_specs=[row(CONV_WIDTH), row(hp), row(hp), row(hp), row(LANES)],
        out_shape=[
            jax.ShapeDtypeStruct((s, CONV_WIDTH), BF16),
            jax.ShapeDtypeStruct((s, hp), BF16),
            jax.ShapeDtypeStruct((s, hp), BF16),
            jax.ShapeDtypeStruct((s, hp), BF16),
            jax.ShapeDtypeStruct((s, LANES), F32),
        ],
        scratch_shapes=[pltpu.VMEM((tm + 8, CONV_WIDTH), F32), pltpu.VMEM((8, LANES), F32)],
        compiler_params=pltpu.CompilerParams(
            dimension_semantics=("arbitrary",), vmem_limit_bytes=VMEM_LIMIT),
        name="mix_in",
    )(x, g, wbcv, wq, wk, wv, wf, bfr, wconv, gconv, tri, eq, ek, vone)


def _att_kernel(q_ref, k_ref, v_ref, o_ref, m_scr, acc_scr):
    i = pl.program_id(1)
    t = q_ref.shape[0]
    q = q_ref[...]
    nt = (((1,), (1,)), ((), ()))

    def scores(j):
        start = pl.multiple_of(j * t, t)
        kb = k_ref[pl.ds(start, t), :]
        return lax.dot_general(q, kb, nt, preferred_element_type=F32), start

    s, start = scores(i)
    row = lax.broadcasted_iota(jnp.int32, (t, t), 0)
    col = lax.broadcasted_iota(jnp.int32, (t, t), 1)
    s = jnp.where(row >= col, s, -jnp.inf)
    m = jnp.max(s, axis=-1, keepdims=True)
    p = jnp.exp(s - m).astype(BF16)
    m_scr[...] = m
    acc_scr[...] = jnp.dot(p, v_ref[pl.ds(start, t), :], preferred_element_type=F32)

    def body(n, carry):
        s, start = scores(i - 1 - n)
        m_old = m_scr[...]
        m_new = jnp.maximum(m_old, jnp.max(s, axis=-1, keepdims=True))
        p = jnp.exp(s - m_new).astype(BF16)
        m_scr[...] = m_new
        acc_scr[...] = jnp.exp(m_old - m_new) * acc_scr[...] + jnp.dot(
            p, v_ref[pl.ds(start, t), :], preferred_element_type=F32)
        return carry

    lax.fori_loop(0, i, body, 0)

    acc = acc_scr[...]
    denom = acc[:, HEAD_DIM:HEAD_DIM + 1]
    lane = lax.broadcasted_iota(jnp.int32, acc.shape, 1)
    o_ref[...] = jnp.where(lane < HEAD_DIM, acc / denom, 0.0)


def _attention(q, k, v):
    s = q.shape[0]
    t = ATT_TILE
    return pl.pallas_call(
        _att_kernel,
        grid=(N_HEADS, s // t),
        in_specs=[
            pl.BlockSpec((t, LANES), lambda h, i: (i, h)),
            pl.BlockSpec((s, LANES), lambda h, i: (0, h)),
            pl.BlockSpec((s, LANES), lambda h, i: (0, h)),
        ],
        out_specs=pl.BlockSpec((t, LANES), lambda h, i: (i, h)),
        out_shape=jax.ShapeDtypeStruct((s, N_HEADS * LANES), F32),
        scratch_shapes=[pltpu.VMEM((t, 1), F32), pltpu.VMEM((t, LANES), F32)],
        compiler_params=pltpu.CompilerParams(
            dimension_semantics=("arbitrary", "arbitrary"), vmem_limit_bytes=VMEM_LIMIT),
        name="fox_attention",
    )(q, k, v)


def _memkv_kernel(mem_ref, g_ref, w_ref, k_ref, v_ref):
    d = mem_ref.shape[1]
    m = _rms(mem_ref[...], g_ref[...]).astype(BF16)
    kv = jnp.dot(m, w_ref[...], preferred_element_type=F32)
    k_ref[...] = kv[:, :d].astype(BF16)
    v_ref[...] = kv[:, d:].astype(BF16)


def _memkv(mem, g, w_kv):
    n, d = mem.shape
    return pl.pallas_call(
        _memkv_kernel,
        out_shape=[jax.ShapeDtypeStruct((n, d), BF16), jax.ShapeDtypeStruct((n, d), BF16)],
        compiler_params=pltpu.CompilerParams(vmem_limit_bytes=VMEM_LIMIT),
        name="mem_kv",
    )(mem, g, w_kv)


def _out_xattn_kernel(x_ref, ycn_ref, ya_ref, gatt_ref, woc_ref, woa_ref, gx_ref, wxq_ref,
                      kx_ref, vx_ref, wxo_ref, o_ref):
    d = x_ref.shape[1]
    xd = d // N_XHEADS
    ya = ya_ref[...]
    ms = jnp.sum(ya * ya, axis=-1, keepdims=True) * (1.0 / ATT_WIDTH)
    yan = (ya * lax.rsqrt(ms + EPS) * gatt_ref[...]).astype(BF16)
    x1 = (x_ref[...] + jnp.dot(ycn_ref[...], woc_ref[...], preferred_element_type=F32)
          + jnp.dot(yan, woa_ref[...], preferred_element_type=F32))

    hx = _rms(x1, gx_ref[...]).astype(BF16)
    q = (jnp.dot(hx, wxq_ref[...], preferred_element_type=F32) * (xd ** -0.5)).astype(BF16)
    nt = (((1,), (1,)), ((), ()))
    heads = []
    for hh in range(N_XHEADS):
        sl = slice(hh * xd, (hh + 1) * xd)
        s = lax.dot_general(q[:, sl], kx_ref[:, sl], nt, preferred_element_type=F32)
        p = jnp.exp(s - jnp.max(s, axis=-1, keepdims=True))
        denom = jnp.sum(p, axis=-1, keepdims=True)
        o = jnp.dot(p.astype(BF16), vx_ref[:, sl], preferred_element_type=F32) / denom
        heads.append(o.astype(BF16))
    o = jnp.concatenate(heads, axis=-1)
    o_ref[...] = x1 + jnp.dot(o, wxo_ref[...], preferred_element_type=F32)


def _out_xattn(x, ycn, ya, gatt, woc, woa, gx, wxq, kx, vx, wxo):
    s, d = x.shape
    tm = ROW_TILE
    n_mem = kx.shape[0]
    hp = N_HEADS * LANES
    const = lambda shape: pl.BlockSpec(shape, lambda i: (0, 0))
    row = lambda width: pl.BlockSpec((tm, width), lambda i: (i, 0))
    return pl.pallas_call(
        _out_xattn_kernel,
        grid=(s // tm,),
        in_specs=[
            row(d), row(CONV_WIDTH), row(hp), const((1, hp)), const((CONV_WIDTH, d)),
            const((hp, d)), const((1, d)), const((d, d)), const((n_mem, d)), const((n_mem, d)),
            const((d, d)),
        ],
        out_specs=row(d),
        out_shape=jax.ShapeDtypeStruct((s, d), F32),
        compiler_params=pltpu.CompilerParams(
            dimension_semantics=("arbitrary",), vmem_limit_bytes=VMEM_LIMIT),
        name="out_xattn",
    )(x, ycn, ya, gatt, woc, woa, gx, wxq, kx, vx, wxo)


def _pad_heads_cols(w):
    d = w.shape[0]
    w = w.reshape(d, N_HEADS, HEAD_DIM)
    w = jnp.pad(w, ((0, 0), (0, 0), (0, LANES - HEAD_DIM)))
    return w.reshape(d, N_HEADS * LANES)


def kernel(x, mem, g_ffn1, w_ffn1_gu, w_ffn1_down, g_mix, w_mix_in, w_conv, b_f, g_conv_out,
           g_att_out, w_mix_out, g_xattn, g_mem, w_xq, w_xkv, w_xo, g_ffn2, w_ffn2_gu,
           w_ffn2_down, g_final):
    b, s, d = x.shape
    assert b == 1
    depth = g_ffn1.shape[0]
    xs = x.reshape(s, d)
    mem2 = mem.reshape(mem.shape[1], d)
    row = lambda a: a.reshape(1, -1).astype(F32)
    cw, aw = CONV_WIDTH, ATT_WIDTH
    gfin = row(g_final)

    for l in range(depth):
        xs = _ffn(xs, row(g_ffn1[l]), w_ffn1_gu[l].astype(BF16), w_ffn1_down[l].astype(BF16),
                  gfin, final_norm=False)

        w_in = w_mix_in[l]
        wbcv = w_in[:, :3 * cw].astype(BF16)
        wq = _pad_heads_cols(w_in[:, 3 * cw:3 * cw + aw]).astype(BF16)
        wk = _pad_heads_cols(w_in[:, 3 * cw + aw:3 * cw + 2 * aw]).astype(BF16)
        wv = _pad_heads_cols(w_in[:, 3 * cw + 2 * aw:3 * cw + 3 * aw]).astype(BF16)
        wf = jnp.repeat(w_in[:, 3 * cw + 3 * aw:], N_SPLIT, axis=1)
        wf = jnp.pad(wf, ((0, 0), (0, LANES - N_SPLIT * N_HEADS))).astype(BF16)
        bfr = jnp.pad(jnp.repeat(b_f[l], N_SPLIT), (0, LANES - N_SPLIT * N_HEADS)).reshape(1, LANES)
        ycn, q, k, v, _ = _mix(xs, row(g_mix[l]), wbcv, wq, wk, wv, wf, bfr.astype(F32),
                               w_conv[l].astype(F32), row(g_conv_out[l]))

        ya = _attention(q, k, v)

        w_out = w_mix_out[l]
        woc = w_out[:cw].astype(BF16)
        woa = jnp.pad(w_out[cw:].reshape(N_HEADS, HEAD_DIM, d),
                      ((0, 0), (0, LANES - HEAD_DIM), (0, 0))).reshape(N_HEADS * LANES, d).astype(BF16)
        gatt = jnp.pad(g_att_out[l].reshape(N_HEADS, HEAD_DIM),
                       ((0, 0), (0, LANES - HEAD_DIM))).reshape(1, N_HEADS * LANES).astype(F32)
        kx, vx = _memkv(mem2, row(g_mem[l]), w_xkv[l].astype(BF16))
        xs = _out_xattn(xs, ycn, ya, gatt, woc, woa, row(g_xattn[l]), w_xq[l].astype(BF16),
                        kx, vx, w_xo[l].astype(BF16))

        xs = _ffn(xs, row(g_ffn2[l]), w_ffn2_gu[l].astype(BF16), w_ffn2_down[l].astype(BF16),
                  gfin, final_norm=(l == depth - 1))
    return xs.reshape(b, s, d)
```

```python
import functools

import numpy as np
import jax
import jax.numpy as jnp
from jax import lax
from jax.experimental import pallas as pl
from jax.experimental.pallas import tpu as pltpu

F32 = jnp.float32
BF16 = jnp.bfloat16

EPS = 1e-6
LANES = 128
HEAD_DIM = 64
N_HEADS = 8
CONV_WIDTH = 512
ATT_WIDTH = 512
CONV_K = 3
N_XHEADS = 4
N_SPLIT = 3
ONE_LANE = LANES - 1
F32_EXP_UNDERFLOW = 104.0
BOUND_SLACK = 1.0
VMEM_LIMIT = 48 * 1024 * 1024

ROW_TILE = 512
ATT_TILE = 512
FF_TILE = 1408


def _rms(x, g):
    ms = jnp.mean(x * x, axis=-1, keepdims=True)
    return x * lax.rsqrt(ms + EPS) * g


def _split3(x):
    hi = x.astype(BF16).astype(F32)
    r = x - hi
    mid = r.astype(BF16).astype(F32)
    lo = (r - mid).astype(BF16).astype(F32)
    return hi, mid, lo


def _ffn_kernel(x_ref, g_ref, wg_ref, wu_ref, wd_ref, gf_ref, o_ref, h_scr, acc_scr,
                *, n_ff, final_norm):
    f = pl.program_id(1)

    @pl.when(f == 0)
    def _():
        h_scr[...] = _rms(x_ref[...], g_ref[...]).astype(BF16)
        acc_scr[...] = jnp.zeros_like(acc_scr)

    h = h_scr[...]
    gate = jnp.dot(h, wg_ref[...], preferred_element_type=F32)
    up = jnp.dot(h, wu_ref[...], preferred_element_type=F32)
    act = (gate * (1.0 / (1.0 + jnp.exp(-gate))) * up).astype(BF16)
    acc_scr[...] += jnp.dot(act, wd_ref[...], preferred_element_type=F32)

    @pl.when(f == n_ff - 1)
    def _():
        y = x_ref[...] + 0.5 * acc_scr[...]
        if final_norm:
            y = _rms(y, gf_ref[...])
        o_ref[...] = y


def _ffn(x, g, w_gu, w_down, g_final, *, final_norm):
    s, d = x.shape
    d_ff = w_down.shape[0]
    tm, tf = ROW_TILE, FF_TILE
    n_ff = d_ff // tf
    assert s % tm == 0 and d_ff % tf == 0
    return pl.pallas_call(
        functools.partial(_ffn_kernel, n_ff=n_ff, final_norm=final_norm),
        grid=(s // tm, n_ff),
        in_specs=[
            pl.BlockSpec((tm, d), lambda i, f: (i, 0)),
            pl.BlockSpec((1, d), lambda i, f: (0, 0)),
            pl.BlockSpec((d, tf), lambda i, f: (0, f)),
            pl.BlockSpec((d, tf), lambda i, f: (0, f + n_ff)),
            pl.BlockSpec((tf, d), lambda i, f: (f, 0)),
            pl.BlockSpec((1, d), lambda i, f: (0, 0)),
        ],
        out_specs=pl.BlockSpec((tm, d), lambda i, f: (i, 0)),
        out_shape=jax.ShapeDtypeStruct((s, d), F32),
        scratch_shapes=[pltpu.VMEM((tm, d), BF16), pltpu.VMEM((tm, d), F32)],
        compiler_params=pltpu.CompilerParams(
            dimension_semantics=("arbitrary", "arbitrary"), vmem_limit_bytes=VMEM_LIMIT),
        name="ffn",
    )(x, g, w_gu, w_gu, w_down, g_final)


def _mix_kernel(x_ref, g_ref, wbcv_ref, wq_ref, wk_ref, wv_ref, wf_ref, bf_ref, wconv_ref,
                gconv_ref, tri_ref, eq_ref, ek_ref, vone_ref,
                ycn_ref, q_ref, k_ref, v_ref, c_ref, u_scr, carry_scr):
    i = pl.program_id(0)
    tm = x_ref.shape[0]
    cw = CONV_WIDTH

    @pl.when(i == 0)
    def _():
        u_scr[0:8, :] = jnp.zeros((8, cw), F32)
        carry_scr[...] = jnp.zeros_like(carry_scr)

    h = _rms(x_ref[...], g_ref[...]).astype(BF16)

    z = jnp.dot(h, wbcv_ref[...], preferred_element_type=F32)
    u = z[:, cw:2 * cw] * z[:, 2 * cw:3 * cw]
    u_scr[8:tm + 8, :] = u
    u1 = u_scr[7:tm + 7, :]
    u2 = u_scr[6:tm + 6, :]
    w = wconv_ref[...]
    conv = w[2:3, :] * u + w[1:2, :] * u1 + w[0:1, :] * u2
    u_scr[0:8, :] = u[tm - 8:tm, :]
    ycn_ref[...] = _rms(z[:, 0:cw] * conv, gconv_ref[...]).astype(BF16)

    zf = jnp.dot(h, wf_ref[...], preferred_element_type=F32) + bf_ref[...]
    logf = jnp.minimum(zf, 0.0) - jnp.log1p(jnp.exp(-jnp.abs(zf)))
    tri = tri_ref[...]
    c = carry_scr[7:8, :]
    for piece in _split3(logf):
        c = c + jnp.dot(tri, piece.astype(BF16), preferred_element_type=F32)
    carry_scr[...] = c[tm - 8:tm, :]
    c_ref[...] = c

    hi, mid, lo = _split3(c)
    lane = lax.broadcasted_iota(jnp.int32, c.shape, 1)
    sel = lane % N_SPLIT
    pieces = jnp.where(sel == 0, hi, jnp.where(sel == 1, mid, lo))
    pieces = jnp.where(lane == ONE_LANE, 1.0, pieces).astype(BF16)

    scale = HEAD_DIM ** -0.5
    q = jnp.dot(h, wq_ref[...], preferred_element_type=F32) * scale
    q_ref[...] = (q + jnp.dot(pieces, eq_ref[...], preferred_element_type=F32)).astype(BF16)
    k = jnp.dot(h, wk_ref[...], preferred_element_type=F32)
    k_ref[...] = (k + jnp.dot(pieces, ek_ref[...], preferred_element_type=F32)).astype(BF16)
    v = jnp.dot(h, wv_ref[...], preferred_element_type=F32)
    v_ref[...] = (v + vone_ref[...]).astype(BF16)


def _bias_placement():
    eq = np.zeros((LANES, N_HEADS * LANES), np.float32)
    ek = np.zeros((LANES, N_HEADS * LANES), np.float32)
    vone = np.zeros((1, N_HEADS * LANES), np.float32)
    for h in range(N_HEADS):
        base = h * LANES + HEAD_DIM
        for i in range(N_SPLIT):
            eq[N_SPLIT * h + i, base + i] = 1.0
            ek[ONE_LANE, base + i] = 1.0
            eq[ONE_LANE, base + N_SPLIT + i] = 1.0
            ek[N_SPLIT * h + i, base + N_SPLIT + i] = -1.0
        vone[0, base] = 1.0
    return jnp.asarray(eq, BF16), jnp.asarray(ek, BF16), jnp.asarray(vone, F32)


def _mix(x, g, wbcv, wq, wk, wv, wf, bfr, wconv, gconv):
    s, d = x.shape
    tm = ROW_TILE
    hp = N_HEADS * LANES
    tri = jnp.asarray(np.tril(np.ones((tm, tm), np.float32)), BF16)
    eq, ek, vone = _bias_placement()
    const = lambda shape: pl.BlockSpec(shape, lambda i: (0, 0))
    row = lambda width: pl.BlockSpec((tm, width), lambda i: (i, 0))
    return pl.pallas_call(
        _mix_kernel,
        grid=(s // tm,),
        in_specs=[
            row(d), const((1, d)), const((d, 3 * CONV_WIDTH)), const((d, hp)), const((d, hp)),
            const((d, hp)), const((d, LANES)), const((1, LANES)), const((CONV_K, CONV_WIDTH)),
            const((1, CONV_WIDTH)), const((tm, tm)), const((LANES, hp)), const((LANES, hp)),
            const((1, hp)),
        ],
        out_specs=[row(CONV_WIDTH), row(hp), row(hp), row(hp), row(LANES)],
        out_shape=[
            jax.ShapeDtypeStruct((s, CONV_WIDTH), BF16),
            jax.ShapeDtypeStruct((s, hp), BF16),
            jax.ShapeDtypeStruct((s, hp), BF16),
            jax.ShapeDtypeStruct((s, hp), BF16),
            jax.ShapeDtypeStruct((s, LANES), F32),
        ],
        scratch_shapes=[pltpu.VMEM((tm + 8, CONV_WIDTH), F32), pltpu.VMEM((8, LANES), F32)],
        compiler_params=pltpu.CompilerParams(
            dimension_semantics=("arbitrary",), vmem_limit_bytes=VMEM_LIMIT),
        name="mix_in",
    )(x, g, wbcv, wq, wk, wv, wf, bfr, wconv, gconv, tri, eq, ek, vone)


def _max_row_norm2(x):
    lane = lax.broadcasted_iota(jnp.int32, x.shape, 1)
    xf = jnp.where(lane < HEAD_DIM, x.astype(F32), 0.0)
    n2 = jnp.sum(xf * xf, axis=-1, keepdims=True)
    return jnp.max(n2, axis=0, keepdims=True)


def _att_kernel(cst_ref, cen_ref, q_ref, k_ref, v_ref, o_ref, m_scr, acc_scr, kn_scr):
    hd = pl.program_id(0)
    i = pl.program_id(1)
    t = q_ref.shape[0]
    q = q_ref[...]
    nt = (((1,), (1,)), ((), ()))

    @pl.when(i == 0)
    def _():
        def kb(n, mx):
            return jnp.maximum(mx, _max_row_norm2(k_ref[pl.ds(pl.multiple_of(n * t, t), t), :]))
        kn2 = lax.fori_loop(0, k_ref.shape[0] // t, kb, jnp.zeros((1, 1), F32))
        kn_scr[...] = jnp.broadcast_to(kn2, kn_scr.shape)

    qk_bound = jnp.sqrt(_max_row_norm2(q) * kn_scr[0:1, 0:1])[0, 0]
    floor = -(F32_EXP_UNDERFLOW + BOUND_SLACK + 2.0 * qk_bound)
    c_first = cst_ref[hd, i]

    def needed(j):
        return jnp.logical_and(j > 0, c_first - cen_ref[hd, jnp.maximum(j - 1, 0)] >= floor)

    j_lo = lax.while_loop(needed, lambda j: j - 1, i)

    def scores(j):
        start = pl.multiple_of(j * t, t)
        kb = k_ref[pl.ds(start, t), :]
        return lax.dot_general(q, kb, nt, preferred_element_type=F32), start

    s, start = scores(i)
    row = lax.broadcasted_iota(jnp.int32, (t, t), 0)
    col = lax.broadcasted_iota(jnp.int32, (t, t), 1)
    s = jnp.where(row >= col, s, -jnp.inf)
    m = jnp.max(s, axis=-1, keepdims=True)
    p = jnp.exp(s - m).astype(BF16)
    m_scr[...] = m
    acc_scr[...] = jnp.dot(p, v_ref[pl.ds(start, t), :], preferred_element_type=F32)

    def body(n, carry):
        s, start = scores(i - 1 - n)
        m_old = m_scr[...]
        m_new = jnp.maximum(m_old, jnp.max(s, axis=-1, keepdims=True))
        p = jnp.exp(s - m_new).astype(BF16)
        m_scr[...] = m_new
        acc_scr[...] = jnp.exp(m_old - m_new) * acc_scr[...] + jnp.dot(
            p, v_ref[pl.ds(start, t), :], preferred_element_type=F32)
        return carry

    lax.fori_loop(0, i - j_lo, body, 0)

    acc = acc_scr[...]
    denom = acc[:, HEAD_DIM:HEAD_DIM + 1]
    lane = lax.broadcasted_iota(jnp.int32, acc.shape, 1)
    o_ref[...] = jnp.where(lane < HEAD_DIM, acc / denom, 0.0)


def _attention(q, k, v, c):
    s = q.shape[0]
    t = ATT_TILE
    c_heads = c[:, 0:N_SPLIT * N_HEADS:N_SPLIT]
    c_first = c_heads[0::t].T
    c_last = c_heads[t - 1::t].T
    grid_spec = pltpu.PrefetchScalarGridSpec(
        num_scalar_prefetch=2,
        grid=(N_HEADS, s // t),
        in_specs=[
            pl.BlockSpec((t, LANES), lambda h, i, *_: (i, h)),
            pl.BlockSpec((s, LANES), lambda h, i, *_: (0, h)),
            pl.BlockSpec((s, LANES), lambda h, i, *_: (0, h)),
        ],
        out_specs=pl.BlockSpec((t, LANES), lambda h, i, *_: (i, h)),
        scratch_shapes=[pltpu.VMEM((t, 1), F32), pltpu.VMEM((t, LANES), F32),
                        pltpu.VMEM((8, LANES), F32)],
    )
    return pl.pallas_call(
        _att_kernel,
        grid_spec=grid_spec,
        out_shape=jax.ShapeDtypeStruct((s, N_HEADS * LANES), F32),
        compiler_params=pltpu.CompilerParams(
            dimension_semantics=("arbitrary", "arbitrary"), vmem_limit_bytes=VMEM_LIMIT),
        name="fox_attention",
    )(c_first, c_last, q, k, v)


def _memkv_kernel(mem_ref, g_ref, w_ref, k_ref, v_ref):
    d = mem_ref.shape[1]
    m = _rms(mem_ref[...], g_ref[...]).astype(BF16)
    kv = jnp.dot(m, w_ref[...], preferred_element_type=F32)
    k_ref[...] = kv[:, :d].astype(BF16)
    v_ref[...] = kv[:, d:].astype(BF16)


def _memkv(mem, g, w_kv):
    n, d = mem.shape
    return pl.pallas_call(
        _memkv_kernel,
        out_shape=[jax.ShapeDtypeStruct((n, d), BF16), jax.ShapeDtypeStruct((n, d), BF16)],
        compiler_params=pltpu.CompilerParams(vmem_limit_bytes=VMEM_LIMIT),
        name="mem_kv",
    )(mem, g, w_kv)


def _out_xattn_kernel(x_ref, ycn_ref, ya_ref, gatt_ref, woc_ref, woa_ref, gx_ref, wxq_ref,
                      kx_ref, vx_ref, wxo_ref, o_ref):
    d = x_ref.shape[1]
    xd = d // N_XHEADS
    ya = ya_ref[...]
    ms = jnp.sum(ya * ya, axis=-1, keepdims=True) * (1.0 / ATT_WIDTH)
    yan = (ya * lax.rsqrt(ms + EPS) * gatt_ref[...]).astype(BF16)
    x1 = (x_ref[...] + jnp.dot(ycn_ref[...], woc_ref[...], preferred_element_type=F32)
          + jnp.dot(yan, woa_ref[...], preferred_element_type=F32))

    hx = _rms(x1, gx_ref[...]).astype(BF16)
    q = (jnp.dot(hx, wxq_ref[...], preferred_element_type=F32) * (xd ** -0.5)).astype(BF16)
    nt = (((1,), (1,)), ((), ()))
    heads = []
    for hh in range(N_XHEADS):
        sl = slice(hh * xd, (hh + 1) * xd)
        s = lax.dot_general(q[:, sl], kx_ref[:, sl], nt, preferred_element_type=F32)
        p = jnp.exp(s - jnp.max(s, axis=-1, keepdims=True))
        denom = jnp.sum(p, axis=-1, keepdims=True)
        o = jnp.dot(p.astype(BF16), vx_ref[:, sl], preferred_element_type=F32) / denom
        heads.append(o.astype(BF16))
    o = jnp.concatenate(heads, axis=-1)
    o_ref[...] = x1 + jnp.dot(o, wxo_ref[...], preferred_element_type=F32)


def _out_xattn(x, ycn, ya, gatt, woc, woa, gx, wxq, kx, vx, wxo):
    s, d = x.shape
    tm = ROW_TILE
    n_mem = kx.shape[0]
    hp = N_HEADS * LANES
    const = lambda shape: pl.BlockSpec(shape, lambda i: (0, 0))
    row = lambda width: pl.BlockSpec((tm, width), lambda i: (i, 0))
    return pl.pallas_call(
        _out_xattn_kernel,
        grid=(s // tm,),
        in_specs=[
            row(d), row(CONV_WIDTH), row(hp), const((1, hp)), const((CONV_WIDTH, d)),
            const((hp, d)), const((1, d)), const((d, d)), const((n_mem, d)), const((n_mem, d)),
            const((d, d)),
        ],
        out_specs=row(d),
        out_shape=jax.ShapeDtypeStruct((s, d), F32),
        compiler_params=pltpu.CompilerParams(
            dimension_semantics=("arbitrary",), vmem_limit_bytes=VMEM_LIMIT),
        name="out_xattn",
    )(x, ycn, ya, gatt, woc, woa, gx, wxq, kx, vx, wxo)


def _pad_heads_cols(w):
    d = w.shape[0]
    w = w.reshape(d, N_HEADS, HEAD_DIM)
    w = jnp.pad(w, ((0, 0), (0, 0), (0, LANES - HEAD_DIM)))
    return w.reshape(d, N_HEADS * LANES)


def kernel(x, mem, g_ffn1, w_ffn1_gu, w_ffn1_down, g_mix, w_mix_in, w_conv, b_f, g_conv_out,
           g_att_out, w_mix_out, g_xattn, g_mem, w_xq, w_xkv, w_xo, g_ffn2, w_ffn2_gu,
           w_ffn2_down, g_final):
    b, s, d = x.shape
    assert b == 1
    depth = g_ffn1.shape[0]
    xs = x.reshape(s, d)
    mem2 = mem.reshape(mem.shape[1], d)
    row = lambda a: a.reshape(1, -1).astype(F32)
    cw, aw = CONV_WIDTH, ATT_WIDTH
    gfin = row(g_final)

    for l in range(depth):
        xs = _ffn(xs, row(g_ffn1[l]), w_ffn1_gu[l].astype(BF16), w_ffn1_down[l].astype(BF16),
                  gfin, final_norm=False)

        w_in = w_mix_in[l]
        wbcv = w_in[:, :3 * cw].astype(BF16)
        wq = _pad_heads_cols(w_in[:, 3 * cw:3 * cw + aw]).astype(BF16)
        wk = _pad_heads_cols(w_in[:, 3 * cw + aw:3 * cw + 2 * aw]).astype(BF16)
        wv = _pad_heads_cols(w_in[:, 3 * cw + 2 * aw:3 * cw + 3 * aw]).astype(BF16)
        wf = jnp.repeat(w_in[:, 3 * cw + 3 * aw:], N_SPLIT, axis=1)
        wf = jnp.pad(wf, ((0, 0), (0, LANES - N_SPLIT * N_HEADS))).astype(BF16)
        bfr = jnp.pad(jnp.repeat(b_f[l], N_SPLIT), (0, LANES - N_SPLIT * N_HEADS)).reshape(1, LANES)
        ycn, q, k, v, c = _mix(xs, row(g_mix[l]), wbcv, wq, wk, wv, wf, bfr.astype(F32),
                               w_conv[l].astype(F32), row(g_conv_out[l]))

        ya = _attention(q, k, v, c)

        w_out = w_mix_out[l]
        woc = w_out[:cw].astype(BF16)
        woa = jnp.pad(w_out[cw:].reshape(N_HEADS, HEAD_DIM, d),
                      ((0, 0), (0, LANES - HEAD_DIM), (0, 0))).reshape(N_HEADS * LANES, d).astype(BF16)
        gatt = jnp.pad(g_att_out[l].reshape(N_HEADS, HEAD_DIM),
                       ((0, 0), (0, LANES - HEAD_DIM))).reshape(1, N_HEADS * LANES).astype(F32)
        kx, vx = _memkv(mem2, row(g_mem[l]), w_xkv[l].astype(BF16))
        xs = _out_xattn(xs, ycn, ya, gatt, woc, woa, row(g_xattn[l]), w_xq[l].astype(BF16),
                        kx, vx, w_xo[l].astype(BF16))

        xs = _ffn(xs, row(g_ffn2[l]), w_ffn2_gu[l].astype(BF16), w_ffn2_down[l].astype(BF16),
                  gfin, final_norm=(l == depth - 1))
    return xs.reshape(b, s, d)
```

```python
import functools

import numpy as np
import jax
import jax.numpy as jnp
from jax import lax
from jax.experimental import pallas as pl
from jax.experimental.pallas import tpu as pltpu

F32 = jnp.float32
BF16 = jnp.bfloat16

EPS = 1e-6
LANES = 128
HEAD_DIM = 64
N_HEADS = 8
CONV_WIDTH = 512
ATT_WIDTH = 512
CONV_K = 3
N_XHEADS = 4
N_SPLIT = 3
ONE_LANE = LANES - 1
F32_EXP_UNDERFLOW = 104.0
BOUND_SLACK = 1.0
LOG2E = 1.4426950408889634
NO_SHIFT_BOUND = 40.0 * LOG2E
VMEM_LIMIT = 48 * 1024 * 1024

ROW_TILE = 512
ATT_TILE = 512
FF_TILE = 1408


def _rms(x, g):
    ms = jnp.mean(x * x, axis=-1, keepdims=True)
    return x * lax.rsqrt(ms + EPS) * g


def _split3(x):
    hi = x.astype(BF16).astype(F32)
    r = x - hi
    mid = r.astype(BF16).astype(F32)
    lo = (r - mid).astype(BF16).astype(F32)
    return hi, mid, lo


def _ffn_kernel(x_ref, g_ref, wg_ref, wu_ref, wd_ref, gf_ref, o_ref, h_scr, acc_scr,
                *, n_ff, final_norm):
    f = pl.program_id(1)

    @pl.when(f == 0)
    def _():
        h_scr[...] = _rms(x_ref[...], g_ref[...]).astype(BF16)
        acc_scr[...] = jnp.zeros_like(acc_scr)

    h = h_scr[...]
    gate = jnp.dot(h, wg_ref[...], preferred_element_type=F32)
    up = jnp.dot(h, wu_ref[...], preferred_element_type=F32)
    act = (gate * (1.0 / (1.0 + jnp.exp(-gate))) * up).astype(BF16)
    acc_scr[...] += jnp.dot(act, wd_ref[...], preferred_element_type=F32)

    @pl.when(f == n_ff - 1)
    def _():
        y = x_ref[...] + 0.5 * acc_scr[...]
        if final_norm:
            y = _rms(y, gf_ref[...])
        o_ref[...] = y


def _ffn(x, g, w_gu, w_down, g_final, *, final_norm):
    s, d = x.shape
    d_ff = w_down.shape[0]
    tm, tf = ROW_TILE, FF_TILE
    n_ff = d_ff // tf
    assert s % tm == 0 and d_ff % tf == 0
    return pl.pallas_call(
        functools.partial(_ffn_kernel, n_ff=n_ff, final_norm=final_norm),
        grid=(s // tm, n_ff),
        in_specs=[
            pl.BlockSpec((tm, d), lambda i, f: (i, 0)),
            pl.BlockSpec((1, d), lambda i, f: (0, 0)),
            pl.BlockSpec((d, tf), lambda i, f: (0, f)),
            pl.BlockSpec((d, tf), lambda i, f: (0, f + n_ff)),
            pl.BlockSpec((tf, d), lambda i, f: (f, 0)),
            pl.BlockSpec((1, d), lambda i, f: (0, 0)),
        ],
        out_specs=pl.BlockSpec((tm, d), lambda i, f: (i, 0)),
        out_shape=jax.ShapeDtypeStruct((s, d), F32),
        scratch_shapes=[pltpu.VMEM((tm, d), BF16), pltpu.VMEM((tm, d), F32)],
        compiler_params=pltpu.CompilerParams(
            dimension_semantics=("arbitrary", "arbitrary"), vmem_limit_bytes=VMEM_LIMIT),
        name="ffn",
    )(x, g, w_gu, w_gu, w_down, g_final)


def _mix_kernel(x_ref, g_ref, wbcv_ref, wq_ref, wk_ref, wv_ref, wf_ref, bf_ref, wconv_ref,
                gconv_ref, tri_ref, eq_ref, ek_ref, vone_ref,
                ycn_ref, qt_ref, k_ref, vt_ref, c_ref, u_scr, carry_scr):
    i = pl.program_id(0)
    tm = x_ref.shape[0]
    cw = CONV_WIDTH

    @pl.when(i == 0)
    def _():
        u_scr[0:8, :] = jnp.zeros((8, cw), F32)
        carry_scr[...] = jnp.zeros_like(carry_scr)

    h = _rms(x_ref[...], g_ref[...]).astype(BF16)

    z = jnp.dot(h, wbcv_ref[...], preferred_element_type=F32)
    u = z[:, cw:2 * cw] * z[:, 2 * cw:3 * cw]
    u_scr[8:tm + 8, :] = u
    u1 = u_scr[7:tm + 7, :]
    u2 = u_scr[6:tm + 6, :]
    w = wconv_ref[...]
    conv = w[2:3, :] * u + w[1:2, :] * u1 + w[0:1, :] * u2
    u_scr[0:8, :] = u[tm - 8:tm, :]
    ycn_ref[...] = _rms(z[:, 0:cw] * conv, gconv_ref[...]).astype(BF16)

    zf = jnp.dot(h, wf_ref[...], preferred_element_type=F32) + bf_ref[...]
    logf = jnp.minimum(zf, 0.0) - jnp.log1p(jnp.exp(-jnp.abs(zf)))
    tri = tri_ref[...]
    c = carry_scr[7:8, :]
    for piece in _split3(logf):
        c = c + jnp.dot(tri, piece.astype(BF16), preferred_element_type=F32)
    carry_scr[...] = c[tm - 8:tm, :]
    c2 = c * LOG2E
    c_ref[...] = c2

    hi, mid, lo = _split3(c2)
    lane = lax.broadcasted_iota(jnp.int32, c.shape, 1)
    sel = lane % N_SPLIT
    pieces = jnp.where(sel == 0, hi, jnp.where(sel == 1, mid, lo))
    pieces = jnp.where(lane == ONE_LANE, 1.0, pieces).astype(BF16)

    scale = HEAD_DIM ** -0.5 * LOG2E
    q = jnp.dot(h, wq_ref[...], preferred_element_type=F32) * scale
    q = q + jnp.dot(pieces, eq_ref[...], preferred_element_type=F32)
    qt_ref[...] = jnp.transpose(q).astype(BF16)
    k = jnp.dot(h, wk_ref[...], preferred_element_type=F32)
    k_ref[...] = (k + jnp.dot(pieces, ek_ref[...], preferred_element_type=F32)).astype(BF16)
    v = jnp.dot(h, wv_ref[...], preferred_element_type=F32)
    vt_ref[...] = jnp.transpose(v + vone_ref[...]).astype(BF16)


def _bias_placement():
    eq = np.zeros((LANES, N_HEADS * LANES), np.float32)
    ek = np.zeros((LANES, N_HEADS * LANES), np.float32)
    vone = np.zeros((1, N_HEADS * LANES), np.float32)
    for h in range(N_HEADS):
        base = h * LANES + HEAD_DIM
        for i in range(N_SPLIT):
            eq[N_SPLIT * h + i, base + i] = 1.0
            ek[ONE_LANE, base + i] = 1.0
            eq[ONE_LANE, base + N_SPLIT + i] = 1.0
            ek[N_SPLIT * h + i, base + N_SPLIT + i] = -1.0
        vone[0, base] = 1.0
    return jnp.asarray(eq, BF16), jnp.asarray(ek, BF16), jnp.asarray(vone, F32)


def _mix(x, g, wbcv, wq, wk, wv, wf, bfr, wconv, gconv):
    s, d = x.shape
    tm = ROW_TILE
    hp = N_HEADS * LANES
    tri = jnp.asarray(np.tril(np.ones((tm, tm), np.float32)), BF16)
    eq, ek, vone = _bias_placement()
    const = lambda shape: pl.BlockSpec(shape, lambda i: (0, 0))
    row = lambda width: pl.BlockSpec((tm, width), lambda i: (i, 0))
    col = lambda height: pl.BlockSpec((height, tm), lambda i: (0, i))
    return pl.pallas_call(
        _mix_kernel,
        grid=(s // tm,),
        in_specs=[
            row(d), const((1, d)), const((d, 3 * CONV_WIDTH)), const((d, hp)), const((d, hp)),
            const((d, hp)), const((d, LANES)), const((1, LANES)), const((CONV_K, CONV_WIDTH)),
            const((1, CONV_WIDTH)), const((tm, tm)), const((LANES, hp)), const((LANES, hp)),
            const((1, hp)),
        ],
        out_specs=[row(CONV_WIDTH), col(hp), row(hp), col(hp), row(LANES)],
        out_shape=[
            jax.ShapeDtypeStruct((s, CONV_WIDTH), BF16),
            jax.ShapeDtypeStruct((hp, s), BF16),
            jax.ShapeDtypeStruct((s, hp), BF16),
            jax.ShapeDtypeStruct((hp, s), BF16),
            jax.ShapeDtypeStruct((s, LANES), F32),
        ],
        scratch_shapes=[pltpu.VMEM((tm + 8, CONV_WIDTH), F32), pltpu.VMEM((8, LANES), F32)],
        compiler_params=pltpu.CompilerParams(
            dimension_semantics=("arbitrary",), vmem_limit_bytes=VMEM_LIMIT),
        name="mix_in",
    )(x, g, wbcv, wq, wk, wv, wf, bfr, wconv, gconv, tri, eq, ek, vone)


def _max_row_norm2(x):
    lane = lax.broadcasted_iota(jnp.int32, x.shape, 1)
    xf = jnp.where(lane < HEAD_DIM, x.astype(F32), 0.0)
    n2 = jnp.sum(xf * xf, axis=-1, keepdims=True)
    return jnp.max(n2, axis=0, keepdims=True)


def _att_kernel(cst_ref, cen_ref, qt_ref, k_ref, vt_ref, tri_ref, o_ref, m_scr, acc_scr, kn_scr):
    hd = pl.program_id(0)
    i = pl.program_id(1)
    t = qt_ref.shape[1]
    qt = qt_ref[...]

    @pl.when(i == 0)
    def _():
        def kb(n, mx):
            return jnp.maximum(mx, _max_row_norm2(k_ref[pl.ds(pl.multiple_of(n * t, t), t), :]))
        kn2 = lax.fori_loop(0, k_ref.shape[0] // t, kb, jnp.zeros((1, 1), F32))
        kn_scr[...] = jnp.broadcast_to(kn2, kn_scr.shape)

    sub = lax.broadcasted_iota(jnp.int32, qt.shape, 0)
    qf = jnp.where(sub < HEAD_DIM, qt.astype(F32), 0.0)
    qn2 = jnp.max(jnp.sum(qf * qf, axis=0, keepdims=True), axis=1, keepdims=True)
    qk_bound = jnp.sqrt(qn2 * kn_scr[0:1, 0:1])[0, 0]
    floor = -((F32_EXP_UNDERFLOW + BOUND_SLACK) * LOG2E + 2.0 * qk_bound)
    c_first = cst_ref[hd, i]

    def needed(j):
        return jnp.logical_and(j > 0, c_first - cen_ref[hd, jnp.maximum(j - 1, 0)] >= floor)

    j_lo = lax.while_loop(needed, lambda j: j - 1, jnp.maximum(i - 1, 0))
    n_rest = jnp.maximum(i - 1, 0) - j_lo

    def scores(start, size):
        return jnp.dot(k_ref[pl.ds(start, size), :], qt, preferred_element_type=F32)

    def values(start, size, p):
        return jnp.dot(vt_ref[:, pl.ds(start, size)], p, preferred_element_type=F32)

    def run(first, rest):
        @pl.when(i == 0)
        def _():
            first(scores(0, t) + tri_ref[...], 0, t)

        @pl.when(i > 0)
        def _():
            start = pl.multiple_of((i - 1) * t, t)
            s = scores(start, 2 * t)
            first(jnp.concatenate([s[:t], s[t:] + tri_ref[...]], axis=0), start, 2 * t)

        def body(n, carry):
            start = pl.multiple_of((i - 2 - n) * t, t)
            rest(scores(start, t), start, t)
            return carry

        lax.fori_loop(0, n_rest, body, 0)

    unshifted = qk_bound <= NO_SHIFT_BOUND

    @pl.when(unshifted)
    def _():
        def first(s, start, size):
            acc_scr[...] = values(start, size, jnp.exp2(s).astype(BF16))

        def rest(s, start, size):
            acc_scr[...] += values(start, size, jnp.exp2(s).astype(BF16))

        run(first, rest)

    @pl.when(jnp.logical_not(unshifted))
    def _():
        def first(s, start, size):
            m = jnp.max(s, axis=0, keepdims=True)
            m_scr[...] = m
            acc_scr[...] = values(start, size, jnp.exp2(s - m).astype(BF16))

        def rest(s, start, size):
            m_old = m_scr[...]
            m_new = jnp.maximum(m_old, jnp.max(s, axis=0, keepdims=True))
            m_scr[...] = m_new
            acc_scr[...] = jnp.exp2(m_old - m_new) * acc_scr[...] + values(
                start, size, jnp.exp2(s - m_new).astype(BF16))

        run(first, rest)

    acc = acc_scr[...]
    out = jnp.transpose(acc / acc[HEAD_DIM:HEAD_DIM + 1, :])
    lane = lax.broadcasted_iota(jnp.int32, out.shape, 1)
    o_ref[...] = jnp.where(lane < HEAD_DIM, out, 0.0)


def _attention(qt, k, vt, c):
    s = k.shape[0]
    t = ATT_TILE
    c_heads = c[:, 0:N_SPLIT * N_HEADS:N_SPLIT]
    c_first = c_heads[0::t].T
    c_last = c_heads[t - 1::t].T
    tri = jnp.asarray(np.where(np.arange(t)[:, None] <= np.arange(t)[None, :], 0.0, -np.inf), F32)
    grid_spec = pltpu.PrefetchScalarGridSpec(
        num_scalar_prefetch=2,
        grid=(N_HEADS, s // t),
        in_specs=[
            pl.BlockSpec((LANES, t), lambda h, i, *_: (h, i)),
            pl.BlockSpec((s, LANES), lambda h, i, *_: (0, h)),
            pl.BlockSpec((LANES, s), lambda h, i, *_: (h, 0)),
            pl.BlockSpec((t, t), lambda h, i, *_: (0, 0)),
        ],
        out_specs=pl.BlockSpec((t, LANES), lambda h, i, *_: (i, h)),
        scratch_shapes=[pltpu.VMEM((1, t), F32), pltpu.VMEM((LANES, t), F32),
                        pltpu.VMEM((8, LANES), F32)],
    )
    return pl.pallas_call(
        _att_kernel,
        grid_spec=grid_spec,
        out_shape=jax.ShapeDtypeStruct((s, N_HEADS * LANES), F32),
        compiler_params=pltpu.CompilerParams(
            dimension_semantics=("arbitrary", "arbitrary"), vmem_limit_bytes=VMEM_LIMIT),
        name="fox_attention",
    )(c_first, c_last, qt, k, vt, tri)


def _memkv_kernel(mem_ref, g_ref, w_ref, k_ref, v_ref):
    d = mem_ref.shape[1]
    m = _rms(mem_ref[...], g_ref[...]).astype(BF16)
    kv = jnp.dot(m, w_ref[...], preferred_element_type=F32)
    k_ref[...] = kv[:, :d].astype(BF16)
    v_ref[...] = kv[:, d:].astype(BF16)


def _memkv(mem, g, w_kv):
    n, d = mem.shape
    return pl.pallas_call(
        _memkv_kernel,
        out_shape=[jax.ShapeDtypeStruct((n, d), BF16), jax.ShapeDtypeStruct((n, d), BF16)],
        compiler_params=pltpu.CompilerParams(vmem_limit_bytes=VMEM_LIMIT),
        name="mem_kv",
    )(mem, g, w_kv)


def _out_xattn_kernel(x_ref, ycn_ref, ya_ref, gatt_ref, woc_ref, woa_ref, gx_ref, wxq_ref,
                      kx_ref, vx_ref, wxo_ref, o_ref):
    d = x_ref.shape[1]
    xd = d // N_XHEADS
    ya = ya_ref[...]
    ms = jnp.sum(ya * ya, axis=-1, keepdims=True) * (1.0 / ATT_WIDTH)
    yan = (ya * lax.rsqrt(ms + EPS) * gatt_ref[...]).astype(BF16)
    x1 = (x_ref[...] + jnp.dot(ycn_ref[...], woc_ref[...], preferred_element_type=F32)
          + jnp.dot(yan, woa_ref[...], preferred_element_type=F32))

    hx = _rms(x1, gx_ref[...]).astype(BF16)
    q = (jnp.dot(hx, wxq_ref[...], preferred_element_type=F32) * (xd ** -0.5)).astype(BF16)
    nt = (((1,), (1,)), ((), ()))
    heads = []
    for hh in range(N_XHEADS):
        sl = slice(hh * xd, (hh + 1) * xd)
        s = lax.dot_general(q[:, sl], kx_ref[:, sl], nt, preferred_element_type=F32)
        p = jnp.exp(s - jnp.max(s, axis=-1, keepdims=True))
        denom = jnp.sum(p, axis=-1, keepdims=True)
        o = jnp.dot(p.astype(BF16), vx_ref[:, sl], preferred_element_type=F32) / denom
        heads.append(o.astype(BF16))
    o = jnp.concatenate(heads, axis=-1)
    o_ref[...] = x1 + jnp.dot(o, wxo_ref[...], preferred_element_type=F32)


def _out_xattn(x, ycn, ya, gatt, woc, woa, gx, wxq, kx, vx, wxo):
    s, d = x.shape
    tm = ROW_TILE
    n_mem = kx.shape[0]
    hp = N_HEADS * LANES
    const = lambda shape: pl.BlockSpec(shape, lambda i: (0, 0))
    row = lambda width: pl.BlockSpec((tm, width), lambda i: (i, 0))
    return pl.pallas_call(
        _out_xattn_kernel,
        grid=(s // tm,),
        in_specs=[
            row(d), row(CONV_WIDTH), row(hp), const((1, hp)), const((CONV_WIDTH, d)),
            const((hp, d)), const((1, d)), const((d, d)), const((n_mem, d)), const((n_mem, d)),
            const((d, d)),
        ],
        out_specs=row(d),
        out_shape=jax.ShapeDtypeStruct((s, d), F32),
        compiler_params=pltpu.CompilerParams(
            dimension_semantics=("arbitrary",), vmem_limit_bytes=VMEM_LIMIT),
        name="out_xattn",
    )(x, ycn, ya, gatt, woc, woa, gx, wxq, kx, vx, wxo)


def _pad_heads_cols(w):
    d = w.shape[0]
    w = w.reshape(d, N_HEADS, HEAD_DIM)
    w = jnp.pad(w, ((0, 0), (0, 0), (0, LANES - HEAD_DIM)))
    return w.reshape(d, N_HEADS * LANES)


def kernel(x, mem, g_ffn1, w_ffn1_gu, w_ffn1_down, g_mix, w_mix_in, w_conv, b_f, g_conv_out,
           g_att_out, w_mix_out, g_xattn, g_mem, w_xq, w_xkv, w_xo, g_ffn2, w_ffn2_gu,
           w_ffn2_down, g_final):
    b, s, d = x.shape
    assert b == 1
    depth = g_ffn1.shape[0]
    xs = x.reshape(s, d)
    mem2 = mem.reshape(mem.shape[1], d)
    row = lambda a: a.reshape(1, -1).astype(F32)
    cw, aw = CONV_WIDTH, ATT_WIDTH
    gfin = row(g_final)

    for l in range(depth):
        xs = _ffn(xs, row(g_ffn1[l]), w_ffn1_gu[l].astype(BF16), w_ffn1_down[l].astype(BF16),
                  gfin, final_norm=False)

        w_in = w_mix_in[l]
        wbcv = w_in[:, :3 * cw].astype(BF16)
        wq = _pad_heads_cols(w_in[:, 3 * cw:3 * cw + aw]).astype(BF16)
        wk = _pad_heads_cols(w_in[:, 3 * cw + aw:3 * cw + 2 * aw]).astype(BF16)
        wv = _pad_heads_cols(w_in[:, 3 * cw + 2 * aw:3 * cw + 3 * aw]).astype(BF16)
        wf = jnp.repeat(w_in[:, 3 * cw + 3 * aw:], N_SPLIT, axis=1)
        wf = jnp.pad(wf, ((0, 0), (0, LANES - N_SPLIT * N_HEADS))).astype(BF16)
        bfr = jnp.pad(jnp.repeat(b_f[l], N_SPLIT), (0, LANES - N_SPLIT * N_HEADS)).reshape(1, LANES)
        ycn, q, k, v, c = _mix(xs, row(g_mix[l]), wbcv, wq, wk, wv, wf, bfr.astype(F32),
                               w_conv[l].astype(F32), row(g_conv_out[l]))

        ya = _attention(q, k, v, c)

        w_out = w_mix_out[l]
        woc = w_out[:cw].astype(BF16)
        woa = jnp.pad(w_out[cw:].reshape(N_HEADS, HEAD_DIM, d),
                      ((0, 0), (0, LANES - HEAD_DIM), (0, 0))).reshape(N_HEADS * LANES, d).astype(BF16)
        gatt = jnp.pad(g_att_out[l].reshape(N_HEADS, HEAD_DIM),
                       ((0, 0), (0, LANES - HEAD_DIM))).reshape(1, N_HEADS * LANES).astype(F32)
        kx, vx = _memkv(mem2, row(g_mem[l]), w_xkv[l].astype(BF16))
        xs = _out_xattn(xs, ycn, ya, gatt, woc, woa, row(g_xattn[l]), w_xq[l].astype(BF16),
                        kx, vx, w_xo[l].astype(BF16))

        xs = _ffn(xs, row(g_ffn2[l]), w_ffn2_gu[l].astype(BF16), w_ffn2_down[l].astype(BF16),
                  gfin, final_norm=(l == depth - 1))
    return xs.reshape(b, s, d)
```

```python
import functools

import numpy as np
import jax
import jax.numpy as jnp
from jax import lax
from jax.experimental import pallas as pl
from jax.experimental.pallas import tpu as pltpu

F32 = jnp.float32
BF16 = jnp.bfloat16

EPS = 1e-6
LANES = 128
MXU_DEPTH = 256
HEAD_DIM = 64
N_HEADS = 8
CONV_WIDTH = 512
ATT_WIDTH = 512
CONV_K = 3
N_XHEADS = 4
N_SPLIT = 3
ONE_LANE = LANES - 1
F32_EXP_UNDERFLOW = 104.0
BOUND_SLACK = 1.0
LOG2E = 1.4426950408889634
NO_SHIFT_BOUND = 40.0 * LOG2E
VMEM_LIMIT = 48 * 1024 * 1024

CAST_BLOCK_BYTES = 6 * 1024 * 1024
ROW_TILE = 512
ATT_TILE = 512


def _rms(x, g):
    ms = jnp.mean(x * x, axis=-1, keepdims=True)
    return x * lax.rsqrt(ms + EPS) * g


def _split3(x):
    hi = x.astype(BF16).astype(F32)
    r = x - hi
    mid = r.astype(BF16).astype(F32)
    lo = (r - mid).astype(BF16).astype(F32)
    return hi, mid, lo


def _ffn_kernel(x_ref, g_ref, wgu_ref, wd_ref, gf_ref, o_ref, *, chunks, final_norm):
    d_ff = wd_ref.shape[0]
    x = x_ref[...]
    h = _rms(x, g_ref[...]).astype(BF16)
    acc = None
    for lo, hi in chunks:
        gate = jnp.dot(h, wgu_ref[:, lo:hi], preferred_element_type=F32)
        up = jnp.dot(h, wgu_ref[:, d_ff + lo:d_ff + hi], preferred_element_type=F32)
        act = (gate * (1.0 / (1.0 + jnp.exp(-gate))) * up).astype(BF16)
        part = jnp.dot(act, wd_ref[lo:hi, :], preferred_element_type=F32)
        acc = part if acc is None else acc + part
    y = x + 0.5 * acc
    if final_norm:
        y = _rms(y, gf_ref[...])
    o_ref[...] = y


def _layer_weight(w, layer):
    return pl.BlockSpec((None,) + w.shape[1:], lambda i: (layer, 0, 0), pipeline_mode=pl.Buffered(1))


def _ffn(x, g, w_gu, w_down, layer, g_final, *, final_norm):
    s, d = x.shape
    d_ff = w_down.shape[1]
    tm = ROW_TILE
    assert s % tm == 0 and d_ff % MXU_DEPTH == 0
    split = (d_ff // MXU_DEPTH + 1) // 2 * MXU_DEPTH
    chunks = ((0, split), (split, d_ff))
    return pl.pallas_call(
        functools.partial(_ffn_kernel, chunks=chunks, final_norm=final_norm),
        grid=(s // tm,),
        in_specs=[
            pl.BlockSpec((tm, d), lambda i: (i, 0)),
            pl.BlockSpec((1, d), lambda i: (0, 0)),
            _layer_weight(w_gu, layer),
            _layer_weight(w_down, layer),
            pl.BlockSpec((1, d), lambda i: (0, 0)),
        ],
        out_specs=pl.BlockSpec((tm, d), lambda i: (i, 0)),
        out_shape=jax.ShapeDtypeStruct((s, d), F32),
        compiler_params=pltpu.CompilerParams(
            dimension_semantics=("arbitrary",), vmem_limit_bytes=VMEM_LIMIT),
        name="ffn",
    )(x, g, w_gu, w_down, g_final)


def _cast_kernel(w_ref, o_ref):
    o_ref[...] = w_ref[...].astype(o_ref.dtype)


def _to_bf16(w):
    layers, rows, cols = w.shape
    tr = max(r for r in range(16, rows + 1, 16)
             if rows % r == 0 and r * cols * 4 <= CAST_BLOCK_BYTES)
    spec = pl.BlockSpec((None, tr, cols), lambda a, b: (a, b, 0))
    return pl.pallas_call(
        _cast_kernel,
        grid=(layers, rows // tr),
        in_specs=[spec],
        out_specs=spec,
        out_shape=jax.ShapeDtypeStruct(w.shape, BF16),
        compiler_params=pltpu.CompilerParams(
            dimension_semantics=("arbitrary", "arbitrary"), vmem_limit_bytes=VMEM_LIMIT),
        name="cast_bf16",
    )(w)


def _mix_kernel(x_ref, g_ref, wbcv_ref, wq_ref, wk_ref, wv_ref, wf_ref, bf_ref, wconv_ref,
                gconv_ref, tri_ref, eq_ref, ek_ref, vone_ref,
                ycn_ref, qt_ref, k_ref, vt_ref, c_ref, u_scr, carry_scr):
    i = pl.program_id(0)
    tm = x_ref.shape[0]
    cw = CONV_WIDTH

    @pl.when(i == 0)
    def _():
        u_scr[0:8, :] = jnp.zeros((8, cw), F32)
        carry_scr[...] = jnp.zeros_like(carry_scr)

    h = _rms(x_ref[...], g_ref[...]).astype(BF16)

    z = jnp.dot(h, wbcv_ref[...], preferred_element_type=F32)
    u = z[:, cw:2 * cw] * z[:, 2 * cw:3 * cw]
    u_scr[8:tm + 8, :] = u
    u1 = u_scr[7:tm + 7, :]
    u2 = u_scr[6:tm + 6, :]
    w = wconv_ref[...]
    conv = w[2:3, :] * u + w[1:2, :] * u1 + w[0:1, :] * u2
    u_scr[0:8, :] = u[tm - 8:tm, :]
    ycn_ref[...] = _rms(z[:, 0:cw] * conv, gconv_ref[...]).astype(BF16)

    zf = jnp.dot(h, wf_ref[...], preferred_element_type=F32) + bf_ref[...]
    logf = jnp.minimum(zf, 0.0) - jnp.log1p(jnp.exp(-jnp.abs(zf)))
    tri = tri_ref[...]
    c = carry_scr[7:8, :]
    for piece in _split3(logf):
        c = c + jnp.dot(tri, piece.astype(BF16), preferred_element_type=F32)
    carry_scr[...] = c[tm - 8:tm, :]
    c2 = c * LOG2E
    c_ref[...] = c2

    hi, mid, lo = _split3(c2)
    lane = lax.broadcasted_iota(jnp.int32, c.shape, 1)
    sel = lane % N_SPLIT
    pieces = jnp.where(sel == 0, hi, jnp.where(sel == 1, mid, lo))
    pieces = jnp.where(lane == ONE_LANE, 1.0, pieces).astype(BF16)

    scale = HEAD_DIM ** -0.5 * LOG2E
    q = jnp.dot(h, wq_ref[...], preferred_element_type=F32) * scale
    q = q + jnp.dot(pieces, eq_ref[...], preferred_element_type=F32)
    qt_ref[...] = jnp.transpose(q).astype(BF16)
    k = jnp.dot(h, wk_ref[...], preferred_element_type=F32)
    k_ref[...] = (k + jnp.dot(pieces, ek_ref[...], preferred_element_type=F32)).astype(BF16)
    v = jnp.dot(h, wv_ref[...], preferred_element_type=F32)
    vt_ref[...] = jnp.transpose(v + vone_ref[...]).astype(BF16)


def _bias_placement():
    eq = np.zeros((LANES, N_HEADS * LANES), np.float32)
    ek = np.zeros((LANES, N_HEADS * LANES), np.float32)
    vone = np.zeros((1, N_HEADS * LANES), np.float32)
    for h in range(N_HEADS):
        base = h * LANES + HEAD_DIM
        for i in range(N_SPLIT):
            eq[N_SPLIT * h + i, base + i] = 1.0
            ek[ONE_LANE, base + i] = 1.0
            eq[ONE_LANE, base + N_SPLIT + i] = 1.0
            ek[N_SPLIT * h + i, base + N_SPLIT + i] = -1.0
        vone[0, base] = 1.0
    return jnp.asarray(eq, BF16), jnp.asarray(ek, BF16), jnp.asarray(vone, F32)


def _mix(x, g, wbcv, wq, wk, wv, wf, bfr, wconv, gconv):
    s, d = x.shape
    tm = ROW_TILE
    hp = N_HEADS * LANES
    tri = jnp.asarray(np.tril(np.ones((tm, tm), np.float32)), BF16)
    eq, ek, vone = _bias_placement()
    const = lambda shape: pl.BlockSpec(shape, lambda i: (0, 0))
    row = lambda width: pl.BlockSpec((tm, width), lambda i: (i, 0))
    col = lambda height: pl.BlockSpec((height, tm), lambda i: (0, i))
    return pl.pallas_call(
        _mix_kernel,
        grid=(s // tm,),
        in_specs=[
            row(d), const((1, d)), const((d, 3 * CONV_WIDTH)), const((d, hp)), const((d, hp)),
            const((d, hp)), const((d, LANES)), const((1, LANES)), const((CONV_K, CONV_WIDTH)),
            const((1, CONV_WIDTH)), const((tm, tm)), const((LANES, hp)), const((LANES, hp)),
            const((1, hp)),
        ],
        out_specs=[row(CONV_WIDTH), col(hp), row(hp), col(hp), row(LANES)],
        out_shape=[
            jax.ShapeDtypeStruct((s, CONV_WIDTH), BF16),
            jax.ShapeDtypeStruct((hp, s), BF16),
            jax.ShapeDtypeStruct((s, hp), BF16),
            jax.ShapeDtypeStruct((hp, s), BF16),
            jax.ShapeDtypeStruct((s, LANES), F32),
        ],
        scratch_shapes=[pltpu.VMEM((tm + 8, CONV_WIDTH), F32), pltpu.VMEM((8, LANES), F32)],
        compiler_params=pltpu.CompilerParams(
            dimension_semantics=("arbitrary",), vmem_limit_bytes=VMEM_LIMIT),
        name="mix_in",
    )(x, g, wbcv, wq, wk, wv, wf, bfr, wconv, gconv, tri, eq, ek, vone)


def _max_row_norm2(x):
    lane = lax.broadcasted_iota(jnp.int32, x.shape, 1)
    xf = jnp.where(lane < HEAD_DIM, x.astype(F32), 0.0)
    n2 = jnp.sum(xf * xf, axis=-1, keepdims=True)
    return jnp.max(n2, axis=0, keepdims=True)


def _att_kernel(cst_ref, cen_ref, qt_ref, k_ref, vt_ref, tri_ref, o_ref, m_scr, acc_scr, kn_scr):
    hd = pl.program_id(0)
    i = pl.program_id(1)
    t = qt_ref.shape[1]
    qt = qt_ref[...]

    @pl.when(i == 0)
    def _():
        def kb(n, mx):
            return jnp.maximum(mx, _max_row_norm2(k_ref[pl.ds(pl.multiple_of(n * t, t), t), :]))
        kn2 = lax.fori_loop(0, k_ref.shape[0] // t, kb, jnp.zeros((1, 1), F32))
        kn_scr[...] = jnp.broadcast_to(kn2, kn_scr.shape)

    sub = lax.broadcasted_iota(jnp.int32, qt.shape, 0)
    qf = jnp.where(sub < HEAD_DIM, qt.astype(F32), 0.0)
    qn2 = jnp.max(jnp.sum(qf * qf, axis=0, keepdims=True), axis=1, keepdims=True)
    qk_bound = jnp.sqrt(qn2 * kn_scr[0:1, 0:1])[0, 0]
    floor = -((F32_EXP_UNDERFLOW + BOUND_SLACK) * LOG2E + 2.0 * qk_bound)
    c_first = cst_ref[hd, i]

    def needed(j):
        return jnp.logical_and(j > 0, c_first - cen_ref[hd, jnp.maximum(j - 1, 0)] >= floor)

    j_lo = lax.while_loop(needed, lambda j: j - 1, jnp.maximum(i - 1, 0))
    n_rest = jnp.maximum(i - 1, 0) - j_lo

    def scores(start, size):
        return jnp.dot(k_ref[pl.ds(start, size), :], qt, preferred_element_type=F32)

    def values(start, size, p):
        return jnp.dot(vt_ref[:, pl.ds(start, size)], p, preferred_element_type=F32)

    def run(first, rest):
        @pl.when(i == 0)
        def _():
            first(scores(0, t) + tri_ref[...], 0, t)

        @pl.when(i > 0)
        def _():
            start = pl.multiple_of((i - 1) * t, t)
            s = scores(start, 2 * t)
            first(jnp.concatenate([s[:t], s[t:] + tri_ref[...]], axis=0), start, 2 * t)

        def body(n, carry):
            start = pl.multiple_of((i - 2 - n) * t, t)
            rest(scores(start, t), start, t)
            return carry

        lax.fori_loop(0, n_rest, body, 0)

    unshifted = qk_bound <= NO_SHIFT_BOUND

    @pl.when(unshifted)
    def _():
        def first(s, start, size):
            acc_scr[...] = values(start, size, jnp.exp2(s).astype(BF16))

        def rest(s, start, size):
            acc_scr[...] += values(start, size, jnp.exp2(s).astype(BF16))

        run(first, rest)

    @pl.when(jnp.logical_not(unshifted))
    def _():
        def first(s, start, size):
            m = jnp.max(s, axis=0, keepdims=True)
            m_scr[...] = m
            acc_scr[...] = values(start, size, jnp.exp2(s - m).astype(BF16))

        def rest(s, start, size):
            m_old = m_scr[...]
            m_new = jnp.maximum(m_old, jnp.max(s, axis=0, keepdims=True))
            m_scr[...] = m_new
            acc_scr[...] = jnp.exp2(m_old - m_new) * acc_scr[...] + values(
                start, size, jnp.exp2(s - m_new).astype(BF16))

        run(first, rest)

    acc = acc_scr[...]
    out = jnp.transpose(acc / acc[HEAD_DIM:HEAD_DIM + 1, :])
    lane = lax.broadcasted_iota(jnp.int32, out.shape, 1)
    o_ref[...] = jnp.where(lane < HEAD_DIM, out, 0.0)


def _attention(qt, k, vt, c):
    s = k.shape[0]
    t = ATT_TILE
    c_heads = c[:, 0:N_SPLIT * N_HEADS:N_SPLIT]
    c_first = c_heads[0::t].T
    c_last = c_heads[t - 1::t].T
    tri = jnp.asarray(np.where(np.arange(t)[:, None] <= np.arange(t)[None, :], 0.0, -np.inf), F32)
    grid_spec = pltpu.PrefetchScalarGridSpec(
        num_scalar_prefetch=2,
        grid=(N_HEADS, s // t),
        in_specs=[
            pl.BlockSpec((LANES, t), lambda h, i, *_: (h, i)),
            pl.BlockSpec((s, LANES), lambda h, i, *_: (0, h)),
            pl.BlockSpec((LANES, s), lambda h, i, *_: (h, 0)),
            pl.BlockSpec((t, t), lambda h, i, *_: (0, 0)),
        ],
        out_specs=pl.BlockSpec((t, LANES), lambda h, i, *_: (i, h)),
        scratch_shapes=[pltpu.VMEM((1, t), F32), pltpu.VMEM((LANES, t), F32),
                        pltpu.VMEM((8, LANES), F32)],
    )
    return pl.pallas_call(
        _att_kernel,
        grid_spec=grid_spec,
        out_shape=jax.ShapeDtypeStruct((s, N_HEADS * LANES), F32),
        compiler_params=pltpu.CompilerParams(
            dimension_semantics=("arbitrary", "arbitrary"), vmem_limit_bytes=VMEM_LIMIT),
        name="fox_attention",
    )(c_first, c_last, qt, k, vt, tri)


def _memkv_kernel(mem_ref, g_ref, w_ref, k_ref, v_ref):
    d = mem_ref.shape[1]
    m = _rms(mem_ref[...], g_ref[...]).astype(BF16)
    kv = jnp.dot(m, w_ref[...], preferred_element_type=F32)
    k_ref[...] = kv[:, :d].astype(BF16)
    v_ref[...] = kv[:, d:].astype(BF16)


def _memkv(mem, g, w_kv, layer):
    n, d = mem.shape
    whole = lambda shape: pl.BlockSpec(shape, lambda i: (0, 0))
    return pl.pallas_call(
        _memkv_kernel,
        grid=(1,),
        in_specs=[whole((n, d)), whole((1, d)), _layer_weight(w_kv, layer)],
        out_specs=[whole((n, d)), whole((n, d))],
        out_shape=[jax.ShapeDtypeStruct((n, d), BF16), jax.ShapeDtypeStruct((n, d), BF16)],
        compiler_params=pltpu.CompilerParams(
            dimension_semantics=("arbitrary",), vmem_limit_bytes=VMEM_LIMIT),
        name="mem_kv",
    )(mem, g, w_kv)


def _out_xattn_kernel(x_ref, ycn_ref, ya_ref, gatt_ref, woc_ref, woa_ref, gx_ref, wxq_ref,
                      kx_ref, vx_ref, wxo_ref, o_ref):
    d = x_ref.shape[1]
    xd = d // N_XHEADS
    ya = ya_ref[...]
    ms = jnp.sum(ya * ya, axis=-1, keepdims=True) * (1.0 / ATT_WIDTH)
    yan = (ya * lax.rsqrt(ms + EPS) * gatt_ref[...]).astype(BF16)
    x1 = (x_ref[...] + jnp.dot(ycn_ref[...], woc_ref[...], preferred_element_type=F32)
          + jnp.dot(yan, woa_ref[...], preferred_element_type=F32))

    hx = _rms(x1, gx_ref[...]).astype(BF16)
    q = (jnp.dot(hx, wxq_ref[...], preferred_element_type=F32) * (xd ** -0.5)).astype(BF16)
    nt = (((1,), (1,)), ((), ()))
    heads = []
    for hh in range(N_XHEADS):
        sl = slice(hh * xd, (hh + 1) * xd)
        s = lax.dot_general(q[:, sl], kx_ref[:, sl], nt, preferred_element_type=F32)
        p = jnp.exp(s - jnp.max(s, axis=-1, keepdims=True))
        denom = jnp.sum(p, axis=-1, keepdims=True)
        o = jnp.dot(p.astype(BF16), vx_ref[:, sl], preferred_element_type=F32) / denom
        heads.append(o.astype(BF16))
    o = jnp.concatenate(heads, axis=-1)
    o_ref[...] = x1 + jnp.dot(o, wxo_ref[...], preferred_element_type=F32)


def _out_xattn(x, ycn, ya, gatt, woc, woa, gx, wxq, kx, vx, wxo, layer):
    s, d = x.shape
    tm = ROW_TILE
    n_mem = kx.shape[0]
    hp = N_HEADS * LANES
    const = lambda shape: pl.BlockSpec(shape, lambda i: (0, 0), pipeline_mode=pl.Buffered(1))
    row = lambda width: pl.BlockSpec((tm, width), lambda i: (i, 0))
    return pl.pallas_call(
        _out_xattn_kernel,
        grid=(s // tm,),
        in_specs=[
            row(d), row(CONV_WIDTH), row(hp), const((1, hp)), const((CONV_WIDTH, d)),
            const((hp, d)), const((1, d)), _layer_weight(wxq, layer), const((n_mem, d)),
            const((n_mem, d)), _layer_weight(wxo, layer),
        ],
        out_specs=row(d),
        out_shape=jax.ShapeDtypeStruct((s, d), F32),
        compiler_params=pltpu.CompilerParams(
            dimension_semantics=("arbitrary",), vmem_limit_bytes=VMEM_LIMIT),
        name="out_xattn",
    )(x, ycn, ya, gatt, woc, woa, gx, wxq, kx, vx, wxo)


def _pad_heads_cols(w):
    d = w.shape[0]
    w = w.reshape(d, N_HEADS, HEAD_DIM)
    w = jnp.pad(w, ((0, 0), (0, 0), (0, LANES - HEAD_DIM)))
    return w.reshape(d, N_HEADS * LANES)


def kernel(x, mem, g_ffn1, w_ffn1_gu, w_ffn1_down, g_mix, w_mix_in, w_conv, b_f, g_conv_out,
           g_att_out, w_mix_out, g_xattn, g_mem, w_xq, w_xkv, w_xo, g_ffn2, w_ffn2_gu,
           w_ffn2_down, g_final):
    b, s, d = x.shape
    assert b == 1
    depth = g_ffn1.shape[0]
    xs = x.reshape(s, d)
    mem2 = mem.reshape(mem.shape[1], d)
    row = lambda a: a.reshape(1, -1).astype(F32)
    cw, aw = CONV_WIDTH, ATT_WIDTH
    gfin = row(g_final)
    w1gu, w1d, w2gu, w2d, wxq, wxkv, wxo = map(
        _to_bf16, (w_ffn1_gu, w_ffn1_down, w_ffn2_gu, w_ffn2_down, w_xq, w_xkv, w_xo))

    for l in range(depth):
        xs = _ffn(xs, row(g_ffn1[l]), w1gu, w1d, l, gfin, final_norm=False)

        w_in = w_mix_in[l]
        wbcv = w_in[:, :3 * cw].astype(BF16)
        wq = _pad_heads_cols(w_in[:, 3 * cw:3 * cw + aw]).astype(BF16)
        wk = _pad_heads_cols(w_in[:, 3 * cw + aw:3 * cw + 2 * aw]).astype(BF16)
        wv = _pad_heads_cols(w_in[:, 3 * cw + 2 * aw:3 * cw + 3 * aw]).astype(BF16)
        wf = jnp.repeat(w_in[:, 3 * cw + 3 * aw:], N_SPLIT, axis=1)
        wf = jnp.pad(wf, ((0, 0), (0, LANES - N_SPLIT * N_HEADS))).astype(BF16)
        bfr = jnp.pad(jnp.repeat(b_f[l], N_SPLIT), (0, LANES - N_SPLIT * N_HEADS)).reshape(1, LANES)
        ycn, q, k, v, c = _mix(xs, row(g_mix[l]), wbcv, wq, wk, wv, wf, bfr.astype(F32),
                               w_conv[l].astype(F32), row(g_conv_out[l]))

        ya = _attention(q, k, v, c)

        w_out = w_mix_out[l]
        woc = w_out[:cw].astype(BF16)
        woa = jnp.pad(w_out[cw:].reshape(N_HEADS, HEAD_DIM, d),
                      ((0, 0), (0, LANES - HEAD_DIM), (0, 0))).reshape(N_HEADS * LANES, d).astype(BF16)
        gatt = jnp.pad(g_att_out[l].reshape(N_HEADS, HEAD_DIM),
                       ((0, 0), (0, LANES - HEAD_DIM))).reshape(1, N_HEADS * LANES).astype(F32)
        kx, vx = _memkv(mem2, row(g_mem[l]), wxkv, l)
        xs = _out_xattn(xs, ycn, ya, gatt, woc, woa, row(g_xattn[l]), wxq, kx, vx, wxo, l)

        xs = _ffn(xs, row(g_ffn2[l]), w2gu, w2d, l, gfin, final_norm=(l == depth - 1))
    return xs.reshape(b, s, d)
```

```python
import functools

import numpy as np
import jax
import jax.numpy as jnp
from jax import lax
from jax.experimental import pallas as pl
from jax.experimental.pallas import tpu as pltpu

F32 = jnp.float32
BF16 = jnp.bfloat16

EPS = 1e-6
LANES = 128
SUBLANES = 8
MXU_DEPTH = 256
HEAD_DIM = 64
N_HEADS = 8
CONV_WIDTH = 512
ATT_WIDTH = N_HEADS * HEAD_DIM
CONV_K = 3
N_XHEADS = 4
N_SPLIT = 3
F32_EXP_UNDERFLOW = 104.0
BOUND_SLACK = 1.0
LOG2E = 1.4426950408889634
NO_SHIFT_BOUND = 40.0 * LOG2E
VMEM_LIMIT = 48 * 1024 * 1024
CAST_BLOCK_BYTES = 6 * 1024 * 1024

ROW_TILE = 512
ATT_TILE = ROW_TILE


def _rms(x, g):
    ms = jnp.mean(x * x, axis=-1, keepdims=True)
    return x * lax.rsqrt(ms + EPS) * g


def _split3(x):
    hi = x.astype(BF16).astype(F32)
    r = x - hi
    mid = r.astype(BF16).astype(F32)
    lo = (r - mid).astype(BF16).astype(F32)
    return hi, mid, lo


def _layer_weight(w, layer):
    return pl.BlockSpec((None,) + w.shape[1:], lambda i: (layer, 0, 0), pipeline_mode=pl.Buffered(1))


def _const(shape):
    return pl.BlockSpec(shape, lambda i: (0, 0), pipeline_mode=pl.Buffered(1))


def _ffn_kernel(x_ref, g_ref, wgu_ref, wd_ref, gf_ref, o_ref, *, chunks, final_norm):
    d_ff = wd_ref.shape[0]
    x = x_ref[...]
    h = _rms(x, g_ref[...]).astype(BF16)
    acc = None
    for lo, hi in chunks:
        gate = jnp.dot(h, wgu_ref[:, lo:hi], preferred_element_type=F32)
        up = jnp.dot(h, wgu_ref[:, d_ff + lo:d_ff + hi], preferred_element_type=F32)
        act = (gate * (1.0 / (1.0 + jnp.exp(-gate))) * up).astype(BF16)
        part = jnp.dot(act, wd_ref[lo:hi, :], preferred_element_type=F32)
        acc = part if acc is None else acc + part
    y = x + 0.5 * acc
    if final_norm:
        y = _rms(y, gf_ref[...])
    o_ref[...] = y


def _ffn(x, g, w_gu, w_down, layer, g_final, *, final_norm):
    s, d = x.shape
    d_ff = w_down.shape[1]
    tm = ROW_TILE
    assert s % tm == 0 and d_ff % MXU_DEPTH == 0
    split = (d_ff // MXU_DEPTH + 1) // 2 * MXU_DEPTH
    chunks = ((0, split), (split, d_ff))
    return pl.pallas_call(
        functools.partial(_ffn_kernel, chunks=chunks, final_norm=final_norm),
        grid=(s // tm,),
        in_specs=[
            pl.BlockSpec((tm, d), lambda i: (i, 0)),
            _const((1, d)),
            _layer_weight(w_gu, layer),
            _layer_weight(w_down, layer),
            _const((1, d)),
        ],
        out_specs=pl.BlockSpec((tm, d), lambda i: (i, 0)),
        out_shape=jax.ShapeDtypeStruct((s, d), F32),
        compiler_params=pltpu.CompilerParams(
            dimension_semantics=("arbitrary",), vmem_limit_bytes=VMEM_LIMIT),
        name="ffn",
    )(x, g, w_gu, w_down, g_final)


def _cast_kernel(w_ref, o_ref):
    o_ref[...] = w_ref[...].astype(o_ref.dtype)


def _to_bf16(w):
    layers, rows, cols = w.shape
    tr = max(r for r in range(16, rows + 1, 16)
             if rows % r == 0 and r * cols * 4 <= CAST_BLOCK_BYTES)
    spec = pl.BlockSpec((None, tr, cols), lambda a, b: (a, b, 0))
    return pl.pallas_call(
        _cast_kernel,
        grid=(layers, rows // tr),
        in_specs=[spec],
        out_specs=spec,
        out_shape=jax.ShapeDtypeStruct(w.shape, BF16),
        compiler_params=pltpu.CompilerParams(
            dimension_semantics=("arbitrary", "arbitrary"), vmem_limit_bytes=VMEM_LIMIT),
        name="cast_bf16",
    )(w)


def _mix_kernel(x_ref, g_ref, win_ref, wf_ref, bf_ref, wconv_ref, gconv_ref, tri_ref,
                ycn_ref, qt_ref, k_ref, vt_ref, c_ref, qn_ref, kn_ref, u_scr, carry_scr):
    i = pl.program_id(0)
    tm = x_ref.shape[0]
    cw, aw = CONV_WIDTH, ATT_WIDTH

    @pl.when(i == 0)
    def _():
        u_scr[0:8, :] = jnp.zeros((8, cw), F32)
        carry_scr[...] = jnp.zeros_like(carry_scr)

    h = _rms(x_ref[...], g_ref[...]).astype(BF16)
    qk = jnp.dot(h, win_ref[:, 3 * cw:3 * cw + 2 * aw], preferred_element_type=F32)

    zf = jnp.dot(h, wf_ref[...], preferred_element_type=F32) + bf_ref[...]
    logf = jnp.minimum(zf, 0.0) - jnp.log1p(jnp.exp(-jnp.abs(zf)))
    tri = tri_ref[...]
    c = carry_scr[7:8, :]
    for piece in _split3(logf):
        c = c + jnp.dot(tri, piece.astype(BF16), preferred_element_type=F32)
    carry_scr[...] = c[tm - 8:tm, :]
    c2 = c * LOG2E
    c_ref[...] = c2

    hi, mid, lo = _split3(c2)
    lane = lax.broadcasted_iota(jnp.int32, c.shape, 1)
    r = lane % SUBLANES
    used = lane < N_HEADS * SUBLANES
    one = jnp.ones_like(c)
    zero = jnp.zeros_like(c)
    q_bias = jnp.where(r == 0, hi, jnp.where(r == 1, mid, jnp.where(r == 2, lo, jnp.where(r < 6, one, zero))))
    k_bias = jnp.where(r < 3, one, jnp.where(r == 3, -hi, jnp.where(r == 4, -mid, jnp.where(r == 5, -lo, zero))))
    q_bias_t = jnp.transpose(jnp.where(used, q_bias, zero))
    k_bias = jnp.where(used, k_bias, zero)

    z = jnp.dot(h, win_ref[:, 0:3 * cw], preferred_element_type=F32)
    v = jnp.dot(h, win_ref[:, 3 * cw + 2 * aw:3 * cw + 3 * aw], preferred_element_type=F32)

    q_t = jnp.transpose(qk[:, 0:aw] * (HEAD_DIM ** -0.5 * LOG2E))
    k = qk[:, aw:2 * aw]
    pad_rows = jnp.zeros((LANES - HEAD_DIM - SUBLANES, tm), F32)
    head_lane = lane < HEAD_DIM
    qn, kn = [], []
    for hh in range(N_HEADS):
        rows = slice(hh * HEAD_DIM, (hh + 1) * HEAD_DIM)
        blk = slice(hh * LANES, (hh + 1) * LANES)
        q_h = q_t[rows]
        qt_ref[blk, :] = jnp.concatenate(
            [q_h, q_bias_t[hh * SUBLANES:(hh + 1) * SUBLANES], pad_rows], axis=0).astype(BF16)

        pair = k[:, (hh // 2) * LANES:(hh // 2 + 1) * LANES]
        k_pos = pair if hh % 2 == 0 else pltpu.roll(pair, HEAD_DIM, 1)
        bias = pltpu.roll(k_bias, (HEAD_DIM - hh * SUBLANES) % LANES, 1)
        k_h = jnp.where(head_lane, k_pos, bias).astype(BF16)
        k_ref[:, blk] = k_h

        qf = q_h.astype(BF16).astype(F32)
        qn.append(jnp.max(jnp.sum(qf * qf, axis=0, keepdims=True), axis=1, keepdims=True))
        kf = jnp.where(head_lane, k_h.astype(F32), 0.0)
        kn.append(jnp.max(jnp.sum(kf * kf, axis=1, keepdims=True), axis=0, keepdims=True))
    qn_ref[...] = jnp.concatenate([jnp.broadcast_to(n, (1, LANES)) for n in qn], axis=0)
    kn_ref[...] = jnp.concatenate([jnp.broadcast_to(n, (1, LANES)) for n in kn], axis=0)

    u = z[:, cw:2 * cw] * z[:, 2 * cw:3 * cw]
    u_scr[8:tm + 8, :] = u
    u1 = u_scr[7:tm + 7, :]
    u2 = u_scr[6:tm + 6, :]
    w = wconv_ref[...]
    conv = w[2:3, :] * u + w[1:2, :] * u1 + w[0:1, :] * u2
    u_scr[0:8, :] = u[tm - 8:tm, :]
    ycn_ref[...] = _rms(z[:, 0:cw] * conv, gconv_ref[...]).astype(BF16)

    v_t = jnp.transpose(v.astype(BF16).astype(F32))
    ones_rows = jnp.where(lax.broadcasted_iota(jnp.int32, (SUBLANES, tm), 0) == 0, 1.0, 0.0)
    for hh in range(N_HEADS):
        vt_ref[hh * LANES:(hh + 1) * LANES, :] = jnp.concatenate(
            [v_t[hh * HEAD_DIM:(hh + 1) * HEAD_DIM], ones_rows, pad_rows], axis=0).astype(BF16)


def _mix(x, g, w_in, layer, wf, bfr, wconv, gconv):
    s, d = x.shape
    tm = ROW_TILE
    hp = N_HEADS * LANES
    nt = s // tm
    tri = jnp.asarray(np.tril(np.ones((tm, tm), np.float32)), BF16)
    row = lambda width: pl.BlockSpec((tm, width), lambda i: (i, 0))
    col = lambda height: pl.BlockSpec((height, tm), lambda i: (0, i))
    return pl.pallas_call(
        _mix_kernel,
        grid=(nt,),
        in_specs=[
            row(d), _const((1, d)), _layer_weight(w_in, layer), _const((d, LANES)),
            _const((1, LANES)), _const((CONV_K, CONV_WIDTH)), _const((1, CONV_WIDTH)),
            _const((tm, tm)),
        ],
        out_specs=[row(CONV_WIDTH), col(hp), row(hp), col(hp), row(LANES),
                   pl.BlockSpec((N_HEADS, LANES), lambda i: (i, 0)),
                   pl.BlockSpec((N_HEADS, LANES), lambda i: (i, 0))],
        out_shape=[
            jax.ShapeDtypeStruct((s, CONV_WIDTH), BF16),
            jax.ShapeDtypeStruct((hp, s), BF16),
            jax.ShapeDtypeStruct((s, hp), BF16),
            jax.ShapeDtypeStruct((hp, s), BF16),
            jax.ShapeDtypeStruct((s, LANES), F32),
            jax.ShapeDtypeStruct((nt * N_HEADS, LANES), F32),
            jax.ShapeDtypeStruct((nt * N_HEADS, LANES), F32),
        ],
        scratch_shapes=[pltpu.VMEM((tm + 8, CONV_WIDTH), F32), pltpu.VMEM((8, LANES), F32)],
        compiler_params=pltpu.CompilerParams(
            dimension_semantics=("arbitrary",), vmem_limit_bytes=VMEM_LIMIT),
        name="mix_in",
    )(x, g, w_in, wf, bfr, wconv, gconv, tri)


def _att_kernel(cst_ref, cen_ref, ub_ref, qt_ref, k_ref, vt_ref, tri_ref, o_ref, m_scr, acc_scr):
    hd = pl.program_id(0)
    i = pl.program_id(1)
    t = qt_ref.shape[1]
    qt = qt_ref[...]

    qk_bound = ub_ref[hd, i]
    floor = -((F32_EXP_UNDERFLOW + BOUND_SLACK) * LOG2E + 2.0 * qk_bound)
    c_first = cst_ref[hd, i]

    def needed(j):
        return jnp.logical_and(j > 0, c_first - cen_ref[hd, jnp.maximum(j - 1, 0)] >= floor)

    j_lo = lax.while_loop(needed, lambda j: j - 1, jnp.maximum(i - 1, 0))
    n_rest = jnp.maximum(i - 1, 0) - j_lo

    def scores(start, size):
        return jnp.dot(k_ref[pl.ds(start, size), :], qt, preferred_element_type=F32)

    def values(start, size, p):
        return jnp.dot(vt_ref[:, pl.ds(start, size)], p, preferred_element_type=F32)

    def run(first, rest):
        @pl.when(i == 0)
        def _():
            first(scores(0, t) + tri_ref[...], 0, t)

        @pl.when(i > 0)
        def _():
            start = pl.multiple_of((i - 1) * t, t)
            s = scores(start, 2 * t)
            first(jnp.concatenate([s[:t], s[t:] + tri_ref[...]], axis=0), start, 2 * t)

        def body(n, carry):
            start = pl.multiple_of((i - 2 - n) * t, t)
            rest(scores(start, t), start, t)
            return carry

        lax.fori_loop(0, n_rest, body, 0)

    unshifted = qk_bound <= NO_SHIFT_BOUND

    @pl.when(unshifted)
    def _():
        def first(s, start, size):
            acc_scr[...] = values(start, size, jnp.exp2(s).astype(BF16))

        def rest(s, start, size):
            acc_scr[...] += values(start, size, jnp.exp2(s).astype(BF16))

        run(first, rest)

    @pl.when(jnp.logical_not(unshifted))
    def _():
        def first(s, start, size):
            m = jnp.max(s, axis=0, keepdims=True)
            m_scr[...] = m
            acc_scr[...] = values(start, size, jnp.exp2(s - m).astype(BF16))

        def rest(s, start, size):
            m_old = m_scr[...]
            m_new = jnp.maximum(m_old, jnp.max(s, axis=0, keepdims=True))
            m_scr[...] = m_new
            acc_scr[...] = jnp.exp2(m_old - m_new) * acc_scr[...] + values(
                start, size, jnp.exp2(s - m_new).astype(BF16))

        run(first, rest)

    acc = acc_scr[...]
    o_ref[...] = acc[0:HEAD_DIM, :] / acc[HEAD_DIM:HEAD_DIM + 1, :]


def _attention(qt, k, vt, c, qn, kn):
    s = k.shape[0]
    t = ATT_TILE
    nq = s // t
    c_heads = c[:, 0:N_HEADS * SUBLANES:SUBLANES]
    c_first = c_heads[0::t].T
    c_last = c_heads[t - 1::t].T
    qn = qn.reshape(nq, N_HEADS, LANES)[:, :, 0].T
    kn = jnp.max(kn.reshape(nq, N_HEADS, LANES)[:, :, 0], axis=0)
    qk_bound = jnp.sqrt(qn * kn[:, None])
    tri = jnp.asarray(np.where(np.arange(t)[:, None] <= np.arange(t)[None, :], 0.0, -np.inf), F32)
    grid_spec = pltpu.PrefetchScalarGridSpec(
        num_scalar_prefetch=3,
        grid=(N_HEADS, nq),
        in_specs=[
            pl.BlockSpec((LANES, t), lambda h, i, *_: (h, i)),
            pl.BlockSpec((s, LANES), lambda h, i, *_: (0, h)),
            pl.BlockSpec((LANES, s), lambda h, i, *_: (h, 0)),
            pl.BlockSpec((t, t), lambda h, i, *_: (0, 0)),
        ],
        out_specs=pl.BlockSpec((HEAD_DIM, t), lambda h, i, *_: (h, i)),
        scratch_shapes=[pltpu.VMEM((1, t), F32), pltpu.VMEM((LANES, t), F32)],
    )
    return pl.pallas_call(
        _att_kernel,
        grid_spec=grid_spec,
        out_shape=jax.ShapeDtypeStruct((ATT_WIDTH, s), F32),
        compiler_params=pltpu.CompilerParams(
            dimension_semantics=("arbitrary", "arbitrary"), vmem_limit_bytes=VMEM_LIMIT),
        name="fox_attention",
    )(c_first, c_last, qk_bound, qt, k, vt, tri)


def _memkv_kernel(mem_ref, g_ref, w_ref, k_ref, v_ref):
    d = mem_ref.shape[1]
    m = _rms(mem_ref[...], g_ref[...]).astype(BF16)
    kv = jnp.dot(m, w_ref[...], preferred_element_type=F32)
    k_ref[...] = kv[:, :d].astype(BF16)
    v_ref[...] = kv[:, d:].astype(BF16)


def _memkv(mem, g, w_kv, layer):
    n, d = mem.shape
    whole = lambda shape: pl.BlockSpec(shape, lambda i: (0, 0))
    return pl.pallas_call(
        _memkv_kernel,
        grid=(1,),
        in_specs=[whole((n, d)), whole((1, d)), _layer_weight(w_kv, layer)],
        out_specs=[whole((n, d)), whole((n, d))],
        out_shape=[jax.ShapeDtypeStruct((n, d), BF16), jax.ShapeDtypeStruct((n, d), BF16)],
        compiler_params=pltpu.CompilerParams(
            dimension_semantics=("arbitrary",), vmem_limit_bytes=VMEM_LIMIT),
        name="mem_kv",
    )(mem, g, w_kv)


def _out_xattn_kernel(x_ref, ycn_ref, yat_ref, gatt_ref, wout_ref, gx_ref, wxq_ref,
                      kx_ref, vx_ref, wxo_ref, o_ref):
    d = x_ref.shape[1]
    xd = d // N_XHEADS
    ya_t = yat_ref[...]
    ms = jnp.mean(ya_t * ya_t, axis=0, keepdims=True)
    ya = jnp.transpose(ya_t * lax.rsqrt(ms + EPS))
    y = jnp.concatenate([ycn_ref[...], (ya * gatt_ref[...]).astype(BF16)], axis=-1)
    x1 = x_ref[...] + jnp.dot(y, wout_ref[...], preferred_element_type=F32)

    hx = _rms(x1, gx_ref[...]).astype(BF16)
    q = (jnp.dot(hx, wxq_ref[...], preferred_element_type=F32) * (xd ** -0.5)).astype(BF16)
    nt = (((1,), (1,)), ((), ()))
    heads = []
    for hh in range(N_XHEADS):
        sl = slice(hh * xd, (hh + 1) * xd)
        s = lax.dot_general(q[:, sl], kx_ref[:, sl], nt, preferred_element_type=F32)
        p = jnp.exp(s - jnp.max(s, axis=-1, keepdims=True))
        denom = jnp.sum(p, axis=-1, keepdims=True)
        o = jnp.dot(p.astype(BF16), vx_ref[:, sl], preferred_element_type=F32) / denom
        heads.append(o.astype(BF16))
    o = jnp.concatenate(heads, axis=-1)
    o_ref[...] = x1 + jnp.dot(o, wxo_ref[...], preferred_element_type=F32)


def _out_xattn(x, ycn, ya_t, gatt, wout, gx, wxq, kx, vx, wxo, layer):
    s, d = x.shape
    tm = ROW_TILE
    n_mem = kx.shape[0]
    row = lambda width: pl.BlockSpec((tm, width), lambda i: (i, 0))
    return pl.pallas_call(
        _out_xattn_kernel,
        grid=(s // tm,),
        in_specs=[
            row(d), row(CONV_WIDTH), pl.BlockSpec((ATT_WIDTH, tm), lambda i: (0, i)),
            _const((1, ATT_WIDTH)), _layer_weight(wout, layer), _const((1, d)),
            _layer_weight(wxq, layer), _const((n_mem, d)), _const((n_mem, d)),
            _layer_weight(wxo, layer),
        ],
        out_specs=row(d),
        out_shape=jax.ShapeDtypeStruct((s, d), F32),
        compiler_params=pltpu.CompilerParams(
            dimension_semantics=("arbitrary",), vmem_limit_bytes=VMEM_LIMIT),
        name="out_xattn",
    )(x, ycn, ya_t, gatt, wout, gx, wxq, kx, vx, wxo)


def kernel(x, mem, g_ffn1, w_ffn1_gu, w_ffn1_down, g_mix, w_mix_in, w_conv, b_f, g_conv_out,
           g_att_out, w_mix_out, g_xattn, g_mem, w_xq, w_xkv, w_xo, g_ffn2, w_ffn2_gu,
           w_ffn2_down, g_final):
    b, s, d = x.shape
    assert b == 1 and w_mix_in.shape[2] == 3 * CONV_WIDTH + 3 * ATT_WIDTH + N_HEADS
    depth = g_ffn1.shape[0]
    xs = x.reshape(s, d)
    mem2 = mem.reshape(mem.shape[1], d)
    row = lambda a: a.reshape(1, -1).astype(F32)
    gfin = row(g_final)
    w1gu, w1d, w2gu, w2d, wxq, wxkv, wxo, win, wout = map(
        _to_bf16, (w_ffn1_gu, w_ffn1_down, w_ffn2_gu, w_ffn2_down, w_xq, w_xkv, w_xo,
                   w_mix_in, w_mix_out))
    gate_lanes = (0, LANES - N_HEADS * SUBLANES)

    for l in range(depth):
        xs = _ffn(xs, row(g_ffn1[l]), w1gu, w1d, l, gfin, final_norm=False)

        wf = jnp.repeat(w_mix_in[l][:, 3 * CONV_WIDTH + 3 * ATT_WIDTH:], SUBLANES, axis=1)
        wf = jnp.pad(wf, ((0, 0), gate_lanes)).astype(BF16)
        bfr = jnp.pad(jnp.repeat(b_f[l], SUBLANES), gate_lanes).reshape(1, LANES).astype(F32)
        ycn, qt, k, vt, c, qn, kn = _mix(xs, row(g_mix[l]), win, l, wf, bfr,
                                         w_conv[l].astype(F32), row(g_conv_out[l]))

        ya_t = _attention(qt, k, vt, c, qn, kn)

        kx, vx = _memkv(mem2, row(g_mem[l]), wxkv, l)
        xs = _out_xattn(xs, ycn, ya_t, row(g_att_out[l]), wout, row(g_xattn[l]), wxq, kx, vx,
                        wxo, l)

        xs = _ffn(xs, row(g_ffn2[l]), w2gu, w2d, l, gfin, final_norm=(l == depth - 1))
    return xs.reshape(b, s, d)
```

```python
import functools

import numpy as np
import jax
import jax.numpy as jnp
from jax import lax
from jax.experimental import pallas as pl
from jax.experimental.pallas import tpu as pltpu

F32 = jnp.float32
BF16 = jnp.bfloat16

EPS = 1e-6
LANES = 128
SUBLANES = 8
MXU_DEPTH = 256
HEAD_DIM = 64
N_HEADS = 8
CONV_WIDTH = 512
ATT_WIDTH = N_HEADS * HEAD_DIM
CONV_K = 3
N_XHEADS = 4
N_SPLIT = 3
F32_EXP_UNDERFLOW = 104.0
BOUND_SLACK = 1.0
LOG2E = 1.4426950408889634
NO_SHIFT_BOUND = 40.0 * LOG2E
VMEM_LIMIT = 48 * 1024 * 1024
CAST_BLOCK_BYTES = 6 * 1024 * 1024

ROW_TILE = 512
ATT_TILE = ROW_TILE
ATT_HEADS = 2
V_ROWS = 80


def _rms(x, g):
    ms = jnp.mean(x * x, axis=-1, keepdims=True)
    return x * lax.rsqrt(ms + EPS) * g


def _split3(x):
    hi = x.astype(BF16).astype(F32)
    r = x - hi
    mid = r.astype(BF16).astype(F32)
    lo = (r - mid).astype(BF16).astype(F32)
    return hi, mid, lo


def _layer_weight(w, layer):
    return pl.BlockSpec((None,) + w.shape[1:], lambda i: (layer, 0, 0), pipeline_mode=pl.Buffered(1))


def _const(shape):
    return pl.BlockSpec(shape, lambda i: (0, 0), pipeline_mode=pl.Buffered(1))


def _ffn_kernel(x_ref, g_ref, wgu_ref, wd_ref, gf_ref, o_ref, *, chunks, final_norm):
    d_ff = wd_ref.shape[0]
    x = x_ref[...]
    h = _rms(x, g_ref[...]).astype(BF16)
    acc = None
    for lo, hi in chunks:
        gate = jnp.dot(h, wgu_ref[:, lo:hi], preferred_element_type=F32)
        up = jnp.dot(h, wgu_ref[:, d_ff + lo:d_ff + hi], preferred_element_type=F32)
        act = (gate * (1.0 / (1.0 + jnp.exp(-gate))) * up).astype(BF16)
        part = jnp.dot(act, wd_ref[lo:hi, :], preferred_element_type=F32)
        acc = part if acc is None else acc + part
    y = x + 0.5 * acc
    if final_norm:
        y = _rms(y, gf_ref[...])
    o_ref[...] = y


def _ffn(x, g, w_gu, w_down, layer, g_final, *, final_norm):
    s, d = x.shape
    d_ff = w_down.shape[1]
    tm = ROW_TILE
    assert s % tm == 0 and d_ff % MXU_DEPTH == 0
    split = (d_ff // MXU_DEPTH + 1) // 2 * MXU_DEPTH
    chunks = ((0, split), (split, d_ff))
    return pl.pallas_call(
        functools.partial(_ffn_kernel, chunks=chunks, final_norm=final_norm),
        grid=(s // tm,),
        in_specs=[
            pl.BlockSpec((tm, d), lambda i: (i, 0)),
            _const((1, d)),
            _layer_weight(w_gu, layer),
            _layer_weight(w_down, layer),
            _const((1, d)),
        ],
        out_specs=pl.BlockSpec((tm, d), lambda i: (i, 0)),
        out_shape=jax.ShapeDtypeStruct((s, d), F32),
        compiler_params=pltpu.CompilerParams(
            dimension_semantics=("arbitrary",), vmem_limit_bytes=VMEM_LIMIT),
        name="ffn",
    )(x, g, w_gu, w_down, g_final)


def _cast_kernel(w_ref, o_ref):
    o_ref[...] = w_ref[...].astype(o_ref.dtype)


def _to_bf16(w):
    layers, rows, cols = w.shape
    tr = max(r for r in range(16, rows + 1, 16)
             if rows % r == 0 and r * cols * 4 <= CAST_BLOCK_BYTES)
    spec = pl.BlockSpec((None, tr, cols), lambda a, b: (a, b, 0))
    return pl.pallas_call(
        _cast_kernel,
        grid=(layers, rows // tr),
        in_specs=[spec],
        out_specs=spec,
        out_shape=jax.ShapeDtypeStruct(w.shape, BF16),
        compiler_params=pltpu.CompilerParams(
            dimension_semantics=("arbitrary", "arbitrary"), vmem_limit_bytes=VMEM_LIMIT),
        name="cast_bf16",
    )(w)


def _mix_kernel(x_ref, g_ref, win_ref, wf_ref, bf_ref, wconv_ref, gconv_ref, tri_ref,
                ycn_ref, qt_ref, k_ref, vt_ref, c_ref, qn_ref, kn_ref, u_scr, carry_scr):
    i = pl.program_id(0)
    tm = x_ref.shape[0]
    cw, aw = CONV_WIDTH, ATT_WIDTH

    @pl.when(i == 0)
    def _():
        u_scr[0:8, :] = jnp.zeros((8, cw), F32)
        carry_scr[...] = jnp.zeros_like(carry_scr)

    h = _rms(x_ref[...], g_ref[...]).astype(BF16)
    qk = jnp.dot(h, win_ref[:, 3 * cw:3 * cw + 2 * aw], preferred_element_type=F32)

    zf = jnp.dot(h, wf_ref[...], preferred_element_type=F32) + bf_ref[...]
    logf = jnp.minimum(zf, 0.0) - jnp.log1p(jnp.exp(-jnp.abs(zf)))
    tri = tri_ref[...]
    c = carry_scr[7:8, :]
    for piece in _split3(logf):
        c = c + jnp.dot(tri, piece.astype(BF16), preferred_element_type=F32)
    carry_scr[...] = c[tm - 8:tm, :]
    c2 = c * LOG2E
    c_ref[...] = c2

    hi, mid, lo = _split3(c2)
    lane = lax.broadcasted_iota(jnp.int32, c.shape, 1)
    r = lane % SUBLANES
    used = lane < N_HEADS * SUBLANES
    one = jnp.ones_like(c)
    zero = jnp.zeros_like(c)
    q_bias = jnp.where(r == 0, hi, jnp.where(r == 1, mid, jnp.where(r == 2, lo, jnp.where(r < 6, one, zero))))
    k_bias = jnp.where(r < 3, one, jnp.where(r == 3, -hi, jnp.where(r == 4, -mid, jnp.where(r == 5, -lo, zero))))
    q_bias_t = jnp.transpose(jnp.where(used, q_bias, zero))
    k_bias = jnp.where(used, k_bias, zero)

    z = jnp.dot(h, win_ref[:, 0:3 * cw], preferred_element_type=F32)
    v = jnp.dot(h, win_ref[:, 3 * cw + 2 * aw:3 * cw + 3 * aw], preferred_element_type=F32)

    q_t = jnp.transpose(qk[:, 0:aw] * (HEAD_DIM ** -0.5 * LOG2E))
    k = qk[:, aw:2 * aw]
    pad_rows = jnp.zeros((LANES - HEAD_DIM - SUBLANES, tm), F32)
    head_lane = lane < HEAD_DIM
    qn, kn = [], []
    for hh in range(N_HEADS):
        rows = slice(hh * HEAD_DIM, (hh + 1) * HEAD_DIM)
        blk = slice(hh * LANES, (hh + 1) * LANES)
        q_h = q_t[rows]
        qt_ref[blk, :] = jnp.concatenate(
            [q_h, q_bias_t[hh * SUBLANES:(hh + 1) * SUBLANES], pad_rows], axis=0).astype(BF16)

        pair = k[:, (hh // 2) * LANES:(hh // 2 + 1) * LANES]
        k_pos = pair if hh % 2 == 0 else pltpu.roll(pair, HEAD_DIM, 1)
        bias = pltpu.roll(k_bias, (HEAD_DIM - hh * SUBLANES) % LANES, 1)
        k_h = jnp.where(head_lane, k_pos, bias).astype(BF16)
        k_ref[:, blk] = k_h

        qf = q_h.astype(BF16).astype(F32)
        qn.append(jnp.max(jnp.sum(qf * qf, axis=0, keepdims=True), axis=1, keepdims=True))
        kf = jnp.where(head_lane, k_h.astype(F32), 0.0)
        kn.append(jnp.max(jnp.sum(kf * kf, axis=1, keepdims=True), axis=0, keepdims=True))
    qn_ref[...] = jnp.concatenate([jnp.broadcast_to(n, (1, LANES)) for n in qn], axis=0)
    kn_ref[...] = jnp.concatenate([jnp.broadcast_to(n, (1, LANES)) for n in kn], axis=0)

    u = z[:, cw:2 * cw] * z[:, 2 * cw:3 * cw]
    u_scr[8:tm + 8, :] = u
    u1 = u_scr[7:tm + 7, :]
    u2 = u_scr[6:tm + 6, :]
    w = wconv_ref[...]
    conv = w[2:3, :] * u + w[1:2, :] * u1 + w[0:1, :] * u2
    u_scr[0:8, :] = u[tm - 8:tm, :]
    ycn_ref[...] = _rms(z[:, 0:cw] * conv, gconv_ref[...]).astype(BF16)

    v_t = jnp.transpose(v.astype(BF16).astype(F32))
    ones_rows = jnp.where(lax.broadcasted_iota(jnp.int32, (SUBLANES, tm), 0) == 0, 1.0, 0.0)
    for hh in range(N_HEADS):
        vt_ref[hh * LANES:(hh + 1) * LANES, :] = jnp.concatenate(
            [v_t[hh * HEAD_DIM:(hh + 1) * HEAD_DIM], ones_rows, pad_rows], axis=0).astype(BF16)


def _mix(x, g, w_in, layer, wf, bfr, wconv, gconv):
    s, d = x.shape
    tm = ROW_TILE
    hp = N_HEADS * LANES
    nt = s // tm
    tri = jnp.asarray(np.tril(np.ones((tm, tm), np.float32)), BF16)
    row = lambda width: pl.BlockSpec((tm, width), lambda i: (i, 0))
    col = lambda height: pl.BlockSpec((height, tm), lambda i: (0, i))
    return pl.pallas_call(
        _mix_kernel,
        grid=(nt,),
        in_specs=[
            row(d), _const((1, d)), _layer_weight(w_in, layer), _const((d, LANES)),
            _const((1, LANES)), _const((CONV_K, CONV_WIDTH)), _const((1, CONV_WIDTH)),
            _const((tm, tm)),
        ],
        out_specs=[row(CONV_WIDTH), col(hp), row(hp), col(hp), row(LANES),
                   pl.BlockSpec((N_HEADS, LANES), lambda i: (i, 0)),
                   pl.BlockSpec((N_HEADS, LANES), lambda i: (i, 0))],
        out_shape=[
            jax.ShapeDtypeStruct((s, CONV_WIDTH), BF16),
            jax.ShapeDtypeStruct((hp, s), BF16),
            jax.ShapeDtypeStruct((s, hp), BF16),
            jax.ShapeDtypeStruct((hp, s), BF16),
            jax.ShapeDtypeStruct((s, LANES), F32),
            jax.ShapeDtypeStruct((nt * N_HEADS, LANES), F32),
            jax.ShapeDtypeStruct((nt * N_HEADS, LANES), F32),
        ],
        scratch_shapes=[pltpu.VMEM((tm + 8, CONV_WIDTH), F32), pltpu.VMEM((8, LANES), F32)],
        compiler_params=pltpu.CompilerParams(
            dimension_semantics=("arbitrary",), vmem_limit_bytes=VMEM_LIMIT),
        name="mix_in",
    )(x, g, w_in, wf, bfr, wconv, gconv, tri)


def _att_kernel(cst_ref, cen_ref, ub_ref, qt_ref, k_ref, vt_ref, tri_ref, o_ref, m_scr, acc_scr):
    group = pl.program_id(0)
    i = pl.program_id(1)
    t = qt_ref.shape[1]
    heads = range(ATT_HEADS)

    def blocks_past_first_two(hh):
        hd = group * ATT_HEADS + hh
        floor = -((F32_EXP_UNDERFLOW + BOUND_SLACK) * LOG2E + 2.0 * ub_ref[hd, i])
        c_first = cst_ref[hd, i]

        def needed(j):
            return jnp.logical_and(j > 0, c_first - cen_ref[hd, jnp.maximum(j - 1, 0)] >= floor)

        j_lo = lax.while_loop(needed, lambda j: j - 1, jnp.maximum(i - 1, 0))
        return jnp.maximum(i - 1, 0) - j_lo

    n_rest = [blocks_past_first_two(hh) for hh in heads]

    def scores(hh, start, size):
        return jnp.dot(k_ref[pl.ds(start, size), hh * LANES:(hh + 1) * LANES],
                       qt_ref[hh * LANES:(hh + 1) * LANES, :], preferred_element_type=F32)

    def values(hh, start, size, p):
        return jnp.dot(vt_ref[hh * LANES:hh * LANES + V_ROWS, pl.ds(start, size)], p,
                       preferred_element_type=F32)

    def run(first, rest):
        @pl.when(i == 0)
        def _():
            for hh in heads:
                first(hh, scores(hh, 0, t) + tri_ref[...], 0, t)

        @pl.when(i > 0)
        def _():
            start = pl.multiple_of((i - 1) * t, t)
            for hh in heads:
                s = scores(hh, start, 2 * t)
                first(hh, jnp.concatenate([s[:t], s[t:] + tri_ref[...]], axis=0), start, 2 * t)

        for hh in heads:
            def body(n, carry, hh=hh):
                start = pl.multiple_of((i - 2 - n) * t, t)
                rest(hh, scores(hh, start, t), start, t)
                return carry

            lax.fori_loop(0, n_rest[hh], body, 0)

    unshifted = ub_ref[group * ATT_HEADS, i] <= NO_SHIFT_BOUND
    for hh in heads[1:]:
        unshifted = jnp.logical_and(unshifted, ub_ref[group * ATT_HEADS + hh, i] <= NO_SHIFT_BOUND)

    @pl.when(unshifted)
    def _():
        def first(hh, s, start, size):
            acc_scr[hh] = values(hh, start, size, jnp.exp2(s).astype(BF16))

        def rest(hh, s, start, size):
            acc_scr[hh] += values(hh, start, size, jnp.exp2(s).astype(BF16))

        run(first, rest)

    @pl.when(jnp.logical_not(unshifted))
    def _():
        def first(hh, s, start, size):
            m = jnp.max(s, axis=0, keepdims=True)
            m_scr[hh] = m
            acc_scr[hh] = values(hh, start, size, jnp.exp2(s - m).astype(BF16))

        def rest(hh, s, start, size):
            m_old = m_scr[hh]
            m_new = jnp.maximum(m_old, jnp.max(s, axis=0, keepdims=True))
            m_scr[hh] = m_new
            acc_scr[hh] = jnp.exp2(m_old - m_new) * acc_scr[hh] + values(
                hh, start, size, jnp.exp2(s - m_new).astype(BF16))

        run(first, rest)

    for hh in heads:
        acc = acc_scr[hh]
        o_ref[hh * HEAD_DIM:(hh + 1) * HEAD_DIM, :] = acc[0:HEAD_DIM, :] / acc[HEAD_DIM:HEAD_DIM + 1, :]


def _attention(qt, k, vt, c, qn, kn):
    s = k.shape[0]
    t = ATT_TILE
    nq = s // t
    c_heads = c[:, 0:N_HEADS * SUBLANES:SUBLANES]
    c_first = c_heads[0::t].T
    c_last = c_heads[t - 1::t].T
    qn = qn.reshape(nq, N_HEADS, LANES)[:, :, 0].T
    kn = jnp.max(kn.reshape(nq, N_HEADS, LANES)[:, :, 0], axis=0)
    qk_bound = jnp.sqrt(qn * kn[:, None])
    tri = jnp.asarray(np.where(np.arange(t)[:, None] <= np.arange(t)[None, :], 0.0, -np.inf), F32)
    grid_spec = pltpu.PrefetchScalarGridSpec(
        num_scalar_prefetch=3,
        grid=(N_HEADS // ATT_HEADS, nq),
        in_specs=[
            pl.BlockSpec((ATT_HEADS * LANES, t), lambda g, i, *_: (g, i)),
            pl.BlockSpec((s, ATT_HEADS * LANES), lambda g, i, *_: (0, g)),
            pl.BlockSpec((ATT_HEADS * LANES, s), lambda g, i, *_: (g, 0)),
            pl.BlockSpec((t, t), lambda g, i, *_: (0, 0)),
        ],
        out_specs=pl.BlockSpec((ATT_HEADS * HEAD_DIM, t), lambda g, i, *_: (g, i)),
        scratch_shapes=[pltpu.VMEM((ATT_HEADS, 1, t), F32), pltpu.VMEM((ATT_HEADS, V_ROWS, t), F32)],
    )
    return pl.pallas_call(
        _att_kernel,
        grid_spec=grid_spec,
        out_shape=jax.ShapeDtypeStruct((ATT_WIDTH, s), F32),
        compiler_params=pltpu.CompilerParams(
            dimension_semantics=("arbitrary", "arbitrary"), vmem_limit_bytes=VMEM_LIMIT),
        name="fox_attention",
    )(c_first, c_last, qk_bound, qt, k, vt, tri)


def _memkv_kernel(mem_ref, g_ref, w_ref, k_ref, v_ref):
    d = mem_ref.shape[1]
    m = _rms(mem_ref[...], g_ref[...]).astype(BF16)
    kv = jnp.dot(m, w_ref[...], preferred_element_type=F32)
    k_ref[...] = kv[:, :d].astype(BF16)
    v_ref[...] = kv[:, d:].astype(BF16)


def _memkv(mem, g, w_kv, layer):
    n, d = mem.shape
    whole = lambda shape: pl.BlockSpec(shape, lambda i: (0, 0))
    return pl.pallas_call(
        _memkv_kernel,
        grid=(1,),
        in_specs=[whole((n, d)), whole((1, d)), _layer_weight(w_kv, layer)],
        out_specs=[whole((n, d)), whole((n, d))],
        out_shape=[jax.ShapeDtypeStruct((n, d), BF16), jax.ShapeDtypeStruct((n, d), BF16)],
        compiler_params=pltpu.CompilerParams(
            dimension_semantics=("arbitrary",), vmem_limit_bytes=VMEM_LIMIT),
        name="mem_kv",
    )(mem, g, w_kv)


def _out_xattn_kernel(x_ref, ycn_ref, yat_ref, gatt_ref, wout_ref, gx_ref, wxq_ref,
                      kx_ref, vx_ref, wxo_ref, o_ref):
    d = x_ref.shape[1]
    xd = d // N_XHEADS
    ya_t = yat_ref[...]
    ms = jnp.mean(ya_t * ya_t, axis=0, keepdims=True)
    ya = jnp.transpose(ya_t * lax.rsqrt(ms + EPS))
    y = jnp.concatenate([ycn_ref[...], (ya * gatt_ref[...]).astype(BF16)], axis=-1)
    x1 = x_ref[...] + jnp.dot(y, wout_ref[...], preferred_element_type=F32)

    hx = _rms(x1, gx_ref[...]).astype(BF16)
    q = (jnp.dot(hx, wxq_ref[...], preferred_element_type=F32) * (xd ** -0.5)).astype(BF16)
    nt = (((1,), (1,)), ((), ()))
    heads = []
    for hh in range(N_XHEADS):
        sl = slice(hh * xd, (hh + 1) * xd)
        s = lax.dot_general(q[:, sl], kx_ref[:, sl], nt, preferred_element_type=F32)
        p = jnp.exp(s - jnp.max(s, axis=-1, keepdims=True))
        denom = jnp.sum(p, axis=-1, keepdims=True)
        o = jnp.dot(p.astype(BF16), vx_ref[:, sl], preferred_element_type=F32) / denom
        heads.append(o.astype(BF16))
    o = jnp.concatenate(heads, axis=-1)
    o_ref[...] = x1 + jnp.dot(o, wxo_ref[...], preferred_element_type=F32)


def _out_xattn(x, ycn, ya_t, gatt, wout, gx, wxq, kx, vx, wxo, layer):
    s, d = x.shape
    tm = ROW_TILE
    n_mem = kx.shape[0]
    row = lambda width: pl.BlockSpec((tm, width), lambda i: (i, 0))
    return pl.pallas_call(
        _out_xattn_kernel,
        grid=(s // tm,),
        in_specs=[
            row(d), row(CONV_WIDTH), pl.BlockSpec((ATT_WIDTH, tm), lambda i: (0, i)),
            _const((1, ATT_WIDTH)), _layer_weight(wout, layer), _const((1, d)),
            _layer_weight(wxq, layer), _const((n_mem, d)), _const((n_mem, d)),
            _layer_weight(wxo, layer),
        ],
        out_specs=row(d),
        out_shape=jax.ShapeDtypeStruct((s, d), F32),
        compiler_params=pltpu.CompilerParams(
            dimension_semantics=("arbitrary",), vmem_limit_bytes=VMEM_LIMIT),
        name="out_xattn",
    )(x, ycn, ya_t, gatt, wout, gx, wxq, kx, vx, wxo)


def kernel(x, mem, g_ffn1, w_ffn1_gu, w_ffn1_down, g_mix, w_mix_in, w_conv, b_f, g_conv_out,
           g_att_out, w_mix_out, g_xattn, g_mem, w_xq, w_xkv, w_xo, g_ffn2, w_ffn2_gu,
           w_ffn2_down, g_final):
    b, s, d = x.shape
    assert b == 1 and w_mix_in.shape[2] == 3 * CONV_WIDTH + 3 * ATT_WIDTH + N_HEADS
    depth = g_ffn1.shape[0]
    xs = x.reshape(s, d)
    mem2 = mem.reshape(mem.shape[1], d)
    row = lambda a: a.reshape(1, -1).astype(F32)
    gfin = row(g_final)
    w1gu, w1d, w2gu, w2d, wxq, wxkv, wxo, win, wout = map(
        _to_bf16, (w_ffn1_gu, w_ffn1_down, w_ffn2_gu, w_ffn2_down, w_xq, w_xkv, w_xo,
                   w_mix_in, w_mix_out))
    gate_lanes = (0, LANES - N_HEADS * SUBLANES)

    for l in range(depth):
        xs = _ffn(xs, row(g_ffn1[l]), w1gu, w1d, l, gfin, final_norm=False)

        wf = jnp.repeat(w_mix_in[l][:, 3 * CONV_WIDTH + 3 * ATT_WIDTH:], SUBLANES, axis=1)
        wf = jnp.pad(wf, ((0, 0), gate_lanes)).astype(BF16)
        bfr = jnp.pad(jnp.repeat(b_f[l], SUBLANES), gate_lanes).reshape(1, LANES).astype(F32)
        ycn, qt, k, vt, c, qn, kn = _mix(xs, row(g_mix[l]), win, l, wf, bfr,
                                         w_conv[l].astype(F32), row(g_conv_out[l]))

        ya_t = _attention(qt, k, vt, c, qn, kn)

        kx, vx = _memkv(mem2, row(g_mem[l]), wxkv, l)
        xs = _out_xattn(xs, ycn, ya_t, row(g_att_out[l]), wout, row(g_xattn[l]), wxq, kx, vx,
                        wxo, l)

        xs = _ffn(xs, row(g_ffn2[l]), w2gu, w2d, l, gfin, final_norm=(l == depth - 1))
    return xs.reshape(b, s, d)
```

```python
import functools

import numpy as np
import jax
import jax.numpy as jnp
from jax import lax
from jax.experimental import pallas as pl
from jax.experimental.pallas import tpu as pltpu

F32 = jnp.float32
BF16 = jnp.bfloat16

EPS = 1e-6
LANES = 128
SUBLANES = 8
MXU_DEPTH = 256
HEAD_DIM = 64
N_HEADS = 8
CONV_WIDTH = 512
ATT_WIDTH = N_HEADS * HEAD_DIM
CONV_K = 3
N_XHEADS = 4
N_SPLIT = 3
F32_EXP_UNDERFLOW = 104.0
BOUND_SLACK = 1.0
NORM_SLACK = 1.01
LOG2E = 1.4426950408889634
NO_SHIFT_BOUND = 40.0 * LOG2E
VMEM_LIMIT = 48 * 1024 * 1024
CAST_BLOCK_BYTES = 6 * 1024 * 1024

ROW_TILE = 512
FFN_TILE = 1024
ATT_TILE = ROW_TILE
ATT_HEADS = 2
V_ROWS = 80


def _rms(x, g):
    ms = jnp.mean(x * x, axis=-1, keepdims=True)
    return x * lax.rsqrt(ms + EPS) * g


def _split3(x):
    hi = x.astype(BF16).astype(F32)
    r = x - hi
    mid = r.astype(BF16).astype(F32)
    lo = (r - mid).astype(BF16).astype(F32)
    return hi, mid, lo


def _layer_weight(w, layer):
    return pl.BlockSpec((None,) + w.shape[1:], lambda i: (layer, 0, 0), pipeline_mode=pl.Buffered(1))


def _const(shape):
    return pl.BlockSpec(shape, lambda i: (0, 0), pipeline_mode=pl.Buffered(1))


def _ffn_kernel(x_ref, g_ref, wgu_ref, wd_ref, gf_ref, o_ref, *, chunks, final_norm):
    d_ff = wd_ref.shape[0]
    x = x_ref[...]
    h = _rms(x, g_ref[...]).astype(BF16)
    acc = None
    for lo, hi in chunks:
        gate = jnp.dot(h, wgu_ref[:, lo:hi], preferred_element_type=F32)
        up = jnp.dot(h, wgu_ref[:, d_ff + lo:d_ff + hi], preferred_element_type=F32)
        act = (gate * (1.0 / (1.0 + jnp.exp(-gate))) * up).astype(BF16)
        part = jnp.dot(act, wd_ref[lo:hi, :], preferred_element_type=F32)
        acc = part if acc is None else acc + part
    y = x + 0.5 * acc
    if final_norm:
        y = _rms(y, gf_ref[...])
    o_ref[...] = y


def _ffn(x, g, w_gu, w_down, layer, g_final, *, final_norm):
    s, d = x.shape
    d_ff = w_down.shape[1]
    tm = FFN_TILE
    assert s % tm == 0 and d_ff % MXU_DEPTH == 0
    split = (d_ff // MXU_DEPTH + 1) // 2 * MXU_DEPTH
    chunks = ((0, split), (split, d_ff))
    return pl.pallas_call(
        functools.partial(_ffn_kernel, chunks=chunks, final_norm=final_norm),
        grid=(s // tm,),
        in_specs=[
            pl.BlockSpec((tm, d), lambda i: (i, 0)),
            _const((1, d)),
            _layer_weight(w_gu, layer),
            _layer_weight(w_down, layer),
            _const((1, d)),
        ],
        out_specs=pl.BlockSpec((tm, d), lambda i: (i, 0)),
        out_shape=jax.ShapeDtypeStruct((s, d), F32),
        compiler_params=pltpu.CompilerParams(
            dimension_semantics=("arbitrary",), vmem_limit_bytes=VMEM_LIMIT),
        name="ffn",
    )(x, g, w_gu, w_down, g_final)


def _cast_kernel(w_ref, o_ref):
    o_ref[...] = w_ref[...].astype(o_ref.dtype)


def _to_bf16(w):
    layers, rows, cols = w.shape
    tr = max(r for r in range(16, rows + 1, 16)
             if rows % r == 0 and r * cols * 4 <= CAST_BLOCK_BYTES)
    spec = pl.BlockSpec((None, tr, cols), lambda a, b: (a, b, 0))
    return pl.pallas_call(
        _cast_kernel,
        grid=(layers, rows // tr),
        in_specs=[spec],
        out_specs=spec,
        out_shape=jax.ShapeDtypeStruct(w.shape, BF16),
        compiler_params=pltpu.CompilerParams(
            dimension_semantics=("arbitrary", "arbitrary"), vmem_limit_bytes=VMEM_LIMIT),
        name="cast_bf16",
    )(w)


def _cast_mix_in_kernel(w_ref, spread_ref, o_ref, gate_ref):
    w = w_ref[...]
    o_ref[...] = w.astype(BF16)
    cols = w.shape[1]
    tail = w[:, cols - LANES:cols].astype(BF16)
    gate_ref[...] = jnp.dot(tail, spread_ref[...], preferred_element_type=F32).astype(BF16)


def _cast_mix_in(w):
    layers, rows, cols = w.shape
    tr = max(r for r in range(16, rows + 1, 16)
             if rows % r == 0 and r * cols * 4 <= CAST_BLOCK_BYTES)
    spread = np.zeros((LANES, LANES), np.float32)
    for h in range(N_HEADS):
        spread[LANES - N_HEADS + h, h * SUBLANES:(h + 1) * SUBLANES] = 1.0
    return pl.pallas_call(
        _cast_mix_in_kernel,
        grid=(layers, rows // tr),
        in_specs=[pl.BlockSpec((None, tr, cols), lambda a, b: (a, b, 0)),
                  pl.BlockSpec((LANES, LANES), lambda a, b: (0, 0))],
        out_specs=[pl.BlockSpec((None, tr, cols), lambda a, b: (a, b, 0)),
                   pl.BlockSpec((None, tr, LANES), lambda a, b: (a, b, 0))],
        out_shape=[jax.ShapeDtypeStruct(w.shape, BF16),
                   jax.ShapeDtypeStruct((layers, rows, LANES), BF16)],
        compiler_params=pltpu.CompilerParams(
            dimension_semantics=("arbitrary", "arbitrary"), vmem_limit_bytes=VMEM_LIMIT),
        name="cast_mix_in",
    )(w, jnp.asarray(spread, BF16))


def _mix_kernel(x_ref, g_ref, win_ref, wf_ref, bf_ref, wconv_ref, gconv_ref, tri_ref, hsum_ref,
                ycn_ref, qt_ref, k_ref, vt_ref, c_ref, qn_ref, kn_ref, u_scr, carry_scr):
    i = pl.program_id(0)
    tm = x_ref.shape[0]
    cw, aw = CONV_WIDTH, ATT_WIDTH

    @pl.when(i == 0)
    def _():
        u_scr[0:8, :] = jnp.zeros((8, cw), F32)
        carry_scr[...] = jnp.zeros_like(carry_scr)

    h = _rms(x_ref[...], g_ref[...]).astype(BF16)
    qk = jnp.dot(h, win_ref[:, 3 * cw:3 * cw + 2 * aw], preferred_element_type=F32)

    zf = jnp.dot(h, wf_ref[...], preferred_element_type=F32) + bf_ref[...]
    logf = jnp.minimum(zf, 0.0) - jnp.log1p(jnp.exp(-jnp.abs(zf)))
    tri = tri_ref[...]
    c = carry_scr[7:8, :]
    for piece in _split3(logf):
        c = c + jnp.dot(tri, piece.astype(BF16), preferred_element_type=F32)
    carry_scr[...] = c[tm - 8:tm, :]
    c2 = c * LOG2E
    c_ref[...] = c2

    hi, mid, lo = _split3(c2)
    lane = lax.broadcasted_iota(jnp.int32, c.shape, 1)
    r = lane % SUBLANES
    used = lane < N_HEADS * SUBLANES
    one = jnp.ones_like(c)
    zero = jnp.zeros_like(c)
    q_bias = jnp.where(r == 0, hi, jnp.where(r == 1, mid, jnp.where(r == 2, lo, jnp.where(r < 6, one, zero))))
    k_bias = jnp.where(r < 3, one, jnp.where(r == 3, -hi, jnp.where(r == 4, -mid, jnp.where(r == 5, -lo, zero))))
    q_bias_t = jnp.transpose(jnp.where(used, q_bias, zero))
    k_bias = jnp.where(used, k_bias, zero)

    z_cv = jnp.dot(h, win_ref[:, cw:3 * cw], preferred_element_type=F32)
    v = jnp.dot(h, win_ref[:, 3 * cw + 2 * aw:3 * cw + 3 * aw], preferred_element_type=F32)
    z_b = jnp.dot(h, win_ref[:, 0:cw], preferred_element_type=F32)

    q_t = jnp.transpose(qk[:, 0:aw] * (HEAD_DIM ** -0.5 * LOG2E))
    k = qk[:, aw:2 * aw]
    pad_rows = jnp.zeros((LANES - HEAD_DIM - SUBLANES, tm), F32)
    head_lane = lane < HEAD_DIM
    qn = []
    for hh in range(N_HEADS):
        rows = slice(hh * HEAD_DIM, (hh + 1) * HEAD_DIM)
        blk = slice(hh * LANES, (hh + 1) * LANES)
        q_h = q_t[rows]
        qt_ref[blk, :] = jnp.concatenate(
            [q_h, q_bias_t[hh * SUBLANES:(hh + 1) * SUBLANES], pad_rows], axis=0).astype(BF16)

        pair = k[:, (hh // 2) * LANES:(hh // 2 + 1) * LANES]
        k_pos = pair if hh % 2 == 0 else pltpu.roll(pair, HEAD_DIM, 1)
        bias = pltpu.roll(k_bias, (HEAD_DIM - hh * SUBLANES) % LANES, 1)
        k_h = jnp.where(head_lane, k_pos, bias).astype(BF16)
        k_ref[:, blk] = k_h

        qf = q_h.astype(BF16).astype(F32)
        qn.append(jnp.max(jnp.sum(qf * qf, axis=0, keepdims=True), axis=1, keepdims=True))
    qn_ref[...] = jnp.concatenate([jnp.broadcast_to(n, (1, LANES)) for n in qn], axis=0)

    kf = k.astype(BF16).astype(F32)
    kn = jnp.dot((kf * kf).astype(BF16), hsum_ref[...], preferred_element_type=F32)
    kn_ref[...] = jnp.max(kn.reshape(tm // SUBLANES, SUBLANES, LANES), axis=0)

    u = z_cv[:, 0:cw] * z_cv[:, cw:2 * cw]
    u_scr[8:tm + 8, :] = u
    u1 = u_scr[7:tm + 7, :]
    u2 = u_scr[6:tm + 6, :]
    w = wconv_ref[...]
    conv = w[2:3, :] * u + w[1:2, :] * u1 + w[0:1, :] * u2
    u_scr[0:8, :] = u[tm - 8:tm, :]

    v_t = jnp.transpose(v.astype(BF16).astype(F32))
    ones_rows = jnp.where(lax.broadcasted_iota(jnp.int32, (SUBLANES, tm), 0) == 0, 1.0, 0.0)
    for hh in range(N_HEADS):
        vt_ref[hh * LANES:(hh + 1) * LANES, :] = jnp.concatenate(
            [v_t[hh * HEAD_DIM:(hh + 1) * HEAD_DIM], ones_rows, pad_rows], axis=0).astype(BF16)

    ycn_ref[...] = _rms(z_b * conv, gconv_ref[...]).astype(BF16)


def _mix(x, g, w_in, layer, wf, bfr, wconv, gconv):
    s, d = x.shape
    tm = ROW_TILE
    hp = N_HEADS * LANES
    nt = s // tm
    tri = jnp.asarray(np.tril(np.ones((tm, tm), np.float32)), BF16)
    hsum = np.zeros((ATT_WIDTH, LANES), np.float32)
    hsum[np.arange(ATT_WIDTH), np.arange(ATT_WIDTH) // HEAD_DIM] = 1.0
    row = lambda width: pl.BlockSpec((tm, width), lambda i: (i, 0))
    col = lambda height: pl.BlockSpec((height, tm), lambda i: (0, i))
    return pl.pallas_call(
        _mix_kernel,
        grid=(nt,),
        in_specs=[
            row(d), _const((1, d)), _layer_weight(w_in, layer), _layer_weight(wf, layer),
            _const((1, LANES)), _const((CONV_K, CONV_WIDTH)), _const((1, CONV_WIDTH)),
            _const((tm, tm)), _const((ATT_WIDTH, LANES)),
        ],
        out_specs=[row(CONV_WIDTH), col(hp), row(hp), col(hp), row(LANES),
                   pl.BlockSpec((N_HEADS, LANES), lambda i: (i, 0)),
                   pl.BlockSpec((N_HEADS, LANES), lambda i: (i, 0))],
        out_shape=[
            jax.ShapeDtypeStruct((s, CONV_WIDTH), BF16),
            jax.ShapeDtypeStruct((hp, s), BF16),
            jax.ShapeDtypeStruct((s, hp), BF16),
            jax.ShapeDtypeStruct((hp, s), BF16),
            jax.ShapeDtypeStruct((s, LANES), F32),
            jax.ShapeDtypeStruct((nt * N_HEADS, LANES), F32),
            jax.ShapeDtypeStruct((nt * N_HEADS, LANES), F32),
        ],
        scratch_shapes=[pltpu.VMEM((tm + 8, CONV_WIDTH), F32), pltpu.VMEM((8, LANES), F32)],
        compiler_params=pltpu.CompilerParams(
            dimension_semantics=("arbitrary",), vmem_limit_bytes=VMEM_LIMIT),
        name="mix_in",
    )(x, g, w_in, wf, bfr, wconv, gconv, tri, jnp.asarray(hsum, BF16))


def _att_kernel(cst_ref, cen_ref, ub_ref, qt_ref, k_ref, vt_ref, tri_ref, o_ref, m_scr, acc_scr):
    group = pl.program_id(0)
    i = pl.program_id(1)
    t = qt_ref.shape[1]
    heads = range(ATT_HEADS)

    def blocks_past_first_two(hh):
        hd = group * ATT_HEADS + hh
        floor = -((F32_EXP_UNDERFLOW + BOUND_SLACK) * LOG2E + 2.0 * ub_ref[hd, i])
        c_first = cst_ref[hd, i]

        def needed(j):
            return jnp.logical_and(j > 0, c_first - cen_ref[hd, jnp.maximum(j - 1, 0)] >= floor)

        j_lo = lax.while_loop(needed, lambda j: j - 1, jnp.maximum(i - 1, 0))
        return jnp.maximum(i - 1, 0) - j_lo

    n_rest = [blocks_past_first_two(hh) for hh in heads]

    def scores(hh, start, size):
        return jnp.dot(k_ref[pl.ds(start, size), hh * LANES:(hh + 1) * LANES],
                       qt_ref[hh * LANES:(hh + 1) * LANES, :], preferred_element_type=F32)

    def values(hh, start, size, p):
        return jnp.dot(vt_ref[hh * LANES:hh * LANES + V_ROWS, pl.ds(start, size)], p,
                       preferred_element_type=F32)

    def run(first, rest):
        @pl.when(i == 0)
        def _():
            for hh in heads:
                first(hh, scores(hh, 0, t) + tri_ref[...], 0, t)

        @pl.when(i > 0)
        def _():
            start = pl.multiple_of((i - 1) * t, t)
            for hh in heads:
                s = scores(hh, start, 2 * t)
                first(hh, jnp.concatenate([s[:t], s[t:] + tri_ref[...]], axis=0), start, 2 * t)

        for hh in heads:
            def body(n, carry, hh=hh):
                start = pl.multiple_of((i - 2 - n) * t, t)
                rest(hh, scores(hh, start, t), start, t)
                return carry

            lax.fori_loop(0, n_rest[hh], body, 0)

    unshifted = ub_ref[group * ATT_HEADS, i] <= NO_SHIFT_BOUND
    for hh in heads[1:]:
        unshifted = jnp.logical_and(unshifted, ub_ref[group * ATT_HEADS + hh, i] <= NO_SHIFT_BOUND)

    @pl.when(unshifted)
    def _():
        def first(hh, s, start, size):
            acc_scr[hh] = values(hh, start, size, jnp.exp2(s).astype(BF16))

        def rest(hh, s, start, size):
            acc_scr[hh] += values(hh, start, size, jnp.exp2(s).astype(BF16))

        run(first, rest)

    @pl.when(jnp.logical_not(unshifted))
    def _():
        def first(hh, s, start, size):
            m = jnp.max(s, axis=0, keepdims=True)
            m_scr[hh] = m
            acc_scr[hh] = values(hh, start, size, jnp.exp2(s - m).astype(BF16))

        def rest(hh, s, start, size):
            m_old = m_scr[hh]
            m_new = jnp.maximum(m_old, jnp.max(s, axis=0, keepdims=True))
            m_scr[hh] = m_new
            acc_scr[hh] = jnp.exp2(m_old - m_new) * acc_scr[hh] + values(
                hh, start, size, jnp.exp2(s - m_new).astype(BF16))

        run(first, rest)

    for hh in heads:
        acc = acc_scr[hh]
        o_ref[hh * HEAD_DIM:(hh + 1) * HEAD_DIM, :] = acc[0:HEAD_DIM, :] / acc[HEAD_DIM:HEAD_DIM + 1, :]


def _attention(qt, k, vt, c, qn, kn):
    s = k.shape[0]
    t = ATT_TILE
    nq = s // t
    c_heads = c[:, 0:N_HEADS * SUBLANES:SUBLANES]
    c_first = c_heads[0::t].T
    c_last = c_heads[t - 1::t].T
    qn = qn.reshape(nq, N_HEADS, LANES)[:, :, 0].T
    kn = jnp.max(kn, axis=0)[:N_HEADS] * NORM_SLACK
    qk_bound = jnp.sqrt(qn * kn[:, None])
    tri = jnp.asarray(np.where(np.arange(t)[:, None] <= np.arange(t)[None, :], 0.0, -np.inf), F32)
    grid_spec = pltpu.PrefetchScalarGridSpec(
        num_scalar_prefetch=3,
        grid=(N_HEADS // ATT_HEADS, nq),
        in_specs=[
            pl.BlockSpec((ATT_HEADS * LANES, t), lambda g, i, *_: (g, i)),
            pl.BlockSpec((s, ATT_HEADS * LANES), lambda g, i, *_: (0, g)),
            pl.BlockSpec((ATT_HEADS * LANES, s), lambda g, i, *_: (g, 0)),
            pl.BlockSpec((t, t), lambda g, i, *_: (0, 0)),
        ],
        out_specs=pl.BlockSpec((ATT_HEADS * HEAD_DIM, t), lambda g, i, *_: (g, i)),
        scratch_shapes=[pltpu.VMEM((ATT_HEADS, 1, t), F32), pltpu.VMEM((ATT_HEADS, V_ROWS, t), F32)],
    )
    return pl.pallas_call(
        _att_kernel,
        grid_spec=grid_spec,
        out_shape=jax.ShapeDtypeStruct((ATT_WIDTH, s), F32),
        compiler_params=pltpu.CompilerParams(
            dimension_semantics=("arbitrary", "arbitrary"), vmem_limit_bytes=VMEM_LIMIT),
        name="fox_attention",
    )(c_first, c_last, qk_bound, qt, k, vt, tri)


def _memkv_kernel(mem_ref, g_ref, w_ref, k_ref, v_ref):
    d = mem_ref.shape[1]
    m = _rms(mem_ref[...], g_ref[...]).astype(BF16)
    kv = jnp.dot(m, w_ref[...], preferred_element_type=F32)
    k_ref[...] = kv[:, :d].astype(BF16)
    v_ref[...] = kv[:, d:].astype(BF16)


def _memkv(mem, g, w_kv, layer):
    n, d = mem.shape
    whole = lambda shape: pl.BlockSpec(shape, lambda i: (0, 0))
    return pl.pallas_call(
        _memkv_kernel,
        grid=(1,),
        in_specs=[whole((n, d)), whole((1, d)), _layer_weight(w_kv, layer)],
        out_specs=[whole((n, d)), whole((n, d))],
        out_shape=[jax.ShapeDtypeStruct((n, d), BF16), jax.ShapeDtypeStruct((n, d), BF16)],
        compiler_params=pltpu.CompilerParams(
            dimension_semantics=("arbitrary",), vmem_limit_bytes=VMEM_LIMIT),
        name="mem_kv",
    )(mem, g, w_kv)


def _out_xattn_kernel(x_ref, ycn_ref, yat_ref, gatt_ref, wout_ref, gx_ref, wxq_ref,
                      kx_ref, vx_ref, wxo_ref, o_ref):
    d = x_ref.shape[1]
    xd = d // N_XHEADS
    ya_t = yat_ref[...]
    ms = jnp.mean(ya_t * ya_t, axis=0, keepdims=True)
    ya = jnp.transpose(ya_t * lax.rsqrt(ms + EPS))
    y = jnp.concatenate([ycn_ref[...], (ya * gatt_ref[...]).astype(BF16)], axis=-1)
    x1 = x_ref[...] + jnp.dot(y, wout_ref[...], preferred_element_type=F32)

    hx = _rms(x1, gx_ref[...]).astype(BF16)
    q = (jnp.dot(hx, wxq_ref[...], preferred_element_type=F32) * (xd ** -0.5)).astype(BF16)
    nt = (((1,), (1,)), ((), ()))
    heads = []
    for hh in range(N_XHEADS):
        sl = slice(hh * xd, (hh + 1) * xd)
        s = lax.dot_general(q[:, sl], kx_ref[:, sl], nt, preferred_element_type=F32)
        p = jnp.exp(s - jnp.max(s, axis=-1, keepdims=True))
        denom = jnp.sum(p, axis=-1, keepdims=True)
        o = jnp.dot(p.astype(BF16), vx_ref[:, sl], preferred_element_type=F32) / denom
        heads.append(o.astype(BF16))
    o = jnp.concatenate(heads, axis=-1)
    o_ref[...] = x1 + jnp.dot(o, wxo_ref[...], preferred_element_type=F32)


def _out_xattn(x, ycn, ya_t, gatt, wout, gx, wxq, kx, vx, wxo, layer):
    s, d = x.shape
    tm = ROW_TILE
    n_mem = kx.shape[0]
    row = lambda width: pl.BlockSpec((tm, width), lambda i: (i, 0))
    return pl.pallas_call(
        _out_xattn_kernel,
        grid=(s // tm,),
        in_specs=[
            row(d), row(CONV_WIDTH), pl.BlockSpec((ATT_WIDTH, tm), lambda i: (0, i)),
            _const((1, ATT_WIDTH)), _layer_weight(wout, layer), _const((1, d)),
            _layer_weight(wxq, layer), _const((n_mem, d)), _const((n_mem, d)),
            _layer_weight(wxo, layer),
        ],
        out_specs=row(d),
        out_shape=jax.ShapeDtypeStruct((s, d), F32),
        compiler_params=pltpu.CompilerParams(
            dimension_semantics=("arbitrary",), vmem_limit_bytes=VMEM_LIMIT),
        name="out_xattn",
    )(x, ycn, ya_t, gatt, wout, gx, wxq, kx, vx, wxo)


def kernel(x, mem, g_ffn1, w_ffn1_gu, w_ffn1_down, g_mix, w_mix_in, w_conv, b_f, g_conv_out,
           g_att_out, w_mix_out, g_xattn, g_mem, w_xq, w_xkv, w_xo, g_ffn2, w_ffn2_gu,
           w_ffn2_down, g_final):
    b, s, d = x.shape
    assert b == 1 and w_mix_in.shape[2] == 3 * CONV_WIDTH + 3 * ATT_WIDTH + N_HEADS
    depth = g_ffn1.shape[0]
    xs = x.reshape(s, d)
    mem2 = mem.reshape(mem.shape[1], d)
    row = lambda a: a.reshape(1, -1).astype(F32)
    gfin = row(g_final)
    w1gu, w1d, w2gu, w2d, wxq, wxkv, wxo, wout = map(
        _to_bf16, (w_ffn1_gu, w_ffn1_down, w_ffn2_gu, w_ffn2_down, w_xq, w_xkv, w_xo, w_mix_out))
    win, wf = _cast_mix_in(w_mix_in)
    gate_lanes = (0, LANES - N_HEADS * SUBLANES)

    for l in range(depth):
        xs = _ffn(xs, row(g_ffn1[l]), w1gu, w1d, l, gfin, final_norm=False)

        bfr = jnp.pad(jnp.repeat(b_f[l], SUBLANES), gate_lanes).reshape(1, LANES).astype(F32)
        ycn, qt, k, vt, c, qn, kn = _mix(xs, row(g_mix[l]), win, l, wf, bfr,
                                         w_conv[l].astype(F32), row(g_conv_out[l]))

        ya_t = _attention(qt, k, vt, c, qn, kn)

        kx, vx = _memkv(mem2, row(g_mem[l]), wxkv, l)
        xs = _out_xattn(xs, ycn, ya_t, row(g_att_out[l]), wout, row(g_xattn[l]), wxq, kx, vx,
                        wxo, l)

        xs = _ffn(xs, row(g_ffn2[l]), w2gu, w2d, l, gfin, final_norm=(l == depth - 1))
    return xs.reshape(b, s, d)
```

```python
import functools

import numpy as np
import jax
import jax.numpy as jnp
from jax import lax
from jax.experimental import pallas as pl
from jax.experimental.pallas import tpu as pltpu

F32 = jnp.float32
BF16 = jnp.bfloat16

EPS = 1e-6
LANES = 128
SUBLANES = 8
MXU_DEPTH = 256
HEAD_DIM = 64
N_HEADS = 8
CONV_WIDTH = 512
ATT_WIDTH = N_HEADS * HEAD_DIM
CONV_K = 3
N_XHEADS = 4
N_SPLIT = 3
F32_EXP_UNDERFLOW = 104.0
BOUND_SLACK = 1.0
NORM_SLACK = 1.01
LOG2E = 1.4426950408889634
NO_SHIFT_BOUND = 40.0 * LOG2E
VMEM_LIMIT = 48 * 1024 * 1024
CAST_BLOCK_BYTES = 6 * 1024 * 1024

ROW_TILE = 512
FFN_TILE = 1024
ATT_TILE = ROW_TILE
ATT_HEADS = 2
V_ROWS = 80


def _rms(x, g):
    ms = jnp.mean(x * x, axis=-1, keepdims=True)
    return x * lax.rsqrt(ms + EPS) * g


def _split3(x):
    hi = x.astype(BF16).astype(F32)
    r = x - hi
    mid = r.astype(BF16).astype(F32)
    lo = (r - mid).astype(BF16).astype(F32)
    return hi, mid, lo


def _layer_weight(w, layer):
    return pl.BlockSpec((None,) + w.shape[1:], lambda i: (layer, 0, 0), pipeline_mode=pl.Buffered(1))


def _const(shape):
    return pl.BlockSpec(shape, lambda i: (0, 0), pipeline_mode=pl.Buffered(1))


def _ffn_kernel(x_ref, g_ref, wgu_ref, wd_ref, gf_ref, o_ref, *, chunks, final_norm):
    d_ff = wd_ref.shape[0]
    x = x_ref[...]
    h = _rms(x, g_ref[...]).astype(BF16)
    acc = None
    for lo, hi in chunks:
        gate = jnp.dot(h, wgu_ref[:, lo:hi], preferred_element_type=F32)
        up = jnp.dot(h, wgu_ref[:, d_ff + lo:d_ff + hi], preferred_element_type=F32)
        act = (gate * (1.0 / (1.0 + jnp.exp(-gate))) * up).astype(BF16)
        part = jnp.dot(act, wd_ref[lo:hi, :], preferred_element_type=F32)
        acc = part if acc is None else acc + part
    y = x + 0.5 * acc
    if final_norm:
        y = _rms(y, gf_ref[...])
    o_ref[...] = y


def _ffn(x, g, w_gu, w_down, layer, g_final, *, final_norm):
    s, d = x.shape
    d_ff = w_down.shape[1]
    tm = FFN_TILE
    assert s % tm == 0 and d_ff % MXU_DEPTH == 0
    split = (d_ff // MXU_DEPTH + 1) // 2 * MXU_DEPTH
    chunks = ((0, split), (split, d_ff))
    return pl.pallas_call(
        functools.partial(_ffn_kernel, chunks=chunks, final_norm=final_norm),
        grid=(s // tm,),
        in_specs=[
            pl.BlockSpec((tm, d), lambda i: (i, 0)),
            _const((1, d)),
            _layer_weight(w_gu, layer),
            _layer_weight(w_down, layer),
            _const((1, d)),
        ],
        out_specs=pl.BlockSpec((tm, d), lambda i: (i, 0)),
        out_shape=jax.ShapeDtypeStruct((s, d), F32),
        compiler_params=pltpu.CompilerParams(
            dimension_semantics=("arbitrary",), vmem_limit_bytes=VMEM_LIMIT),
        name="ffn",
    )(x, g, w_gu, w_down, g_final)


def _cast_kernel(w_ref, o_ref):
    o_ref[...] = w_ref[...].astype(o_ref.dtype)


def _to_bf16(w):
    layers, rows, cols = w.shape
    tr = max(r for r in range(16, rows + 1, 16)
             if rows % r == 0 and r * cols * 4 <= CAST_BLOCK_BYTES)
    spec = pl.BlockSpec((None, tr, cols), lambda a, b: (a, b, 0))
    return pl.pallas_call(
        _cast_kernel,
        grid=(layers, rows // tr),
        in_specs=[spec],
        out_specs=spec,
        out_shape=jax.ShapeDtypeStruct(w.shape, BF16),
        compiler_params=pltpu.CompilerParams(
            dimension_semantics=("arbitrary", "arbitrary"), vmem_limit_bytes=VMEM_LIMIT),
        name="cast_bf16",
    )(w)


def _cast_mix_in_kernel(w_ref, spread_ref, o_ref, gate_ref):
    w = w_ref[...]
    o_ref[...] = w.astype(BF16)
    cols = w.shape[1]
    tail = w[:, cols - LANES:cols].astype(BF16)
    gate_ref[...] = jnp.dot(tail, spread_ref[...], preferred_element_type=F32).astype(BF16)


def _cast_mix_in(w):
    layers, rows, cols = w.shape
    tr = max(r for r in range(16, rows + 1, 16)
             if rows % r == 0 and r * cols * 4 <= CAST_BLOCK_BYTES)
    spread = np.zeros((LANES, LANES), np.float32)
    for h in range(N_HEADS):
        spread[LANES - N_HEADS + h, h * SUBLANES:(h + 1) * SUBLANES] = 1.0
    return pl.pallas_call(
        _cast_mix_in_kernel,
        grid=(layers, rows // tr),
        in_specs=[pl.BlockSpec((None, tr, cols), lambda a, b: (a, b, 0)),
                  pl.BlockSpec((LANES, LANES), lambda a, b: (0, 0))],
        out_specs=[pl.BlockSpec((None, tr, cols), lambda a, b: (a, b, 0)),
                   pl.BlockSpec((None, tr, LANES), lambda a, b: (a, b, 0))],
        out_shape=[jax.ShapeDtypeStruct(w.shape, BF16),
                   jax.ShapeDtypeStruct((layers, rows, LANES), BF16)],
        compiler_params=pltpu.CompilerParams(
            dimension_semantics=("arbitrary", "arbitrary"), vmem_limit_bytes=VMEM_LIMIT),
        name="cast_mix_in",
    )(w, jnp.asarray(spread, BF16))


def _mix_kernel(x_ref, g_ref, win_ref, wf_ref, bf_ref, wconv_ref, gconv_ref, hsum_ref,
                ycn_ref, qt_ref, k_ref, vt_ref, c_ref, qn_ref, kn_ref, u_scr, carry_scr):
    i = pl.program_id(0)
    tm = x_ref.shape[0]
    cw, aw = CONV_WIDTH, ATT_WIDTH

    @pl.when(i == 0)
    def _():
        u_scr[0:8, :] = jnp.zeros((8, cw), F32)
        carry_scr[...] = jnp.zeros_like(carry_scr)

    h = _rms(x_ref[...], g_ref[...]).astype(BF16)
    zf = jnp.dot(h, wf_ref[...], preferred_element_type=F32) + bf_ref[...]
    qk = jnp.dot(h, win_ref[:, 3 * cw:3 * cw + 2 * aw], preferred_element_type=F32)

    logf = jnp.minimum(zf, 0.0) - jnp.log1p(jnp.exp(-jnp.abs(zf)))
    c = logf
    row_id = lax.broadcasted_iota(jnp.int32, c.shape, 0)
    shift = 1
    while shift < tm:
        c = c + jnp.where(row_id >= shift, pltpu.roll(c, shift, 0), 0.0)
        shift *= 2
    c = c + carry_scr[7:8, :]
    carry_scr[...] = c[tm - 8:tm, :]
    c2 = c * LOG2E
    c_ref[...] = c2

    hi, mid, lo = _split3(c2)
    lane = lax.broadcasted_iota(jnp.int32, c.shape, 1)
    r = lane % SUBLANES
    used = lane < N_HEADS * SUBLANES
    one = jnp.ones_like(c)
    zero = jnp.zeros_like(c)
    q_bias = jnp.where(r == 0, hi, jnp.where(r == 1, mid, jnp.where(r == 2, lo, jnp.where(r < 6, one, zero))))
    k_bias = jnp.where(r < 3, one, jnp.where(r == 3, -hi, jnp.where(r == 4, -mid, jnp.where(r == 5, -lo, zero))))
    q_bias_t = jnp.transpose(jnp.where(used, q_bias, zero))
    k_bias = jnp.where(used, k_bias, zero)

    z_cv = jnp.dot(h, win_ref[:, cw:3 * cw], preferred_element_type=F32)
    v = jnp.dot(h, win_ref[:, 3 * cw + 2 * aw:3 * cw + 3 * aw], preferred_element_type=F32)
    z_b = jnp.dot(h, win_ref[:, 0:cw], preferred_element_type=F32)

    q_t = jnp.transpose(qk[:, 0:aw] * (HEAD_DIM ** -0.5 * LOG2E))
    k = qk[:, aw:2 * aw]
    pad_rows = jnp.zeros((LANES - HEAD_DIM - SUBLANES, tm), F32)
    head_lane = lane < HEAD_DIM
    qn = []
    for hh in range(N_HEADS):
        rows = slice(hh * HEAD_DIM, (hh + 1) * HEAD_DIM)
        blk = slice(hh * LANES, (hh + 1) * LANES)
        q_h = q_t[rows]
        qt_ref[blk, :] = jnp.concatenate(
            [q_h, q_bias_t[hh * SUBLANES:(hh + 1) * SUBLANES], pad_rows], axis=0).astype(BF16)

        pair = k[:, (hh // 2) * LANES:(hh // 2 + 1) * LANES]
        k_pos = pair if hh % 2 == 0 else pltpu.roll(pair, HEAD_DIM, 1)
        bias = pltpu.roll(k_bias, (HEAD_DIM - hh * SUBLANES) % LANES, 1)
        k_h = jnp.where(head_lane, k_pos, bias).astype(BF16)
        k_ref[:, blk] = k_h

        qf = q_h.astype(BF16).astype(F32)
        qn.append(jnp.max(jnp.sum(qf * qf, axis=0, keepdims=True), axis=1, keepdims=True))
    qn_ref[...] = jnp.concatenate([jnp.broadcast_to(n, (1, LANES)) for n in qn], axis=0)

    kf = k.astype(BF16).astype(F32)
    kn = jnp.dot((kf * kf).astype(BF16), hsum_ref[...], preferred_element_type=F32)
    kn_ref[...] = jnp.max(kn.reshape(tm // SUBLANES, SUBLANES, LANES), axis=0)

    u = z_cv[:, 0:cw] * z_cv[:, cw:2 * cw]
    u_scr[8:tm + 8, :] = u
    u1 = u_scr[7:tm + 7, :]
    u2 = u_scr[6:tm + 6, :]
    w = wconv_ref[...]
    conv = w[2:3, :] * u + w[1:2, :] * u1 + w[0:1, :] * u2
    u_scr[0:8, :] = u[tm - 8:tm, :]

    v_t = jnp.transpose(v.astype(BF16).astype(F32))
    ones_rows = jnp.where(lax.broadcasted_iota(jnp.int32, (SUBLANES, tm), 0) == 0, 1.0, 0.0)
    for hh in range(N_HEADS):
        vt_ref[hh * LANES:(hh + 1) * LANES, :] = jnp.concatenate(
            [v_t[hh * HEAD_DIM:(hh + 1) * HEAD_DIM], ones_rows, pad_rows], axis=0).astype(BF16)

    ycn_ref[...] = _rms(z_b * conv, gconv_ref[...]).astype(BF16)


def _mix(x, g, w_in, layer, wf, bfr, wconv, gconv):
    s, d = x.shape
    tm = ROW_TILE
    hp = N_HEADS * LANES
    nt = s // tm
    hsum =np.zeros((ATT_WIDTH, LANES), np.float32)
    hsum[np.arange(ATT_WIDTH), np.arange(ATT_WIDTH) // HEAD_DIM] = 1.0
    row = lambda width: pl.BlockSpec((tm, width), lambda i: (i, 0))
    col = lambda height: pl.BlockSpec((height, tm), lambda i: (0, i))
    return pl.pallas_call(
        _mix_kernel,
        grid=(nt,),
        in_specs=[
            row(d), _const((1, d)), _layer_weight(w_in, layer), _layer_weight(wf, layer),
            _const((1, LANES)), _const((CONV_K, CONV_WIDTH)), _const((1, CONV_WIDTH)),
            _const((ATT_WIDTH, LANES)),
        ],
        out_specs=[row(CONV_WIDTH), col(hp), row(hp), col(hp), row(LANES),
                   pl.BlockSpec((N_HEADS, LANES), lambda i: (i, 0)),
                   pl.BlockSpec((N_HEADS, LANES), lambda i: (i, 0))],
        out_shape=[
            jax.ShapeDtypeStruct((s, CONV_WIDTH), BF16),
            jax.ShapeDtypeStruct((hp, s), BF16),
            jax.ShapeDtypeStruct((s, hp), BF16),
            jax.ShapeDtypeStruct((hp, s), BF16),
            jax.ShapeDtypeStruct((s, LANES), F32),
            jax.ShapeDtypeStruct((nt * N_HEADS, LANES), F32),
            jax.ShapeDtypeStruct((nt * N_HEADS, LANES), F32),
        ],
        scratch_shapes=[pltpu.VMEM((tm + 8, CONV_WIDTH), F32), pltpu.VMEM((8, LANES), F32)],
        compiler_params=pltpu.CompilerParams(
            dimension_semantics=("arbitrary",), vmem_limit_bytes=VMEM_LIMIT),
        name="mix_in",
    )(x, g, w_in, wf, bfr, wconv, gconv, jnp.asarray(hsum, BF16))


def _att_kernel(cst_ref, cen_ref, ub_ref, qt_ref, k_ref, vt_ref, tri_ref, o_ref, m_scr, acc_scr):
    group = pl.program_id(0)
    i = pl.program_id(1)
    t = qt_ref.shape[1]
    heads = range(ATT_HEADS)

    def blocks_past_first_two(hh):
        hd = group * ATT_HEADS + hh
        floor = -((F32_EXP_UNDERFLOW + BOUND_SLACK) * LOG2E + 2.0 * ub_ref[hd, i])
        c_first = cst_ref[hd, i]

        def needed(j):
            return jnp.logical_and(j > 0, c_first - cen_ref[hd, jnp.maximum(j - 1, 0)] >= floor)

        j_lo = lax.while_loop(needed, lambda j: j - 1, jnp.maximum(i - 1, 0))
        return jnp.maximum(i - 1, 0) - j_lo

    n_rest = [blocks_past_first_two(hh) for hh in heads]

    def scores(hh, start, size):
        return jnp.dot(k_ref[pl.ds(start, size), hh * LANES:(hh + 1) * LANES],
                       qt_ref[hh * LANES:(hh + 1) * LANES, :], preferred_element_type=F32)

    def values(hh, start, size, p):
        return jnp.dot(vt_ref[hh * LANES:hh * LANES + V_ROWS, pl.ds(start, size)], p,
                       preferred_element_type=F32)

    def run(first, rest):
        @pl.when(i == 0)
        def _():
            for hh in heads:
                first(hh, scores(hh, 0, t) + tri_ref[...], 0, t)

        @pl.when(i > 0)
        def _():
            start = pl.multiple_of((i - 1) * t, t)
            for hh in heads:
                s = scores(hh, start, 2 * t)
                first(hh, jnp.concatenate([s[:t], s[t:] + tri_ref[...]], axis=0), start, 2 * t)

        for hh in heads:
            def body(n, carry, hh=hh):
                start = pl.multiple_of((i - 2 - n) * t, t)
                rest(hh, scores(hh, start, t), start, t)
                return carry

            lax.fori_loop(0, n_rest[hh], body, 0)

    unshifted = ub_ref[group * ATT_HEADS, i] <= NO_SHIFT_BOUND
    for hh in heads[1:]:
        unshifted = jnp.logical_and(unshifted, ub_ref[group * ATT_HEADS + hh, i] <= NO_SHIFT_BOUND)

    @pl.when(unshifted)
    def _():
        def first(hh, s, start, size):
            acc_scr[hh] = values(hh, start, size, jnp.exp2(s).astype(BF16))

        def rest(hh, s, start, size):
            acc_scr[hh] += values(hh, start, size, jnp.exp2(s).astype(BF16))

        run(first, rest)

    @pl.when(jnp.logical_not(unshifted))
    def _():
        def first(hh, s, start, size):
            m = jnp.max(s, axis=0, keepdims=True)
            m_scr[hh] = m
            acc_scr[hh] = values(hh, start, size, jnp.exp2(s - m).astype(BF16))

        def rest(hh, s, start, size):
            m_old = m_scr[hh]
            m_new = jnp.maximum(m_old, jnp.max(s, axis=0, keepdims=True))
            m_scr[hh] = m_new
            acc_scr[hh] = jnp.exp2(m_old - m_new) * acc_scr[hh] + values(
                hh, start, size, jnp.exp2(s - m_new).astype(BF16))

        run(first, rest)

    for hh in heads:
        acc = acc_scr[hh]
        o_ref[hh * HEAD_DIM:(hh + 1) * HEAD_DIM, :] = acc[0:HEAD_DIM, :] / acc[HEAD_DIM:HEAD_DIM + 1, :]


def _attention(qt, k, vt, c, qn, kn):
    s = k.shape[0]
    t = ATT_TILE
    nq = s // t
    c_heads = c[:, 0:N_HEADS * SUBLANES:SUBLANES]
    c_first = c_heads[0::t].T
    c_last = c_heads[t - 1::t].T
    qn = qn.reshape(nq, N_HEADS, LANES)[:, :, 0].T
    kn = jnp.max(kn, axis=0)[:N_HEADS] * NORM_SLACK
    qk_bound = jnp.sqrt(qn * kn[:, None])
    tri = jnp.asarray(np.where(np.arange(t)[:, None] <= np.arange(t)[None, :], 0.0, -np.inf), F32)
    grid_spec = pltpu.PrefetchScalarGridSpec(
        num_scalar_prefetch=3,
        grid=(N_HEADS // ATT_HEADS, nq),
        in_specs=[
            pl.BlockSpec((ATT_HEADS * LANES, t), lambda g, i, *_: (g, i)),
            pl.BlockSpec((s, ATT_HEADS * LANES), lambda g, i, *_: (0, g)),
            pl.BlockSpec((ATT_HEADS * LANES, s), lambda g, i, *_: (g, 0)),
            pl.BlockSpec((t, t), lambda g, i, *_: (0, 0)),
        ],
        out_specs=pl.BlockSpec((ATT_HEADS * HEAD_DIM, t), lambda g, i, *_: (g, i)),
        scratch_shapes=[pltpu.VMEM((ATT_HEADS, 1, t), F32), pltpu.VMEM((ATT_HEADS, V_ROWS, t), F32)],
    )
    return pl.pallas_call(
        _att_kernel,
        grid_spec=grid_spec,
        out_shape=jax.ShapeDtypeStruct((ATT_WIDTH, s), F32),
        compiler_params=pltpu.CompilerParams(
            dimension_semantics=("arbitrary", "arbitrary"), vmem_limit_bytes=VMEM_LIMIT),
        name="fox_attention",
    )(c_first, c_last, qk_bound, qt, k, vt, tri)


def _memkv_kernel(mem_ref, g_ref, w_ref, k_ref, v_ref):
    d = mem_ref.shape[1]
    m = _rms(mem_ref[...], g_ref[...]).astype(BF16)
    kv = jnp.dot(m, w_ref[...], preferred_element_type=F32)
    k_ref[...] = kv[:, :d].astype(BF16)
    v_ref[...] = kv[:, d:].astype(BF16)


def _memkv(mem, g, w_kv, layer):
    n, d = mem.shape
    whole = lambda shape: pl.BlockSpec(shape, lambda i: (0, 0))
    return pl.pallas_call(
        _memkv_kernel,
        grid=(1,),
        in_specs=[whole((n, d)), whole((1, d)), _layer_weight(w_kv, layer)],
        out_specs=[whole((n, d)), whole((n, d))],
        out_shape=[jax.ShapeDtypeStruct((n, d), BF16), jax.ShapeDtypeStruct((n, d), BF16)],
        compiler_params=pltpu.CompilerParams(
            dimension_semantics=("arbitrary",), vmem_limit_bytes=VMEM_LIMIT),
        name="mem_kv",
    )(mem, g, w_kv)


def _out_xattn_kernel(x_ref, ycn_ref, yat_ref, gatt_ref, wout_ref, gx_ref, wxq_ref,
                      kx_ref, vx_ref, wxo_ref, o_ref):
    d = x_ref.shape[1]
    xd = d // N_XHEADS
    ya_t = yat_ref[...]
    ms = jnp.mean(ya_t * ya_t, axis=0, keepdims=True)
    ya = jnp.transpose(ya_t * lax.rsqrt(ms + EPS))
    y = jnp.concatenate([ycn_ref[...], (ya * gatt_ref[...]).astype(BF16)], axis=-1)
    x1 = x_ref[...] + jnp.dot(y, wout_ref[...], preferred_element_type=F32)

    hx = _rms(x1, gx_ref[...]).astype(BF16)
    q = (jnp.dot(hx, wxq_ref[...], preferred_element_type=F32) * (xd ** -0.5)).astype(BF16)
    nt = (((1,), (1,)), ((), ()))
    heads = []
    for hh in range(N_XHEADS):
        sl = slice(hh * xd, (hh + 1) * xd)
        s = lax.dot_general(q[:, sl], kx_ref[:, sl], nt, preferred_element_type=F32)
        p = jnp.exp(s - jnp.max(s, axis=-1, keepdims=True))
        denom = jnp.sum(p, axis=-1, keepdims=True)
        o = jnp.dot(p.astype(BF16), vx_ref[:, sl], preferred_element_type=F32) / denom
        heads.append(o.astype(BF16))
    o = jnp.concatenate(heads, axis=-1)
    o_ref[...] = x1 + jnp.dot(o, wxo_ref[...], preferred_element_type=F32)


def _out_xattn(x, ycn, ya_t, gatt, wout, gx, wxq, kx, vx, wxo, layer):
    s, d = x.shape
    tm = ROW_TILE
    n_mem = kx.shape[0]
    row = lambda width: pl.BlockSpec((tm, width), lambda i: (i, 0))
    return pl.pallas_call(
        _out_xattn_kernel,
        grid=(s // tm,),
        in_specs=[
            row(d), row(CONV_WIDTH), pl.BlockSpec((ATT_WIDTH, tm), lambda i: (0, i)),
            _const((1, ATT_WIDTH)), _layer_weight(wout, layer), _const((1, d)),
            _layer_weight(wxq, layer), _const((n_mem, d)), _const((n_mem, d)),
            _layer_weight(wxo, layer),
        ],
        out_specs=row(d),
        out_shape=jax.ShapeDtypeStruct((s, d), F32),
        compiler_params=pltpu.CompilerParams(
            dimension_semantics=("arbitrary",), vmem_limit_bytes=VMEM_LIMIT),
        name="out_xattn",
    )(x, ycn, ya_t, gatt, wout, gx, wxq, kx, vx, wxo)


def kernel(x, mem, g_ffn1, w_ffn1_gu, w_ffn1_down, g_mix, w_mix_in, w_conv, b_f, g_conv_out,
           g_att_out, w_mix_out, g_xattn, g_mem, w_xq, w_xkv, w_xo, g_ffn2, w_ffn2_gu,
           w_ffn2_down, g_final):
    b, s, d = x.shape
    assert b == 1 and w_mix_in.shape[2] == 3 * CONV_WIDTH + 3 * ATT_WIDTH + N_HEADS
    depth = g_ffn1.shape[0]
    xs = x.reshape(s, d)
    mem2 = mem.reshape(mem.shape[1], d)
    row = lambda a: a.reshape(1, -1).astype(F32)
    gfin = row(g_final)
    w1gu, w1d, w2gu, w2d, wxq, wxkv, wxo, wout = map(
        _to_bf16, (w_ffn1_gu, w_ffn1_down, w_ffn2_gu, w_ffn2_down, w_xq, w_xkv, w_xo, w_mix_out))
    win, wf = _cast_mix_in(w_mix_in)
    gate_lanes = (0, LANES - N_HEADS * SUBLANES)

    for l in range(depth):
        xs = _ffn(xs, row(g_ffn1[l]), w1gu, w1d, l, gfin, final_norm=False)

        bfr = jnp.pad(jnp.repeat(b_f[l], SUBLANES), gate_lanes).reshape(1, LANES).astype(F32)
        ycn, qt, k, vt, c, qn, kn = _mix(xs, row(g_mix[l]), win, l, wf, bfr,
                                         w_conv[l].astype(F32), row(g_conv_out[l]))

        ya_t = _attention(qt, k, vt, c, qn, kn)

        kx, vx = _memkv(mem2, row(g_mem[l]), wxkv, l)
        xs = _out_xattn(xs, ycn, ya_t, row(g_att_out[l]), wout, row(g_xattn[l]), wxq, kx, vx,
                        wxo, l)

        xs = _ffn(xs, row(g_ffn2[l]), w2gu, w2d, l, gfin, final_norm=(l == depth - 1))
    return xs.reshape(b, s, d)
```

```python
import functools

import numpy as np
import jax
import jax.numpy as jnp
from jax import lax
from jax.experimental import pallas as pl
from jax.experimental.pallas import tpu as pltpu

F32 = jnp.float32
BF16 = jnp.bfloat16

EPS = 1e-6
LANES = 128
SUBLANES = 8
MXU_DEPTH = 256
HEAD_DIM = 64
N_HEADS = 8
CONV_WIDTH = 512
ATT_WIDTH = N_HEADS * HEAD_DIM
CONV_K = 3
N_XHEADS = 4
N_SPLIT = 3
F32_EXP_UNDERFLOW = 104.0
BOUND_SLACK = 1.0
NORM_SLACK = 1.01
LOG2E = 1.4426950408889634
NO_SHIFT_BOUND = 40.0 * LOG2E
VMEM_LIMIT = 48 * 1024 * 1024
CAST_BLOCK_BYTES = 6 * 1024 * 1024

ROW_TILE = 512
FFN_TILE = 1024
ATT_TILE = ROW_TILE
ATT_HEADS = 2
ATT_QBLOCKS = 2
V_ROWS = 80


def _rms(x, g):
    ms = jnp.mean(x * x, axis=-1, keepdims=True)
    return x * lax.rsqrt(ms + EPS) * g


def _split3(x):
    hi = x.astype(BF16).astype(F32)
    r = x - hi
    mid = r.astype(BF16).astype(F32)
    lo = (r - mid).astype(BF16).astype(F32)
    return hi, mid, lo


def _layer_weight(w, layer):
    return pl.BlockSpec((None,) + w.shape[1:], lambda i: (layer, 0, 0), pipeline_mode=pl.Buffered(1))


def _const(shape):
    return pl.BlockSpec(shape, lambda i: (0, 0), pipeline_mode=pl.Buffered(1))


def _ffn_kernel(x_ref, g_ref, wgu_ref, wd_ref, gf_ref, o_ref, *, chunks, final_norm):
    d_ff = wd_ref.shape[0]
    x = x_ref[...]
    h = _rms(x, g_ref[...]).astype(BF16)
    acc = None
    for lo, hi in chunks:
        gate = jnp.dot(h, wgu_ref[:, lo:hi], preferred_element_type=F32)
        up = jnp.dot(h, wgu_ref[:, d_ff + lo:d_ff + hi], preferred_element_type=F32)
        act = (gate * (1.0 / (1.0 + jnp.exp(-gate))) * up).astype(BF16)
        part = jnp.dot(act, wd_ref[lo:hi, :], preferred_element_type=F32)
        acc = part if acc is None else acc + part
    y = x + 0.5 * acc
    if final_norm:
        y = _rms(y, gf_ref[...])
    o_ref[...] = y


def _ffn(x, g, w_gu, w_down, layer, g_final, *, final_norm):
    s, d = x.shape
    d_ff = w_down.shape[1]
    tm = FFN_TILE
    assert s % tm == 0 and d_ff % MXU_DEPTH == 0
    split = (d_ff // MXU_DEPTH + 1) // 2 * MXU_DEPTH
    chunks = ((0, split), (split, d_ff))
    return pl.pallas_call(
        functools.partial(_ffn_kernel, chunks=chunks, final_norm=final_norm),
        grid=(s // tm,),
        in_specs=[
            pl.BlockSpec((tm, d), lambda i: (i, 0)),
            _const((1, d)),
            _layer_weight(w_gu, layer),
            _layer_weight(w_down, layer),
            _const((1, d)),
        ],
        out_specs=pl.BlockSpec((tm, d), lambda i: (i, 0)),
        out_shape=jax.ShapeDtypeStruct((s, d), F32),
        compiler_params=pltpu.CompilerParams(
            dimension_semantics=("arbitrary",), vmem_limit_bytes=VMEM_LIMIT),
        name="ffn",
    )(x, g, w_gu, w_down, g_final)


def _cast_kernel(w_ref, o_ref):
    o_ref[...] = w_ref[...].astype(o_ref.dtype)


def _to_bf16(w):
    layers, rows, cols = w.shape
    tr = max(r for r in range(16, rows + 1, 16)
             if rows % r == 0 and r * cols * 4 <= CAST_BLOCK_BYTES)
    spec = pl.BlockSpec((None, tr, cols), lambda a, b: (a, b, 0))
    return pl.pallas_call(
        _cast_kernel,
        grid=(layers, rows // tr),
        in_specs=[spec],
        out_specs=spec,
        out_shape=jax.ShapeDtypeStruct(w.shape, BF16),
        compiler_params=pltpu.CompilerParams(
            dimension_semantics=("arbitrary", "arbitrary"), vmem_limit_bytes=VMEM_LIMIT),
        name="cast_bf16",
    )(w)


def _cast_mix_in_kernel(w_ref, spread_ref, o_ref, gate_ref):
    w = w_ref[...]
    o_ref[...] = w.astype(BF16)
    cols = w.shape[1]
    tail = w[:, cols - LANES:cols].astype(BF16)
    gate_ref[...] = jnp.dot(tail, spread_ref[...], preferred_element_type=F32).astype(BF16)


def _cast_mix_in(w):
    layers, rows, cols = w.shape
    tr = max(r for r in range(16, rows + 1, 16)
             if rows % r == 0 and r * cols * 4 <= CAST_BLOCK_BYTES)
    spread = np.zeros((LANES, LANES), np.float32)
    for h in range(N_HEADS):
        spread[LANES - N_HEADS + h, h * SUBLANES:(h + 1) * SUBLANES] = 1.0
    return pl.pallas_call(
        _cast_mix_in_kernel,
        grid=(layers, rows // tr),
        in_specs=[pl.BlockSpec((None, tr, cols), lambda a, b: (a, b, 0)),
                  pl.BlockSpec((LANES, LANES), lambda a, b: (0, 0))],
        out_specs=[pl.BlockSpec((None, tr, cols), lambda a, b: (a, b, 0)),
                   pl.BlockSpec((None, tr, LANES), lambda a, b: (a, b, 0))],
        out_shape=[jax.ShapeDtypeStruct(w.shape, BF16),
                   jax.ShapeDtypeStruct((layers, rows, LANES), BF16)],
        compiler_params=pltpu.CompilerParams(
            dimension_semantics=("arbitrary", "arbitrary"), vmem_limit_bytes=VMEM_LIMIT),
        name="cast_mix_in",
    )(w, jnp.asarray(spread, BF16))


def _mix_kernel(x_ref, g_ref, win_ref, wf_ref, bf_ref, wconv_ref, gconv_ref, hsum_ref,
                ycn_ref, qt_ref, k_ref, vt_ref, c_ref, qn_ref, kn_ref, u_scr, carry_scr):
    i = pl.program_id(0)
    tm = x_ref.shape[0]
    cw, aw = CONV_WIDTH, ATT_WIDTH

    @pl.when(i == 0)
    def _():
        u_scr[0:8, :] = jnp.zeros((8, cw), F32)
        carry_scr[...] = jnp.zeros_like(carry_scr)

    h = _rms(x_ref[...], g_ref[...]).astype(BF16)
    zf = jnp.dot(h, wf_ref[...], preferred_element_type=F32) + bf_ref[...]
    qk = jnp.dot(h, win_ref[:, 3 * cw:3 * cw + 2 * aw], preferred_element_type=F32)

    logf = jnp.minimum(zf, 0.0) - jnp.log1p(jnp.exp(-jnp.abs(zf)))
    c = logf
    row_id = lax.broadcasted_iota(jnp.int32, c.shape, 0)
    shift = 1
    while shift < tm:
        c = c + jnp.where(row_id >= shift, pltpu.roll(c, shift, 0), 0.0)
        shift *= 2
    c = c + carry_scr[7:8, :]
    carry_scr[...] = c[tm - 8:tm, :]
    c2 = c * LOG2E
    c_ref[...] = c2

    hi, mid, lo = _split3(c2)
    lane = lax.broadcasted_iota(jnp.int32, c.shape, 1)
    r = lane % SUBLANES
    used = lane < N_HEADS * SUBLANES
    one = jnp.ones_like(c)
    zero = jnp.zeros_like(c)
    q_bias = jnp.where(r == 0, hi, jnp.where(r == 1, mid, jnp.where(r == 2, lo, jnp.where(r < 6, one, zero))))
    k_bias = jnp.where(r < 3, one, jnp.where(r == 3, -hi, jnp.where(r == 4, -mid, jnp.where(r == 5, -lo, zero))))
    q_bias_t = jnp.transpose(jnp.where(used, q_bias, zero))
    k_bias = jnp.where(used, k_bias, zero)

    z_cv = jnp.dot(h, win_ref[:, cw:3 * cw], preferred_element_type=F32)
    v = jnp.dot(h, win_ref[:, 3 * cw + 2 * aw:3 * cw + 3 * aw], preferred_element_type=F32)
    z_b = jnp.dot(h, win_ref[:, 0:cw], preferred_element_type=F32)

    q_t = jnp.transpose(qk[:, 0:aw] * (HEAD_DIM ** -0.5 * LOG2E))
    k = qk[:, aw:2 * aw]
    pad_rows = jnp.zeros((LANES - HEAD_DIM - SUBLANES, tm), F32)
    head_lane = lane < HEAD_DIM
    qn = []
    for hh in range(N_HEADS):
        rows = slice(hh * HEAD_DIM, (hh + 1) * HEAD_DIM)
        blk = slice(hh * LANES, (hh + 1) * LANES)
        q_h = q_t[rows]
        qt_ref[blk, :] = jnp.concatenate(
            [q_h, q_bias_t[hh * SUBLANES:(hh + 1) * SUBLANES], pad_rows], axis=0).astype(BF16)

        pair = k[:, (hh // 2) * LANES:(hh // 2 + 1) * LANES]
        k_pos = pair if hh % 2 == 0 else pltpu.roll(pair, HEAD_DIM, 1)
        bias = pltpu.roll(k_bias, (HEAD_DIM - hh * SUBLANES) % LANES, 1)
        k_h = jnp.where(head_lane, k_pos, bias).astype(BF16)
        k_ref[:, blk] = k_h

        qf = q_h.astype(BF16).astype(F32)
        qn.append(jnp.max(jnp.sum(qf * qf, axis=0, keepdims=True), axis=1, keepdims=True))
    qn_ref[...] = jnp.concatenate([jnp.broadcast_to(n, (1, LANES)) for n in qn], axis=0)

    kf = k.astype(BF16).astype(F32)
    kn = jnp.dot((kf * kf).astype(BF16), hsum_ref[...], preferred_element_type=F32)
    kn_ref[...] = jnp.max(kn.reshape(tm // SUBLANES, SUBLANES, LANES), axis=0)

    u = z_cv[:, 0:cw] * z_cv[:, cw:2 * cw]
    u_scr[8:tm + 8, :] = u
    u1 = u_scr[7:tm + 7, :]
    u2 = u_scr[6:tm + 6, :]
    w = wconv_ref[...]
    conv = w[2:3, :] * u + w[1:2, :] * u1 + w[0:1, :] * u2
    u_scr[0:8, :] = u[tm - 8:tm, :]

    v_t = jnp.transpose(v.astype(BF16).astype(F32))
    ones_rows = jnp.where(lax.broadcasted_iota(jnp.int32, (SUBLANES, tm), 0) == 0, 1.0, 0.0)
    for hh in range(N_HEADS):
        vt_ref[hh * LANES:(hh + 1) * LANES, :] = jnp.concatenate(
            [v_t[hh * HEAD_DIM:(hh + 1) * HEAD_DIM], ones_rows, pad_rows], axis=0).astype(BF16)

    ycn_ref[...] = _rms(z_b * conv, gconv_ref[...]).astype(BF16)


def _mix(x, g, w_in, layer, wf, bfr, wconv, gconv):
    s, d = x.shape
    tm = ROW_TILE
    hp = N_HEADS * LANES
    nt = s // tm
    hsum =np.zeros((ATT_WIDTH, LANES), np.float32)
    hsum[np.arange(ATT_WIDTH), np.arange(ATT_WIDTH) // HEAD_DIM] = 1.0
    row = lambda width: pl.BlockSpec((tm, width), lambda i: (i, 0))
    col = lambda height: pl.BlockSpec((height, tm), lambda i: (0, i))
    return pl.pallas_call(
        _mix_kernel,
        grid=(nt,),
        in_specs=[
            row(d), _const((1, d)), _layer_weight(w_in, layer), _layer_weight(wf, layer),
            _const((1, LANES)), _const((CONV_K, CONV_WIDTH)), _const((1, CONV_WIDTH)),
            _const((ATT_WIDTH, LANES)),
        ],
        out_specs=[row(CONV_WIDTH), col(hp), row(hp), col(hp), row(LANES),
                   pl.BlockSpec((N_HEADS, LANES), lambda i: (i, 0)),
                   pl.BlockSpec((N_HEADS, LANES), lambda i: (i, 0))],
        out_shape=[
            jax.ShapeDtypeStruct((s, CONV_WIDTH), BF16),
            jax.ShapeDtypeStruct((hp, s), BF16),
            jax.ShapeDtypeStruct((s, hp), BF16),
            jax.ShapeDtypeStruct((hp, s), BF16),
            jax.ShapeDtypeStruct((s, LANES), F32),
            jax.ShapeDtypeStruct((nt * N_HEADS, LANES), F32),
            jax.ShapeDtypeStruct((nt * N_HEADS, LANES), F32),
        ],
        scratch_shapes=[pltpu.VMEM((tm + 8, CONV_WIDTH), F32), pltpu.VMEM((8, LANES), F32)],
        compiler_params=pltpu.CompilerParams(
            dimension_semantics=("arbitrary",), vmem_limit_bytes=VMEM_LIMIT),
        name="mix_in",
    )(x, g, w_in, wf, bfr, wconv, gconv, jnp.asarray(hsum, BF16))


def _att_kernel(cst_ref, cen_ref, ub_ref, qt_ref, k_ref, vt_ref, tri_ref, o_ref, m_scr, acc_scr):
    group = pl.program_id(0)
    ib = pl.program_id(1)
    t = tri_ref.shape[0]
    chains = [(hh, qs) for hh in range(ATT_HEADS) for qs in range(ATT_QBLOCKS)]

    def block_index(qs):
        return ib * ATT_QBLOCKS + qs

    def blocks_past_first_two(hh, qs):
        hd = group * ATT_HEADS + hh
        i = block_index(qs)
        floor = -((F32_EXP_UNDERFLOW + BOUND_SLACK) * LOG2E + 2.0 * ub_ref[hd, i])
        c_first = cst_ref[hd, i]

        def needed(j):
            return jnp.logical_and(j > 0, c_first - cen_ref[hd, jnp.maximum(j - 1, 0)] >= floor)

        j_lo = lax.while_loop(needed, lambda j: j - 1, jnp.maximum(i - 1, 0))
        return jnp.maximum(i - 1, 0) - j_lo

    n_rest = [blocks_past_first_two(hh, qs) for hh, qs in chains]

    def scores(chain, start, size):
        hh, qs = chain
        return jnp.dot(k_ref[pl.ds(start, size), hh * LANES:(hh + 1) * LANES],
                       qt_ref[hh * LANES:(hh + 1) * LANES, qs * t:(qs + 1) * t],
                       preferred_element_type=F32)

    def values(chain, start, size, p):
        hh, _ = chain
        return jnp.dot(vt_ref[hh * LANES:hh * LANES + V_ROWS, pl.ds(start, size)], p,
                       preferred_element_type=F32)

    def run(first, rest):
        def diagonal_only(c):
            first(c, scores(chains[c], 0, t) + tri_ref[...], 0, t)

        def previous_and_diagonal(c):
            start = pl.multiple_of((block_index(chains[c][1]) - 1) * t, t)
            s = scores(chains[c], start, 2 * t)
            first(c, jnp.concatenate([s[:t], s[t:] + tri_ref[...]], axis=0), start, 2 * t)

        @pl.when(ib == 0)
        def _():
            for c, (_, qs) in enumerate(chains):
                (diagonal_only if qs == 0 else previous_and_diagonal)(c)

        @pl.when(ib > 0)
        def _():
            for c in range(len(chains)):
                previous_and_diagonal(c)

        for c, (_, qs) in enumerate(chains):
            def body(n, carry, c=c, qs=qs):
                start = pl.multiple_of((block_index(qs) - 2 - n) * t, t)
                rest(c, scores(chains[c], start, t), start, t)
                return carry

            lax.fori_loop(0, n_rest[c], body, 0)

    unshifted = None
    for hh, qs in chains:
        ok = ub_ref[group * ATT_HEADS + hh, block_index(qs)] <= NO_SHIFT_BOUND
        unshifted = ok if unshifted is None else jnp.logical_and(unshifted, ok)

    @pl.when(unshifted)
    def _():
        def first(c, s, start, size):
            acc_scr[c] = values(chains[c], start, size, jnp.exp2(s).astype(BF16))

        def rest(c, s, start, size):
            acc_scr[c] += values(chains[c], start, size, jnp.exp2(s).astype(BF16))

        run(first, rest)

    @pl.when(jnp.logical_not(unshifted))
    def _():
        def first(c, s, start, size):
            m = jnp.max(s, axis=0, keepdims=True)
            m_scr[c] = m
            acc_scr[c] = values(chains[c], start, size, jnp.exp2(s - m).astype(BF16))

        def rest(c, s, start, size):
            m_old = m_scr[c]
            m_new = jnp.maximum(m_old, jnp.max(s, axis=0, keepdims=True))
            m_scr[c] = m_new
            acc_scr[c] = jnp.exp2(m_old - m_new) * acc_scr[c] + values(
                chains[c], start, size, jnp.exp2(s - m_new).astype(BF16))

        run(first, rest)

    for c, (hh, qs) in enumerate(chains):
        acc = acc_scr[c]
        o_ref[hh * HEAD_DIM:(hh + 1) * HEAD_DIM, qs * t:(qs + 1) * t] = (
            acc[0:HEAD_DIM, :] / acc[HEAD_DIM:HEAD_DIM + 1, :])


def _attention(qt, k, vt, c, qn, kn):
    s = k.shape[0]
    t = ATT_TILE
    nq = s // t
    c_heads = c[:, 0:N_HEADS * SUBLANES:SUBLANES]
    c_first = c_heads[0::t].T
    c_last = c_heads[t - 1::t].T
    qn = qn.reshape(nq, N_HEADS, LANES)[:, :, 0].T
    kn = jnp.max(kn, axis=0)[:N_HEADS] * NORM_SLACK
    qk_bound = jnp.sqrt(qn * kn[:, None])
    tri = jnp.asarray(np.where(np.arange(t)[:, None] <= np.arange(t)[None, :], 0.0, -np.inf), F32)
    grid_spec = pltpu.PrefetchScalarGridSpec(
        num_scalar_prefetch=3,
        grid=(N_HEADS // ATT_HEADS, nq // ATT_QBLOCKS),
        in_specs=[
            pl.BlockSpec((ATT_HEADS * LANES, ATT_QBLOCKS * t), lambda g, i, *_: (g, i)),
            pl.BlockSpec((s, ATT_HEADS * LANES), lambda g, i, *_: (0, g)),
            pl.BlockSpec((ATT_HEADS * LANES, s), lambda g, i, *_: (g, 0)),
            pl.BlockSpec((t, t), lambda g, i, *_: (0, 0)),
        ],
        out_specs=pl.BlockSpec((ATT_HEADS * HEAD_DIM, ATT_QBLOCKS * t), lambda g, i, *_: (g, i)),
        scratch_shapes=[pltpu.VMEM((ATT_HEADS * ATT_QBLOCKS, 1, t), F32),
                        pltpu.VMEM((ATT_HEADS * ATT_QBLOCKS, V_ROWS, t), F32)],
    )
    return pl.pallas_call(
        _att_kernel,
        grid_spec=grid_spec,
        out_shape=jax.ShapeDtypeStruct((ATT_WIDTH, s), F32),
        compiler_params=pltpu.CompilerParams(
            dimension_semantics=("arbitrary", "arbitrary"), vmem_limit_bytes=VMEM_LIMIT),
        name="fox_attention",
    )(c_first, c_last, qk_bound, qt, k, vt, tri)


def _memkv_kernel(mem_ref, g_ref, w_ref, k_ref, v_ref):
    d = mem_ref.shape[1]
    m = _rms(mem_ref[...], g_ref[...]).astype(BF16)
    kv = jnp.dot(m, w_ref[...], preferred_element_type=F32)
    k_ref[...] = kv[:, :d].astype(BF16)
    v_ref[...] = kv[:, d:].astype(BF16)


def _memkv(mem, g, w_kv, layer):
    n, d = mem.shape
    whole = lambda shape: pl.BlockSpec(shape, lambda i: (0, 0))
    return pl.pallas_call(
        _memkv_kernel,
        grid=(1,),
        in_specs=[whole((n, d)), whole((1, d)), _layer_weight(w_kv, layer)],
        out_specs=[whole((n, d)), whole((n, d))],
        out_shape=[jax.ShapeDtypeStruct((n, d), BF16), jax.ShapeDtypeStruct((n, d), BF16)],
        compiler_params=pltpu.CompilerParams(
            dimension_semantics=("arbitrary",), vmem_limit_bytes=VMEM_LIMIT),
        name="mem_kv",
    )(mem, g, w_kv)


def _out_xattn_kernel(x_ref, ycn_ref, yat_ref, gatt_ref, wout_ref, gx_ref, wxq_ref,
                      kx_ref, vx_ref, wxo_ref, o_ref):
    d = x_ref.shape[1]
    xd = d // N_XHEADS
    ya_t = yat_ref[...]
    ms = jnp.mean(ya_t * ya_t, axis=0, keepdims=True)
    ya = jnp.transpose(ya_t * lax.rsqrt(ms + EPS))
    y = jnp.concatenate([ycn_ref[...], (ya * gatt_ref[...]).astype(BF16)], axis=-1)
    x1 = x_ref[...] + jnp.dot(y, wout_ref[...], preferred_element_type=F32)

    hx = _rms(x1, gx_ref[...]).astype(BF16)
    q = (jnp.dot(hx, wxq_ref[...], preferred_element_type=F32) * (xd ** -0.5)).astype(BF16)
    nt = (((1,), (1,)), ((), ()))
    heads = []
    for hh in range(N_XHEADS):
        sl = slice(hh * xd, (hh + 1) * xd)
        s = lax.dot_general(q[:, sl], kx_ref[:, sl], nt, preferred_element_type=F32)
        p = jnp.exp(s - jnp.max(s, axis=-1, keepdims=True))
        denom = jnp.sum(p, axis=-1, keepdims=True)
        o = jnp.dot(p.astype(BF16), vx_ref[:, sl], preferred_element_type=F32) / denom
        heads.append(o.astype(BF16))
    o = jnp.concatenate(heads, axis=-1)
    o_ref[...] = x1 + jnp.dot(o, wxo_ref[...], preferred_element_type=F32)


def _out_xattn(x, ycn, ya_t, gatt, wout, gx, wxq, kx, vx, wxo, layer):
    s, d = x.shape
    tm = ROW_TILE
    n_mem = kx.shape[0]
    row = lambda width: pl.BlockSpec((tm, width), lambda i: (i, 0))
    return pl.pallas_call(
        _out_xattn_kernel,
        grid=(s // tm,),
        in_specs=[
            row(d), row(CONV_WIDTH), pl.BlockSpec((ATT_WIDTH, tm), lambda i: (0, i)),
            _const((1, ATT_WIDTH)), _layer_weight(wout, layer), _const((1, d)),
            _layer_weight(wxq, layer), _const((n_mem, d)), _const((n_mem, d)),
            _layer_weight(wxo, layer),
        ],
        out_specs=row(d),
        out_shape=jax.ShapeDtypeStruct((s, d), F32),
        compiler_params=pltpu.CompilerParams(
            dimension_semantics=("arbitrary",), vmem_limit_bytes=VMEM_LIMIT),
        name="out_xattn",
    )(x, ycn, ya_t, gatt, wout, gx, wxq, kx, vx, wxo)


def kernel(x, mem, g_ffn1, w_ffn1_gu, w_ffn1_down, g_mix, w_mix_in, w_conv, b_f, g_conv_out,
           g_att_out, w_mix_out, g_xattn, g_mem, w_xq, w_xkv, w_xo, g_ffn2, w_ffn2_gu,
           w_ffn2_down, g_final):
    b, s, d = x.shape
    assert b == 1 and w_mix_in.shape[2] == 3 * CONV_WIDTH + 3 * ATT_WIDTH + N_HEADS
    depth = g_ffn1.shape[0]
    xs = x.reshape(s, d)
    mem2 = mem.reshape(mem.shape[1], d)
    row = lambda a: a.reshape(1, -1).astype(F32)
    gfin = row(g_final)
    w1gu, w1d, w2gu, w2d, wxq, wxkv, wxo, wout = map(
        _to_bf16, (w_ffn1_gu, w_ffn1_down, w_ffn2_gu, w_ffn2_down, w_xq, w_xkv, w_xo, w_mix_out))
    win, wf = _cast_mix_in(w_mix_in)
    gate_lanes = (0, LANES - N_HEADS * SUBLANES)

    for l in range(depth):
        xs = _ffn(xs, row(g_ffn1[l]), w1gu, w1d, l, gfin, final_norm=False)

        bfr = jnp.pad(jnp.repeat(b_f[l], SUBLANES), gate_lanes).reshape(1, LANES).astype(F32)
        ycn, qt, k, vt, c, qn, kn = _mix(xs, row(g_mix[l]), win, l, wf, bfr,
                                         w_conv[l].astype(F32), row(g_conv_out[l]))

        ya_t = _attention(qt, k, vt, c, qn, kn)

        kx, vx = _memkv(mem2, row(g_mem[l]), wxkv, l)
        xs = _out_xattn(xs, ycn, ya_t, row(g_att_out[l]), wout, row(g_xattn[l]), wxq, kx, vx,
                        wxo, l)

        xs = _ffn(xs, row(g_ffn2[l]), w2gu, w2d, l, gfin, final_norm=(l == depth - 1))
    return xs.reshape(b, s, d)
```

```python
import functools

import numpy as np
import jax
import jax.numpy as jnp
from jax import lax
from jax.experimental import pallas as pl
from jax.experimental.pallas import tpu as pltpu

F32 = jnp.float32
BF16 = jnp.bfloat16

EPS = 1e-6
LANES = 128
SUBLANES = 8
BF16_ROWS = 16
MXU_DEPTH = 256
HEAD_DIM = 64
N_HEADS = 8
CONV_WIDTH = 512
ATT_WIDTH = N_HEADS * HEAD_DIM
CONV_K = 3
N_XHEADS = 4
N_SPLIT = 3
F32_EXP_UNDERFLOW = 104.0
BOUND_SLACK = 1.0
NORM_SLACK = 1.01
LOG2E = 1.4426950408889634
NO_SHIFT_BOUND = 40.0 * LOG2E
VMEM_LIMIT = 52 * 1024 * 1024
CAST_BLOCK_BYTES = 6 * 1024 * 1024

ROW_TILE = 512
FFN_TILE = 1024
FFN_SIDE_TILE = 512
ATT_TILE = ROW_TILE
ATT_HEADS = 2
ATT_QBLOCKS = 2
V_ROWS = 80


def _rms(x, g):
    ms = jnp.mean(x * x, axis=-1, keepdims=True)
    return x * lax.rsqrt(ms + EPS) * g


def _split3(x):
    hi = x.astype(BF16).astype(F32)
    r = x - hi
    mid = r.astype(BF16).astype(F32)
    lo = (r - mid).astype(BF16).astype(F32)
    return hi, mid, lo


def _layer_weight(w, layer):
    return pl.BlockSpec((None,) + w.shape[1:], lambda i: (layer, 0, 0), pipeline_mode=pl.Buffered(1))


def _const(shape):
    return pl.BlockSpec(shape, lambda i: (0, 0), pipeline_mode=pl.Buffered(1))


def _ffn_kernel(x_ref, g_ref, wgu_ref, wd_ref, gf_ref, *refs, chunks, final_norm, n_side):
    side_in, o_ref, side_out = refs[:n_side], refs[n_side], refs[n_side + 1:]
    for src, dst in zip(side_in, side_out):
        dst[...] = src[...].astype(BF16)

    d_ff = wd_ref.shape[0]
    x = x_ref[...]
    h = _rms(x, g_ref[...]).astype(BF16)
    acc = None
    for lo, hi in chunks:
        gate = jnp.dot(h, wgu_ref[:, lo:hi], preferred_element_type=F32)
        up = jnp.dot(h, wgu_ref[:, d_ff + lo:d_ff + hi], preferred_element_type=F32)
        act = (gate * (1.0 / (1.0 + jnp.exp(-gate))) * up).astype(BF16)
        part = jnp.dot(act, wd_ref[lo:hi, :], preferred_element_type=F32)
        acc = part if acc is None else acc + part
    y = x + 0.5 * acc
    if final_norm:
        y = _rms(y, gf_ref[...])
    o_ref[...] = y


def _ffn(x, g, w_gu, w_down, layer, g_final, *, final_norm, tm, side=()):
    s, d = x.shape
    d_ff = w_down.shape[1]
    n_steps = s // tm
    assert s % tm == 0 and d_ff % MXU_DEPTH == 0
    split = (d_ff // MXU_DEPTH + 1) // 2 * MXU_DEPTH
    chunks = ((0, split), (split, d_ff))
    side_args, side_in, side_out, side_shapes = [], [], [], []
    for w, first, count in side:
        _, rows, cols = w.shape
        slab = count * rows // n_steps
        assert slab * n_steps == count * rows and slab % BF16_ROWS == 0 and (first * rows) % slab == 0
        side_args.append(w.reshape(-1, cols))
        side_in.append(pl.BlockSpec((slab, cols), lambda i, off=first * rows // slab: (i + off, 0)))
        side_out.append(pl.BlockSpec((slab, cols), lambda i: (i, 0)))
        side_shapes.append(jax.ShapeDtypeStruct((count * rows, cols), BF16))
    out = pl.pallas_call(
        functools.partial(_ffn_kernel, chunks=chunks, final_norm=final_norm, n_side=len(side)),
        grid=(n_steps,),
        in_specs=[
            pl.BlockSpec((tm, d), lambda i: (i, 0)),
            _const((1, d)),
            _layer_weight(w_gu, layer),
            _layer_weight(w_down, layer),
            _const((1, d)),
        ] + side_in,
        out_specs=[pl.BlockSpec((tm, d), lambda i: (i, 0))] + side_out,
        out_shape=[jax.ShapeDtypeStruct((s, d), F32)] + side_shapes,
        compiler_params=pltpu.CompilerParams(
            dimension_semantics=("arbitrary",), vmem_limit_bytes=VMEM_LIMIT),
        name="ffn",
    )(x, g, w_gu, w_down, g_final, *side_args)
    converted = [o.reshape(count, w.shape[1], w.shape[2]) for o, (w, _, count) in zip(out[1:], side)]
    return (out[0], *converted)


def _cast_kernel(w_ref, o_ref):
    o_ref[...] = w_ref[...].astype(o_ref.dtype)


def _to_bf16(w, first=0, count=None):
    layers, rows, cols = w.shape
    count = layers - first if count is None else count
    tr = max(r for r in range(BF16_ROWS, rows + 1, BF16_ROWS)
             if rows % r == 0 and r * cols * 4 <= CAST_BLOCK_BYTES)
    return pl.pallas_call(
        _cast_kernel,
        grid=(count, rows // tr),
        in_specs=[pl.BlockSpec((None, tr, cols), lambda a, b: (a + first, b, 0))],
        out_specs=pl.BlockSpec((None, tr, cols), lambda a, b: (a, b, 0)),
        out_shape=jax.ShapeDtypeStruct((count, rows, cols), BF16),
        compiler_params=pltpu.CompilerParams(
            dimension_semantics=("arbitrary", "arbitrary"), vmem_limit_bytes=VMEM_LIMIT),
        name="cast_bf16",
    )(w)


def _cast_mix_in_kernel(w_ref, spread_ref, o_ref, gate_ref):
    w = w_ref[...]
    o_ref[...] = w.astype(BF16)
    cols = w.shape[1]
    tail = w[:, cols - LANES:cols].astype(BF16)
    gate_ref[...] = jnp.dot(tail, spread_ref[...], preferred_element_type=F32).astype(BF16)


def _cast_mix_in(w):
    layers, rows, cols = w.shape
    tr = max(r for r in range(BF16_ROWS, rows + 1, BF16_ROWS)
             if rows % r == 0 and r * cols * 4 <= CAST_BLOCK_BYTES)
    spread = np.zeros((LANES, LANES), np.float32)
    for h in range(N_HEADS):
        spread[LANES - N_HEADS + h, h * SUBLANES:(h + 1) * SUBLANES] = 1.0
    return pl.pallas_call(
        _cast_mix_in_kernel,
        grid=(layers, rows // tr),
        in_specs=[pl.BlockSpec((None, tr, cols), lambda a, b: (a, b, 0)),
                  pl.BlockSpec((LANES, LANES), lambda a, b: (0, 0))],
        out_specs=[pl.BlockSpec((None, tr, cols), lambda a, b: (a, b, 0)),
                   pl.BlockSpec((None, tr, LANES), lambda a, b: (a, b, 0))],
        out_shape=[jax.ShapeDtypeStruct(w.shape, BF16),
                   jax.ShapeDtypeStruct((layers, rows, LANES), BF16)],
        compiler_params=pltpu.CompilerParams(
            dimension_semantics=("arbitrary", "arbitrary"), vmem_limit_bytes=VMEM_LIMIT),
        name="cast_mix_in",
    )(w, jnp.asarray(spread, BF16))


def _mix_kernel(x_ref, g_ref, win_ref, wf_ref, bf_ref, wconv_ref, gconv_ref, hsum_ref,
                ycn_ref, qt_ref, k_ref, vt_ref, c_ref, qn_ref, kn_ref, u_scr, carry_scr):
    i = pl.program_id(0)
    tm = x_ref.shape[0]
    cw, aw = CONV_WIDTH, ATT_WIDTH

    @pl.when(i == 0)
    def _():
        u_scr[0:8, :] = jnp.zeros((8, cw), F32)
        carry_scr[...] = jnp.zeros_like(carry_scr)

    h = _rms(x_ref[...], g_ref[...]).astype(BF16)
    zf = jnp.dot(h, wf_ref[...], preferred_element_type=F32) + bf_ref[...]
    qk = jnp.dot(h, win_ref[:, 3 * cw:3 * cw + 2 * aw], preferred_element_type=F32)

    logf = jnp.minimum(zf, 0.0) - jnp.log1p(jnp.exp(-jnp.abs(zf)))
    c = logf
    row_id = lax.broadcasted_iota(jnp.int32, c.shape, 0)
    shift = 1
    while shift < tm:
        c = c + jnp.where(row_id >= shift, pltpu.roll(c, shift, 0), 0.0)
        shift *= 2
    c = c + carry_scr[7:8, :]
    carry_scr[...] = c[tm - 8:tm, :]
    c2 = c * LOG2E
    c_ref[...] = c2

    hi, mid, lo = _split3(c2)
    lane = lax.broadcasted_iota(jnp.int32, c.shape, 1)
    r = lane % SUBLANES
    used = lane < N_HEADS * SUBLANES
    one = jnp.ones_like(c)
    zero = jnp.zeros_like(c)
    q_bias = jnp.where(r == 0, hi, jnp.where(r == 1, mid, jnp.where(r == 2, lo, jnp.where(r < 6, one, zero))))
    k_bias = jnp.where(r < 3, one, jnp.where(r == 3, -hi, jnp.where(r == 4, -mid, jnp.where(r == 5, -lo, zero))))
    q_bias_t = jnp.transpose(jnp.where(used, q_bias, zero))
    k_bias = jnp.where(used, k_bias, zero)

    z_cv = jnp.dot(h, win_ref[:, cw:3 * cw], preferred_element_type=F32)
    v = jnp.dot(h, win_ref[:, 3 * cw + 2 * aw:3 * cw + 3 * aw], preferred_element_type=F32)
    z_b = jnp.dot(h, win_ref[:, 0:cw], preferred_element_type=F32)

    q_t = jnp.transpose(qk[:, 0:aw] * (HEAD_DIM ** -0.5 * LOG2E))
    k = qk[:, aw:2 * aw]
    pad_rows = jnp.zeros((LANES - HEAD_DIM - SUBLANES, tm), F32)
    head_lane = lane < HEAD_DIM
    qn = []
    for hh in range(N_HEADS):
        rows = slice(hh * HEAD_DIM, (hh + 1) * HEAD_DIM)
        blk = slice(hh * LANES, (hh + 1) * LANES)
        q_h = q_t[rows]
        qt_ref[blk, :] = jnp.concatenate(
            [q_h, q_bias_t[hh * SUBLANES:(hh + 1) * SUBLANES], pad_rows], axis=0).astype(BF16)

        pair = k[:, (hh // 2) * LANES:(hh // 2 + 1) * LANES]
        k_pos = pair if hh % 2 == 0 else pltpu.roll(pair, HEAD_DIM, 1)
        bias = pltpu.roll(k_bias, (HEAD_DIM - hh * SUBLANES) % LANES, 1)
        k_h = jnp.where(head_lane, k_pos, bias).astype(BF16)
        k_ref[:, blk] = k_h

        qf = q_h.astype(BF16).astype(F32)
        qn.append(jnp.max(jnp.sum(qf * qf, axis=0, keepdims=True), axis=1, keepdims=True))
    qn_ref[...] = jnp.concatenate([jnp.broadcast_to(n, (1, LANES)) for n in qn], axis=0)

    kf = k.astype(BF16).astype(F32)
    kn = jnp.dot((kf * kf).astype(BF16), hsum_ref[...], preferred_element_type=F32)
    kn_ref[...] = jnp.max(kn.reshape(tm // SUBLANES, SUBLANES, LANES), axis=0)

    u = z_cv[:, 0:cw] * z_cv[:, cw:2 * cw]
    u_scr[8:tm + 8, :] = u
    u1 = u_scr[7:tm + 7, :]
    u2 = u_scr[6:tm + 6, :]
    w = wconv_ref[...]
    conv = w[2:3, :] * u + w[1:2, :] * u1 + w[0:1, :] * u2
    u_scr[0:8, :] = u[tm - 8:tm, :]

    v_t = jnp.transpose(v.astype(BF16).astype(F32))
    ones_rows = jnp.where(lax.broadcasted_iota(jnp.int32, (SUBLANES, tm), 0) == 0, 1.0, 0.0)
    for hh in range(N_HEADS):
        vt_ref[hh * LANES:(hh + 1) * LANES, :] = jnp.concatenate(
            [v_t[hh * HEAD_DIM:(hh + 1) * HEAD_DIM], ones_rows, pad_rows], axis=0).astype(BF16)

    ycn_ref[...] = _rms(z_b * conv, gconv_ref[...]).astype(BF16)


def _mix(x, g, w_in, layer, wf, bfr, wconv, gconv):
    s, d = x.shape
    tm = ROW_TILE
    hp = N_HEADS * LANES
    nt = s // tm
    hsum =np.zeros((ATT_WIDTH, LANES), np.float32)
    hsum[np.arange(ATT_WIDTH), np.arange(ATT_WIDTH) // HEAD_DIM] = 1.0
    row = lambda width: pl.BlockSpec((tm, width), lambda i: (i, 0))
    col = lambda height: pl.BlockSpec((height, tm), lambda i: (0, i))
    return pl.pallas_call(
        _mix_kernel,
        grid=(nt,),
        in_specs=[
            row(d), _const((1, d)), _layer_weight(w_in, layer), _layer_weight(wf, layer),
            _const((1, LANES)), _const((CONV_K, CONV_WIDTH)), _const((1, CONV_WIDTH)),
            _const((ATT_WIDTH, LANES)),
        ],
        out_specs=[row(CONV_WIDTH), col(hp), row(hp), col(hp), row(LANES),
                   pl.BlockSpec((N_HEADS, LANES), lambda i: (i, 0)),
                   pl.BlockSpec((N_HEADS, LANES), lambda i: (i, 0))],
        out_shape=[
            jax.ShapeDtypeStruct((s, CONV_WIDTH), BF16),
            jax.ShapeDtypeStruct((hp, s), BF16),
            jax.ShapeDtypeStruct((s, hp), BF16),
            jax.ShapeDtypeStruct((hp, s), BF16),
            jax.ShapeDtypeStruct((s, LANES), F32),
            jax.ShapeDtypeStruct((nt * N_HEADS, LANES), F32),
            jax.ShapeDtypeStruct((nt * N_HEADS, LANES), F32),
        ],
        scratch_shapes=[pltpu.VMEM((tm + 8, CONV_WIDTH), F32), pltpu.VMEM((8, LANES), F32)],
        compiler_params=pltpu.CompilerParams(
            dimension_semantics=("arbitrary",), vmem_limit_bytes=VMEM_LIMIT),
        name="mix_in",
    )(x, g, w_in, wf, bfr, wconv, gconv, jnp.asarray(hsum, BF16))


def _att_kernel(cst_ref, cen_ref, ub_ref, qt_ref, k_ref, vt_ref, tri_ref, o_ref, m_scr, acc_scr):
    group = pl.program_id(0)
    ib = pl.program_id(1)
    t = tri_ref.shape[0]
    chains = [(hh, qs) for hh in range(ATT_HEADS) for qs in range(ATT_QBLOCKS)]

    def block_index(qs):
        return ib * ATT_QBLOCKS + qs

    def blocks_past_first_two(hh, qs):
        hd = group * ATT_HEADS + hh
        i = block_index(qs)
        floor = -((F32_EXP_UNDERFLOW + BOUND_SLACK) * LOG2E + 2.0 * ub_ref[hd, i])
        c_first = cst_ref[hd, i]

        def needed(j):
            return jnp.logical_and(j > 0, c_first - cen_ref[hd, jnp.maximum(j - 1, 0)] >= floor)

        j_lo = lax.while_loop(needed, lambda j: j - 1, jnp.maximum(i - 1, 0))
        return jnp.maximum(i - 1, 0) - j_lo

    n_rest = [blocks_past_first_two(hh, qs) for hh, qs in chains]

    def scores(chain, start, size):
        hh, qs = chain
        return jnp.dot(k_ref[pl.ds(start, size), hh * LANES:(hh + 1) * LANES],
                       qt_ref[hh * LANES:(hh + 1) * LANES, qs * t:(qs + 1) * t],
                       preferred_element_type=F32)

    def values(chain, start, size, p):
        hh, _ = chain
        return jnp.dot(vt_ref[hh * LANES:hh * LANES + V_ROWS, pl.ds(start, size)], p,
                       preferred_element_type=F32)

    def run(first, rest):
        def diagonal_only(c):
            first(c, scores(chains[c], 0, t) + tri_ref[...], 0, t)

        def previous_and_diagonal(c):
            start = pl.multiple_of((block_index(chains[c][1]) - 1) * t, t)
            s = scores(chains[c], start, 2 * t)
            first(c, jnp.concatenate([s[:t], s[t:] + tri_ref[...]], axis=0), start, 2 * t)

        @pl.when(ib == 0)
        def _():
            for c, (_, qs) in enumerate(chains):
                (diagonal_only if qs == 0 else previous_and_diagonal)(c)

        @pl.when(ib > 0)
        def _():
            for c in range(len(chains)):
                previous_and_diagonal(c)

        for c, (_, qs) in enumerate(chains):
            def body(n, carry, c=c, qs=qs):
                start = pl.multiple_of((block_index(qs) - 2 - n) * t, t)
                rest(c, scores(chains[c], start, t), start, t)
                return carry

            lax.fori_loop(0, n_rest[c], body, 0)

    unshifted = None
    for hh, qs in chains:
        ok = ub_ref[group * ATT_HEADS + hh, block_index(qs)] <= NO_SHIFT_BOUND
        unshifted = ok if unshifted is None else jnp.logical_and(unshifted, ok)

    @pl.when(unshifted)
    def _():
        def first(c, s, start, size):
            acc_scr[c] = values(chains[c], start, size, jnp.exp2(s).astype(BF16))

        def rest(c, s, start, size):
            acc_scr[c] += values(chains[c], start, size, jnp.exp2(s).astype(BF16))

        run(first, rest)

    @pl.when(jnp.logical_not(unshifted))
    def _():
        def first(c, s, start, size):
            m = jnp.max(s, axis=0, keepdims=True)
            m_scr[c] = m
            acc_scr[c] = values(chains[c], start, size, jnp.exp2(s - m).astype(BF16))

        def rest(c, s, start, size):
            m_old = m_scr[c]
            m_new = jnp.maximum(m_old, jnp.max(s, axis=0, keepdims=True))
            m_scr[c] = m_new
            acc_scr[c] = jnp.exp2(m_old - m_new) * acc_scr[c] + values(
                chains[c], start, size, jnp.exp2(s - m_new).astype(BF16))

        run(first, rest)

    for c, (hh, qs) in enumerate(chains):
        acc = acc_scr[c]
        o_ref[hh * HEAD_DIM:(hh + 1) * HEAD_DIM, qs * t:(qs + 1) * t] = (
            acc[0:HEAD_DIM, :] / acc[HEAD_DIM:HEAD_DIM + 1, :])


def _attention(qt, k, vt, c, qn, kn):
    s = k.shape[0]
    t = ATT_TILE
    nq = s // t
    c_heads = c[:, 0:N_HEADS * SUBLANES:SUBLANES]
    c_first = c_heads[0::t].T
    c_last = c_heads[t - 1::t].T
    qn = qn.reshape(nq, N_HEADS, LANES)[:, :, 0].T
    kn = jnp.max(kn, axis=0)[:N_HEADS] * NORM_SLACK
    qk_bound = jnp.sqrt(qn * kn[:, None])
    tri = jnp.asarray(np.where(np.arange(t)[:, None] <= np.arange(t)[None, :], 0.0, -np.inf), F32)
    grid_spec = pltpu.PrefetchScalarGridSpec(
        num_scalar_prefetch=3,
        grid=(N_HEADS // ATT_HEADS, nq // ATT_QBLOCKS),
        in_specs=[
            pl.BlockSpec((ATT_HEADS * LANES, ATT_QBLOCKS * t), lambda g, i, *_: (g, i)),
            pl.BlockSpec((s, ATT_HEADS * LANES), lambda g, i, *_: (0, g)),
            pl.BlockSpec((ATT_HEADS * LANES, s), lambda g, i, *_: (g, 0)),
            pl.BlockSpec((t, t), lambda g, i, *_: (0, 0)),
        ],
        out_specs=pl.BlockSpec((ATT_HEADS * HEAD_DIM, ATT_QBLOCKS * t), lambda g, i, *_: (g, i)),
        scratch_shapes=[pltpu.VMEM((ATT_HEADS * ATT_QBLOCKS, 1, t), F32),
                        pltpu.VMEM((ATT_HEADS * ATT_QBLOCKS, V_ROWS, t), F32)],
    )
    return pl.pallas_call(
        _att_kernel,
        grid_spec=grid_spec,
        out_shape=jax.ShapeDtypeStruct((ATT_WIDTH, s), F32),
        compiler_params=pltpu.CompilerParams(
            dimension_semantics=("arbitrary", "arbitrary"), vmem_limit_bytes=VMEM_LIMIT),
        name="fox_attention",
    )(c_first, c_last, qk_bound, qt, k, vt, tri)


def _memkv_kernel(mem_ref, g_ref, w_ref, k_ref, v_ref):
    d = mem_ref.shape[1]
    m = _rms(mem_ref[...], g_ref[...]).astype(BF16)
    kv = jnp.dot(m, w_ref[...], preferred_element_type=F32)
    k_ref[...] = kv[:, :d].astype(BF16)
    v_ref[...] = kv[:, d:].astype(BF16)


def _memkv(mem, g, w_kv, layer):
    n, d = mem.shape
    whole = lambda shape: pl.BlockSpec(shape, lambda i: (0, 0))
    return pl.pallas_call(
        _memkv_kernel,
        grid=(1,),
        in_specs=[whole((n, d)), whole((1, d)), _layer_weight(w_kv, layer)],
        out_specs=[whole((n, d)), whole((n, d))],
        out_shape=[jax.ShapeDtypeStruct((n, d), BF16), jax.ShapeDtypeStruct((n, d), BF16)],
        compiler_params=pltpu.CompilerParams(
            dimension_semantics=("arbitrary",), vmem_limit_bytes=VMEM_LIMIT),
        name="mem_kv",
    )(mem, g, w_kv)


def _out_xattn_kernel(x_ref, ycn_ref, yat_ref, gatt_ref, wout_ref, gx_ref, wxq_ref,
                      kx_ref, vx_ref, wxo_ref, o_ref):
    d = x_ref.shape[1]
    xd = d // N_XHEADS
    ya_t = yat_ref[...]
    ms = jnp.mean(ya_t * ya_t, axis=0, keepdims=True)
    ya = jnp.transpose(ya_t * lax.rsqrt(ms + EPS))
    y = jnp.concatenate([ycn_ref[...], (ya * gatt_ref[...]).astype(BF16)], axis=-1)
    x1 = x_ref[...] + jnp.dot(y, wout_ref[...], preferred_element_type=F32)

    hx = _rms(x1, gx_ref[...]).astype(BF16)
    q = (jnp.dot(hx, wxq_ref[...], preferred_element_type=F32) * (xd ** -0.5)).astype(BF16)
    nt = (((1,), (1,)), ((), ()))
    heads = []
    for hh in range(N_XHEADS):
        sl = slice(hh * xd, (hh + 1) * xd)
        s = lax.dot_general(q[:, sl], kx_ref[:, sl], nt, preferred_element_type=F32)
        p = jnp.exp(s - jnp.max(s, axis=-1, keepdims=True))
        denom = jnp.sum(p, axis=-1, keepdims=True)
        o = jnp.dot(p.astype(BF16), vx_ref[:, sl], preferred_element_type=F32) / denom
        heads.append(o.astype(BF16))
    o = jnp.concatenate(heads, axis=-1)
    o_ref[...] = x1 + jnp.dot(o, wxo_ref[...], preferred_element_type=F32)


def _out_xattn(x, ycn, ya_t, gatt, wout, gx, wxq, kx, vx, wxo, layer):
    s, d = x.shape
    tm = ROW_TILE
    n_mem = kx.shape[0]
    row = lambda width: pl.BlockSpec((tm, width), lambda i: (i, 0))
    return pl.pallas_call(
        _out_xattn_kernel,
        grid=(s // tm,),
        in_specs=[
            row(d), row(CONV_WIDTH), pl.BlockSpec((ATT_WIDTH, tm), lambda i: (0, i)),
            _const((1, ATT_WIDTH)), _layer_weight(wout, layer), _const((1, d)),
            _layer_weight(wxq, layer), _const((n_mem, d)), _const((n_mem, d)),
            _layer_weight(wxo, layer),
        ],
        out_specs=row(d),
        out_shape=jax.ShapeDtypeStruct((s, d), F32),
        compiler_params=pltpu.CompilerParams(
            dimension_semantics=("arbitrary",), vmem_limit_bytes=VMEM_LIMIT),
        name="out_xattn",
    )(x, ycn, ya_t, gatt, wout, gx, wxq, kx, vx, wxo)


def kernel(x, mem, g_ffn1, w_ffn1_gu, w_ffn1_down, g_mix, w_mix_in, w_conv, b_f, g_conv_out,
           g_att_out, w_mix_out, g_xattn, g_mem, w_xq, w_xkv, w_xo, g_ffn2, w_ffn2_gu,
           w_ffn2_down, g_final):
    b, s, d = x.shape
    assert b == 1 and w_mix_in.shape[2] == 3 * CONV_WIDTH + 3 * ATT_WIDTH + N_HEADS
    depth = g_ffn1.shape[0]
    xs = x.reshape(s, d)
    mem2 = mem.reshape(mem.shape[1], d)
    row = lambda a: a.reshape(1, -1).astype(F32)
    gfin = row(g_final)
    w1gu, w1d = _to_bf16(w_ffn1_gu, 0, 1), _to_bf16(w_ffn1_down, 0, 1)
    wxq, wxkv, wxo, wout = map(_to_bf16, (w_xq, w_xkv, w_xo, w_mix_out))
    win, wf = _cast_mix_in(w_mix_in)
    gate_lanes = (0, LANES - N_HEADS * SUBLANES)
    w1_layer0 = 0

    for l in range(depth):
        if l == 0:
            xs, w2gu, w2d = _ffn(xs, row(g_ffn1[l]), w1gu, w1d, 0, gfin, final_norm=False,
                                 tm=FFN_SIDE_TILE,
                                 side=((w_ffn2_gu, 0, depth), (w_ffn2_down, 0, depth)))
        else:
            (xs,) = _ffn(xs, row(g_ffn1[l]), w1gu, w1d, l - w1_layer0, gfin, final_norm=False,
                         tm=FFN_TILE)

        bfr = jnp.pad(jnp.repeat(b_f[l], SUBLANES), gate_lanes).reshape(1, LANES).astype(F32)
        ycn, qt, k, vt, c, qn, kn = _mix(xs, row(g_mix[l]), win, l, wf, bfr,
                                         w_conv[l].astype(F32), row(g_conv_out[l]))

        ya_t = _attention(qt, k, vt, c, qn, kn)

        kx, vx = _memkv(mem2, row(g_mem[l]), wxkv, l)
        xs = _out_xattn(xs, ycn, ya_t, row(g_att_out[l]), wout, row(g_xattn[l]), wxq, kx, vx,
                        wxo, l)

        last = l == depth - 1
        if l == 0 and not last:
            xs, w1gu, w1d = _ffn(xs, row(g_ffn2[l]), w2gu, w2d, l, gfin, final_norm=last,
                                 tm=FFN_TILE,
                                 side=((w_ffn1_gu, 1, depth - 1), (w_ffn1_down, 1, depth - 1)))
            w1_layer0 = 1
        else:
            (xs,) = _ffn(xs, row(g_ffn2[l]), w2gu, w2d, l, gfin, final_norm=last, tm=FFN_TILE)
    return xs.reshape(b, s, d)
```

```python
import functools

import numpy as np
import jax
import jax.numpy as jnp
from jax import lax
from jax.experimental import pallas as pl
from jax.experimental.pallas import tpu as pltpu

F32 = jnp.float32
BF16 = jnp.bfloat16

EPS = 1e-6
LANES = 128
SUBLANES = 8
BF16_ROWS = 16
MXU_DEPTH = 256
HEAD_DIM = 64
N_HEADS = 8
CONV_WIDTH = 512
ATT_WIDTH = N_HEADS * HEAD_DIM
CONV_K = 3
N_XHEADS = 4
N_SPLIT = 3
F32_EXP_UNDERFLOW = 104.0
BOUND_SLACK = 1.0
NORM_SLACK = 1.01
LOG2E = 1.4426950408889634
NO_SHIFT_BOUND = 40.0 * LOG2E
VMEM_LIMIT = 52 * 1024 * 1024
CAST_BLOCK_BYTES = 6 * 1024 * 1024

ROW_TILE = 512
FFN_TILE = 1024
FFN_SIDE_TILE = 512
ATT_TILE = ROW_TILE
ATT_HEADS = 2
ATT_QBLOCKS = 2
V_ROWS = 80


def _rms(x, g):
    ms = jnp.mean(x * x, axis=-1, keepdims=True)
    return x * lax.rsqrt(ms + EPS) * g


def _split3(x):
    hi = x.astype(BF16).astype(F32)
    r = x - hi
    mid = r.astype(BF16).astype(F32)
    lo = (r - mid).astype(BF16).astype(F32)
    return hi, mid, lo


def _layer_weight(w, layer):
    return pl.BlockSpec((None,) + w.shape[1:], lambda i: (layer, 0, 0), pipeline_mode=pl.Buffered(1))


def _const(shape):
    return pl.BlockSpec(shape, lambda i: (0, 0), pipeline_mode=pl.Buffered(1))


def _ffn_kernel(x_ref, g_ref, wgu_ref, wd_ref, gf_ref, *refs, chunks, final_norm, n_side):
    side_in, o_ref, side_out = refs[:n_side], refs[n_side], refs[n_side + 1:]
    for src, dst in zip(side_in, side_out):
        dst[...] = src[...].astype(BF16)

    d_ff = wd_ref.shape[0]
    x = x_ref[...]
    h = _rms(x, g_ref[...]).astype(BF16)
    acc = None
    for lo, hi in chunks:
        gate = jnp.dot(h, wgu_ref[:, lo:hi], preferred_element_type=F32)
        up = jnp.dot(h, wgu_ref[:, d_ff + lo:d_ff + hi], preferred_element_type=F32)
        act = (gate * (1.0 / (1.0 + jnp.exp(-gate))) * up).astype(BF16)
        part = jnp.dot(act, wd_ref[lo:hi, :], preferred_element_type=F32)
        acc = part if acc is None else acc + part
    y = x + 0.5 * acc
    if final_norm:
        y = _rms(y, gf_ref[...])
    o_ref[...] = y


def _ffn(x, g, w_gu, w_down, layer, g_final, *, final_norm, tm, side=()):
    s, d = x.shape
    d_ff = w_down.shape[1]
    n_steps = s // tm
    assert s % tm == 0 and d_ff % MXU_DEPTH == 0
    split = (d_ff // MXU_DEPTH + 1) // 2 * MXU_DEPTH
    chunks = ((0, split), (split, d_ff))
    side_args, side_in, side_out, side_shapes = [], [], [], []
    for w, first, count in side:
        _, rows, cols = w.shape
        slab = count * rows // n_steps
        assert slab * n_steps == count * rows and slab % BF16_ROWS == 0 and (first * rows) % slab == 0
        side_args.append(w.reshape(-1, cols))
        side_in.append(pl.BlockSpec((slab, cols), lambda i, off=first * rows // slab: (i + off, 0)))
        side_out.append(pl.BlockSpec((slab, cols), lambda i: (i, 0)))
        side_shapes.append(jax.ShapeDtypeStruct((count * rows, cols), BF16))
    out = pl.pallas_call(
        functools.partial(_ffn_kernel, chunks=chunks, final_norm=final_norm, n_side=len(side)),
        grid=(n_steps,),
        in_specs=[
            pl.BlockSpec((tm, d), lambda i: (i, 0)),
            _const((1, d)),
            _layer_weight(w_gu, layer),
            _layer_weight(w_down, layer),
            _const((1, d)),
        ] + side_in,
        out_specs=[pl.BlockSpec((tm, d), lambda i: (i, 0))] + side_out,
        out_shape=[jax.ShapeDtypeStruct((s, d), F32)] + side_shapes,
        compiler_params=pltpu.CompilerParams(
            dimension_semantics=("arbitrary",), vmem_limit_bytes=VMEM_LIMIT),
        name="ffn",
    )(x, g, w_gu, w_down, g_final, *side_args)
    converted = [o.reshape(count, w.shape[1], w.shape[2]) for o, (w, _, count) in zip(out[1:], side)]
    return (out[0], *converted)


def _cast_kernel(w_ref, o_ref):
    o_ref[...] = w_ref[...].astype(o_ref.dtype)


def _to_bf16(w, first=0, count=None):
    layers, rows, cols = w.shape
    count = layers - first if count is None else count
    tr = max(r for r in range(BF16_ROWS, rows + 1, BF16_ROWS)
             if rows % r == 0 and r * cols * 4 <= CAST_BLOCK_BYTES)
    return pl.pallas_call(
        _cast_kernel,
        grid=(count, rows // tr),
        in_specs=[pl.BlockSpec((None, tr, cols), lambda a, b: (a + first, b, 0))],
        out_specs=pl.BlockSpec((None, tr, cols), lambda a, b: (a, b, 0)),
        out_shape=jax.ShapeDtypeStruct((count, rows, cols), BF16),
        compiler_params=pltpu.CompilerParams(
            dimension_semantics=("arbitrary", "arbitrary"), vmem_limit_bytes=VMEM_LIMIT),
        name="cast_bf16",
    )(w)


def _cast_mix_in_kernel(w_ref, spread_ref, o_ref, gate_ref):
    w = w_ref[...]
    o_ref[...] = w.astype(BF16)
    cols = w.shape[1]
    tail = w[:, cols - LANES:cols].astype(BF16)
    gate_ref[...] = jnp.dot(tail, spread_ref[...], preferred_element_type=F32).astype(BF16)


def _cast_mix_in(w):
    layers, rows, cols = w.shape
    tr = max(r for r in range(BF16_ROWS, rows + 1, BF16_ROWS)
             if rows % r == 0 and r * cols * 4 <= CAST_BLOCK_BYTES)
    spread = np.zeros((LANES, LANES), np.float32)
    for h in range(N_HEADS):
        spread[LANES - N_HEADS + h, h * SUBLANES:(h + 1) * SUBLANES] = 1.0
    return pl.pallas_call(
        _cast_mix_in_kernel,
        grid=(layers, rows // tr),
        in_specs=[pl.BlockSpec((None, tr, cols), lambda a, b: (a, b, 0)),
                  pl.BlockSpec((LANES, LANES), lambda a, b: (0, 0))],
        out_specs=[pl.BlockSpec((None, tr, cols), lambda a, b: (a, b, 0)),
                   pl.BlockSpec((None, tr, LANES), lambda a, b: (a, b, 0))],
        out_shape=[jax.ShapeDtypeStruct(w.shape, BF16),
                   jax.ShapeDtypeStruct((layers, rows, LANES), BF16)],
        compiler_params=pltpu.CompilerParams(
            dimension_semantics=("arbitrary", "arbitrary"), vmem_limit_bytes=VMEM_LIMIT),
        name="cast_mix_in",
    )(w, jnp.asarray(spread, BF16))


def _mix_kernel(x_ref, g_ref, win_ref, wf_ref, bf_ref, wconv_ref, gconv_ref, hsum_ref,
                ycn_ref, qt_ref, k_ref, vt_ref, c_ref, qn_ref, kn_ref, u_scr, carry_scr):
    i = pl.program_id(0)
    tm = x_ref.shape[0]
    cw, aw = CONV_WIDTH, ATT_WIDTH

    @pl.when(i == 0)
    def _():
        u_scr[0:8, :] = jnp.zeros((8, cw), F32)
        carry_scr[...] = jnp.zeros_like(carry_scr)

    h = _rms(x_ref[...], g_ref[...]).astype(BF16)
    zf = jnp.dot(h, wf_ref[...], preferred_element_type=F32) + bf_ref[...]
    qk = jnp.dot(h, win_ref[:, 3 * cw:3 * cw + 2 * aw], preferred_element_type=F32)

    logf = jnp.minimum(zf, 0.0) - jnp.log1p(jnp.exp(-jnp.abs(zf)))
    c = logf
    row_id = lax.broadcasted_iota(jnp.int32, c.shape, 0)
    shift = 1
    while shift < tm:
        c = c + jnp.where(row_id >= shift, pltpu.roll(c, shift, 0), 0.0)
        shift *= 2
    c = c + carry_scr[7:8, :]
    carry_scr[...] = c[tm - 8:tm, :]
    c2 = c * LOG2E
    c_ref[...] = c2

    hi, mid, lo = _split3(c2)
    lane = lax.broadcasted_iota(jnp.int32, c.shape, 1)
    r = lane % SUBLANES
    used = lane < N_HEADS * SUBLANES
    one = jnp.ones_like(c)
    zero = jnp.zeros_like(c)
    q_bias = jnp.where(r == 0, hi, jnp.where(r == 1, mid, jnp.where(r == 2, lo, jnp.where(r < 6, one, zero))))
    k_bias = jnp.where(r < 3, one, jnp.where(r == 3, -hi, jnp.where(r == 4, -mid, jnp.where(r == 5, -lo, zero))))
    q_bias_t = jnp.transpose(jnp.where(used, q_bias, zero))
    k_bias = jnp.where(used, k_bias, zero)

    z_cv = jnp.dot(h, win_ref[:, cw:3 * cw], preferred_element_type=F32)
    v = jnp.dot(h, win_ref[:, 3 * cw + 2 * aw:3 * cw + 3 * aw], preferred_element_type=F32)
    z_b = jnp.dot(h, win_ref[:, 0:cw], preferred_element_type=F32)

    q_t = jnp.transpose(qk[:, 0:aw] * (HEAD_DIM ** -0.5 * LOG2E))
    k = qk[:, aw:2 * aw]
    pad_rows = jnp.zeros((LANES - HEAD_DIM - SUBLANES, tm), F32)
    head_lane = lane < HEAD_DIM
    qn = []
    for hh in range(N_HEADS):
        rows = slice(hh * HEAD_DIM, (hh + 1) * HEAD_DIM)
        blk = slice(hh * LANES, (hh + 1) * LANES)
        q_h = q_t[rows]
        qt_ref[blk, :] = jnp.concatenate(
            [q_h, q_bias_t[hh * SUBLANES:(hh + 1) * SUBLANES], pad_rows], axis=0).astype(BF16)

        pair = k[:, (hh // 2) * LANES:(hh // 2 + 1) * LANES]
        k_pos = pair if hh % 2 == 0 else pltpu.roll(pair, HEAD_DIM, 1)
        bias = pltpu.roll(k_bias, (HEAD_DIM - hh * SUBLANES) % LANES, 1)
        k_h = jnp.where(head_lane, k_pos, bias).astype(BF16)
        k_ref[:, blk] = k_h

        qf = q_h.astype(BF16).astype(F32)
        qn.append(jnp.max(jnp.sum(qf * qf, axis=0, keepdims=True), axis=1, keepdims=True))
    qn_ref[...] = jnp.concatenate([jnp.broadcast_to(n, (1, LANES)) for n in qn], axis=0)

    kf = k.astype(BF16).astype(F32)
    kn = jnp.dot((kf * kf).astype(BF16), hsum_ref[...], preferred_element_type=F32)
    kn_ref[...] = jnp.max(kn.reshape(tm // SUBLANES, SUBLANES, LANES), axis=0)

    u = z_cv[:, 0:cw] * z_cv[:, cw:2 * cw]
    u_scr[8:tm + 8, :] = u
    u1 = u_scr[7:tm + 7, :]
    u2 = u_scr[6:tm + 6, :]
    w = wconv_ref[...]
    conv = w[2:3, :] * u + w[1:2, :] * u1 + w[0:1, :] * u2
    u_scr[0:8, :] = u[tm - 8:tm, :]

    v_t = jnp.transpose(v.astype(BF16).astype(F32))
    ones_rows = jnp.where(lax.broadcasted_iota(jnp.int32, (SUBLANES, tm), 0) == 0, 1.0, 0.0)
    for hh in range(N_HEADS):
        vt_ref[hh * LANES:(hh + 1) * LANES, :] = jnp.concatenate(
            [v_t[hh * HEAD_DIM:(hh + 1) * HEAD_DIM], ones_rows, pad_rows], axis=0).astype(BF16)

    ycn_ref[...] = _rms(z_b * conv, gconv_ref[...]).astype(BF16)


def _mix(x, g, w_in, layer, wf, bfr, wconv, gconv):
    s, d = x.shape
    tm = ROW_TILE
    hp = N_HEADS * LANES
    nt = s // tm
    hsum =np.zeros((ATT_WIDTH, LANES), np.float32)
    hsum[np.arange(ATT_WIDTH), np.arange(ATT_WIDTH) // HEAD_DIM] = 1.0
    row = lambda width: pl.BlockSpec((tm, width), lambda i: (i, 0))
    col = lambda height: pl.BlockSpec((height, tm), lambda i: (0, i))
    return pl.pallas_call(
        _mix_kernel,
        grid=(nt,),
        in_specs=[
            row(d), _const((1, d)), _layer_weight(w_in, layer), _layer_weight(wf, layer),
            _const((1, LANES)), _const((CONV_K, CONV_WIDTH)), _const((1, CONV_WIDTH)),
            _const((ATT_WIDTH, LANES)),
        ],
        out_specs=[row(CONV_WIDTH), col(hp), row(hp), col(hp), row(LANES),
                   pl.BlockSpec((N_HEADS, LANES), lambda i: (i, 0)),
                   pl.BlockSpec((N_HEADS, LANES), lambda i: (i, 0))],
        out_shape=[
            jax.ShapeDtypeStruct((s, CONV_WIDTH), BF16),
            jax.ShapeDtypeStruct((hp, s), BF16),
            jax.ShapeDtypeStruct((s, hp), BF16),
            jax.ShapeDtypeStruct((hp, s), BF16),
            jax.ShapeDtypeStruct((s, LANES), F32),
            jax.ShapeDtypeStruct((nt * N_HEADS, LANES), F32),
            jax.ShapeDtypeStruct((nt * N_HEADS, LANES), F32),
        ],
        scratch_shapes=[pltpu.VMEM((tm + 8, CONV_WIDTH), F32), pltpu.VMEM((8, LANES), F32)],
        compiler_params=pltpu.CompilerParams(
            dimension_semantics=("arbitrary",), vmem_limit_bytes=VMEM_LIMIT),
        name="mix_in",
    )(x, g, w_in, wf, bfr, wconv, gconv, jnp.asarray(hsum, BF16))


def _att_kernel(cst_ref, cen_ref, ub_ref, qt_ref, k_ref, vt_ref, tri_ref, o_ref, m_scr, acc_scr):
    group = pl.program_id(0)
    ib = pl.program_id(1)
    t = tri_ref.shape[0]
    chains = [(hh, qs) for hh in range(ATT_HEADS) for qs in range(ATT_QBLOCKS)]

    def block_index(qs):
        return ib * ATT_QBLOCKS + qs

    def blocks_past_first_two(hh, qs):
        hd = group * ATT_HEADS + hh
        i = block_index(qs)
        floor = -((F32_EXP_UNDERFLOW + BOUND_SLACK) * LOG2E + 2.0 * ub_ref[hd, i])
        c_first = cst_ref[hd, i]

        def needed(j):
            return jnp.logical_and(j > 0, c_first - cen_ref[hd, jnp.maximum(j - 1, 0)] >= floor)

        j_lo = lax.while_loop(needed, lambda j: j - 1, jnp.maximum(i - 1, 0))
        return jnp.maximum(i - 1, 0) - j_lo

    n_rest = [blocks_past_first_two(hh, qs) for hh, qs in chains]

    def scores(chain, start, size):
        hh, qs = chain
        return jnp.dot(k_ref[pl.ds(start, size), hh * LANES:(hh + 1) * LANES],
                       qt_ref[hh * LANES:(hh + 1) * LANES, qs * t:(qs + 1) * t],
                       preferred_element_type=F32)

    def values(chain, start, size, p):
        hh, _ = chain
        return jnp.dot(vt_ref[hh * LANES:hh * LANES + V_ROWS, pl.ds(start, size)], p,
                       preferred_element_type=F32)

    def run(first, rest):
        def diagonal_only(c):
            first(c, scores(chains[c], 0, t) + tri_ref[...], 0, t)

        def previous_and_diagonal(c):
            start = pl.multiple_of((block_index(chains[c][1]) - 1) * t, t)
            s = scores(chains[c], start, 2 * t)
            first(c, jnp.concatenate([s[:t], s[t:] + tri_ref[...]], axis=0), start, 2 * t)

        @pl.when(ib == 0)
        def _():
            for c, (_, qs) in enumerate(chains):
                (diagonal_only if qs == 0 else previous_and_diagonal)(c)

        @pl.when(ib > 0)
        def _():
            for c in range(len(chains)):
                previous_and_diagonal(c)

        for c, (_, qs) in enumerate(chains):
            def body(n, carry, c=c, qs=qs):
                start = pl.multiple_of((block_index(qs) - 2 - n) * t, t)
                rest(c, scores(chains[c], start, t), start, t)
                return carry

            lax.fori_loop(0, n_rest[c], body, 0)

    unshifted = None
    for hh, qs in chains:
        ok = ub_ref[group * ATT_HEADS + hh, block_index(qs)] <= NO_SHIFT_BOUND
        unshifted = ok if unshifted is None else jnp.logical_and(unshifted, ok)

    @pl.when(unshifted)
    def _():
        def first(c, s, start, size):
            acc_scr[c] = values(chains[c], start, size, jnp.exp2(s).astype(BF16))

        def rest(c, s, start, size):
            acc_scr[c] += values(chains[c], start, size, jnp.exp2(s).astype(BF16))

        run(first, rest)

    @pl.when(jnp.logical_not(unshifted))
    def _():
        def first(c, s, start, size):
            m = jnp.max(s, axis=0, keepdims=True)
            m_scr[c] = m
            acc_scr[c] = values(chains[c], start, size, jnp.exp2(s - m).astype(BF16))

        def rest(c, s, start, size):
            m_old = m_scr[c]
            m_new = jnp.maximum(m_old, jnp.max(s, axis=0, keepdims=True))
            m_scr[c] = m_new
            acc_scr[c] = jnp.exp2(m_old - m_new) * acc_scr[c] + values(
                chains[c], start, size, jnp.exp2(s - m_new).astype(BF16))

        run(first, rest)

    for c, (hh, qs) in enumerate(chains):
        acc = acc_scr[c]
        o_ref[hh * HEAD_DIM:(hh + 1) * HEAD_DIM, qs * t:(qs + 1) * t] = (
            acc[0:HEAD_DIM, :] / acc[HEAD_DIM:HEAD_DIM + 1, :])


def _attention(qt, k, vt, c, qn, kn):
    s = k.shape[0]
    t = ATT_TILE
    nq = s // t
    c_heads = c[:, 0:N_HEADS * SUBLANES:SUBLANES]
    c_first = c_heads[0::t].T
    c_last = c_heads[t - 1::t].T
    qn = qn.reshape(nq, N_HEADS, LANES)[:, :, 0].T
    kn = jnp.max(kn, axis=0)[:N_HEADS] * NORM_SLACK
    qk_bound = jnp.sqrt(qn * kn[:, None])
    tri = jnp.asarray(np.where(np.arange(t)[:, None] <= np.arange(t)[None, :], 0.0, -np.inf), F32)
    grid_spec = pltpu.PrefetchScalarGridSpec(
        num_scalar_prefetch=3,
        grid=(N_HEADS // ATT_HEADS, nq // ATT_QBLOCKS),
        in_specs=[
            pl.BlockSpec((ATT_HEADS * LANES, ATT_QBLOCKS * t), lambda g, i, *_: (g, i)),
            pl.BlockSpec((s, ATT_HEADS * LANES), lambda g, i, *_: (0, g)),
            pl.BlockSpec((ATT_HEADS * LANES, s), lambda g, i, *_: (g, 0)),
            pl.BlockSpec((t, t), lambda g, i, *_: (0, 0)),
        ],
        out_specs=pl.BlockSpec((ATT_HEADS * HEAD_DIM, ATT_QBLOCKS * t), lambda g, i, *_: (g, i)),
        scratch_shapes=[pltpu.VMEM((ATT_HEADS * ATT_QBLOCKS, 1, t), F32),
                        pltpu.VMEM((ATT_HEADS * ATT_QBLOCKS, V_ROWS, t), F32)],
    )
    return pl.pallas_call(
        _att_kernel,
        grid_spec=grid_spec,
        out_shape=jax.ShapeDtypeStruct((ATT_WIDTH, s), F32),
        compiler_params=pltpu.CompilerParams(
            dimension_semantics=("arbitrary", "arbitrary"), vmem_limit_bytes=VMEM_LIMIT),
        name="fox_attention",
    )(c_first, c_last, qk_bound, qt, k, vt, tri)


def _memkv_kernel(mem_ref, g_ref, w_ref, k_ref, v_ref):
    d = mem_ref.shape[1]
    m = _rms(mem_ref[...], g_ref[...]).astype(BF16)
    kv = jnp.dot(m, w_ref[...], preferred_element_type=F32)
    k_ref[...] = kv[:, :d].astype(BF16)
    v_ref[...] = kv[:, d:].astype(BF16)


def _memkv(mem, g, w_kv, layer):
    n, d = mem.shape
    whole = lambda shape: pl.BlockSpec(shape, lambda i: (0, 0))
    return pl.pallas_call(
        _memkv_kernel,
        grid=(1,),
        in_specs=[whole((n, d)), whole((1, d)), _layer_weight(w_kv, layer)],
        out_specs=[whole((n, d)), whole((n, d))],
        out_shape=[jax.ShapeDtypeStruct((n, d), BF16), jax.ShapeDtypeStruct((n, d), BF16)],
        compiler_params=pltpu.CompilerParams(
            dimension_semantics=("arbitrary",), vmem_limit_bytes=VMEM_LIMIT),
        name="mem_kv",
    )(mem, g, w_kv)


def _out_xattn_kernel(x_ref, ycn_ref, yat_ref, gatt_ref, wout_ref, gx_ref, wxq_ref,
                      kx_ref, vx_ref, wxo_ref, o_ref):
    d = x_ref.shape[1]
    xd = d // N_XHEADS
    ya_t = yat_ref[...]
    ms = jnp.mean(ya_t * ya_t, axis=0, keepdims=True)
    ya = jnp.transpose(ya_t * lax.rsqrt(ms + EPS))
    y = jnp.concatenate([ycn_ref[...], (ya * gatt_ref[...]).astype(BF16)], axis=-1)
    x1 = x_ref[...] + jnp.dot(y, wout_ref[...], preferred_element_type=F32)

    hx = _rms(x1, gx_ref[...]).astype(BF16)
    q = (jnp.dot(hx, wxq_ref[...], preferred_element_type=F32) * (xd ** -0.5)).astype(BF16)
    nt = (((1,), (1,)), ((), ()))
    heads = []
    for hh in range(N_XHEADS):
        sl = slice(hh * xd, (hh + 1) * xd)
        s = lax.dot_general(q[:, sl], kx_ref[:, sl], nt, preferred_element_type=F32)
        p = jnp.exp(s - jnp.max(s, axis=-1, keepdims=True))
        denom = jnp.sum(p, axis=-1, keepdims=True)
        o = jnp.dot(p.astype(BF16), vx_ref[:, sl], preferred_element_type=F32) / denom
        heads.append(o.astype(BF16))
    o = jnp.concatenate(heads, axis=-1)
    o_ref[...] = x1 + jnp.dot(o, wxo_ref[...], preferred_element_type=F32)


def _out_xattn(x, ycn, ya_t, gatt, wout, gx, wxq, kx, vx, wxo, layer):
    s, d = x.shape
    tm = ROW_TILE
    n_mem = kx.shape[0]
    row = lambda width: pl.BlockSpec((tm, width), lambda i: (i, 0))
    return pl.pallas_call(
        _out_xattn_kernel,
        grid=(s // tm,),
        in_specs=[
            row(d), row(CONV_WIDTH), pl.BlockSpec((ATT_WIDTH, tm), lambda i: (0, i)),
            _const((1, ATT_WIDTH)), _layer_weight(wout, layer), _const((1, d)),
            _layer_weight(wxq, layer), _const((n_mem, d)), _const((n_mem, d)),
            _layer_weight(wxo, layer),
        ],
        out_specs=row(d),
        out_shape=jax.ShapeDtypeStruct((s, d), F32),
        compiler_params=pltpu.CompilerParams(
            dimension_semantics=("arbitrary",), vmem_limit_bytes=VMEM_LIMIT),
        name="out_xattn",
    )(x, ycn, ya_t, gatt, wout, gx, wxq, kx, vx, wxo)


def kernel(x, mem, g_ffn1, w_ffn1_gu, w_ffn1_down, g_mix, w_mix_in, w_conv, b_f, g_conv_out,
           g_att_out, w_mix_out, g_xattn, g_mem, w_xq, w_xkv, w_xo, g_ffn2, w_ffn2_gu,
           w_ffn2_down, g_final):
    b, s, d = x.shape
    assert b == 1 and w_mix_in.shape[2] == 3 * CONV_WIDTH + 3 * ATT_WIDTH + N_HEADS
    depth = g_ffn1.shape[0]
    xs = x.reshape(s, d)
    mem2 = mem.reshape(mem.shape[1], d)
    row = lambda a: a.reshape(1, -1).astype(F32)
    gfin = row(g_final)
    w1gu, w1d = _to_bf16(w_ffn1_gu, 0, 1), _to_bf16(w_ffn1_down, 0, 1)
    win, wf = _cast_mix_in(w_mix_in)
    later = (w_ffn2_gu, w_ffn2_down, w_xq, w_xkv, w_xo, w_mix_out)
    gate_lanes = (0, LANES - N_HEADS * SUBLANES)
    w1_layer0 = 0

    for l in range(depth):
        if l == 0:
            xs, w2gu, w2d, wxq, wxkv, wxo, wout = _ffn(
                xs, row(g_ffn1[l]), w1gu, w1d, 0, gfin, final_norm=False, tm=FFN_SIDE_TILE,
                side=tuple((w, 0, depth) for w in later))
        else:
            (xs,) = _ffn(xs, row(g_ffn1[l]), w1gu, w1d, l - w1_layer0, gfin, final_norm=False,
                         tm=FFN_TILE)

        bfr = jnp.pad(jnp.repeat(b_f[l], SUBLANES), gate_lanes).reshape(1, LANES).astype(F32)
        ycn, qt, k, vt, c, qn, kn = _mix(xs, row(g_mix[l]), win, l, wf, bfr,
                                         w_conv[l].astype(F32), row(g_conv_out[l]))

        ya_t = _attention(qt, k, vt, c, qn, kn)

        kx, vx = _memkv(mem2, row(g_mem[l]), wxkv, l)
        xs = _out_xattn(xs, ycn, ya_t, row(g_att_out[l]), wout, row(g_xattn[l]), wxq, kx, vx,
                        wxo, l)

        last = l == depth - 1
        if l == 0 and not last:
            xs, w1gu, w1d = _ffn(xs, row(g_ffn2[l]), w2gu, w2d, l, gfin, final_norm=last,
                                 tm=FFN_TILE,
                                 side=((w_ffn1_gu, 1, depth - 1), (w_ffn1_down, 1, depth - 1)))
            w1_layer0 = 1
        else:
            (xs,) = _ffn(xs, row(g_ffn2[l]), w2gu, w2d, l, gfin, final_norm=last, tm=FFN_TILE)
    return xs.reshape(b, s, d)
```

```python
import functools

import numpy as np
import jax
import jax.numpy as jnp
from jax import lax
from jax.experimental import pallas as pl
from jax.experimental.pallas import tpu as pltpu

F32 = jnp.float32
BF16 = jnp.bfloat16

EPS = 1e-6
LANES = 128
SUBLANES = 8
BF16_ROWS = 16
MXU_DEPTH = 256
HEAD_DIM = 64
N_HEADS = 8
CONV_WIDTH = 512
ATT_WIDTH = N_HEADS * HEAD_DIM
CONV_K = 3
N_XHEADS = 4
N_SPLIT = 3
F32_EXP_UNDERFLOW = 104.0
BOUND_SLACK = 1.0
NORM_SLACK = 1.01
LOG2E = 1.4426950408889634
NO_SHIFT_BOUND = 40.0 * LOG2E
VMEM_LIMIT = 52 * 1024 * 1024
CAST_BLOCK_BYTES = 6 * 1024 * 1024

ROW_TILE = 512
FFN_TILE = 1024
FFN_SIDE_TILE = 512
ATT_TILE = ROW_TILE
ATT_HEADS = 2
ATT_QBLOCKS = 2
V_ROWS = 80


def _rms(x, g):
    ms = jnp.mean(x * x, axis=-1, keepdims=True)
    return x * lax.rsqrt(ms + EPS) * g


def _split3(x):
    hi = x.astype(BF16).astype(F32)
    r = x - hi
    mid = r.astype(BF16).astype(F32)
    lo = (r - mid).astype(BF16).astype(F32)
    return hi, mid, lo


def _layer_weight(w, layer):
    return pl.BlockSpec((None,) + w.shape[1:], lambda i: (layer, 0, 0), pipeline_mode=pl.Buffered(1))


def _const(shape):
    return pl.BlockSpec(shape, lambda i: (0, 0), pipeline_mode=pl.Buffered(1))


def _ffn_kernel(x_ref, g_ref, wgu_ref, wd_ref, gf_ref, *refs, chunks, final_norm, n_side):
    side_in, o_ref, side_out = refs[:n_side], refs[n_side], refs[n_side + 1:]
    for src, dst in zip(side_in, side_out):
        dst[...] = src[...].astype(BF16)

    d_ff = wd_ref.shape[0]
    x = x_ref[...]
    h = _rms(x, g_ref[...]).astype(BF16)
    acc = None
    for lo, hi in chunks:
        gate = jnp.dot(h, wgu_ref[:, lo:hi], preferred_element_type=F32)
        up = jnp.dot(h, wgu_ref[:, d_ff + lo:d_ff + hi], preferred_element_type=F32)
        act = (gate * (1.0 / (1.0 + jnp.exp(-gate))) * up).astype(BF16)
        part = jnp.dot(act, wd_ref[lo:hi, :], preferred_element_type=F32)
        acc = part if acc is None else acc + part
    y = x + 0.5 * acc
    if final_norm:
        y = _rms(y, gf_ref[...])
    o_ref[...] = y


def _ffn(x, g, w_gu, w_down, layer, g_final, *, final_norm, tm, side=()):
    s, d = x.shape
    d_ff = w_down.shape[1]
    n_steps = s // tm
    assert s % tm == 0 and d_ff % MXU_DEPTH == 0
    split = (d_ff // MXU_DEPTH + 1) // 2 * MXU_DEPTH
    chunks = ((0, split), (split, d_ff))
    side_args, side_in, side_out, side_shapes = [], [], [], []
    for w, first, count in side:
        _, rows, cols = w.shape
        slab = count * rows // n_steps
        assert slab * n_steps == count * rows and slab % BF16_ROWS == 0 and (first * rows) % slab == 0
        side_args.append(w.reshape(-1, cols))
        side_in.append(pl.BlockSpec((slab, cols), lambda i, off=first * rows // slab: (i + off, 0)))
        side_out.append(pl.BlockSpec((slab, cols), lambda i: (i, 0)))
        side_shapes.append(jax.ShapeDtypeStruct((count * rows, cols), BF16))
    out = pl.pallas_call(
        functools.partial(_ffn_kernel, chunks=chunks, final_norm=final_norm, n_side=len(side)),
        grid=(n_steps,),
        in_specs=[
            pl.BlockSpec((tm, d), lambda i: (i, 0)),
            _const((1, d)),
            _layer_weight(w_gu, layer),
            _layer_weight(w_down, layer),
            _const((1, d)),
        ] + side_in,
        out_specs=[pl.BlockSpec((tm, d), lambda i: (i, 0))] + side_out,
        out_shape=[jax.ShapeDtypeStruct((s, d), F32)] + side_shapes,
        compiler_params=pltpu.CompilerParams(
            dimension_semantics=("arbitrary",), vmem_limit_bytes=VMEM_LIMIT),
        name="ffn",
    )(x, g, w_gu, w_down, g_final, *side_args)
    converted = [o.reshape(count, w.shape[1], w.shape[2]) for o, (w, _, count) in zip(out[1:], side)]
    return (out[0], *converted)


def _cast_kernel(w_ref, o_ref):
    o_ref[...] = w_ref[...].astype(o_ref.dtype)


def _to_bf16(w, first=0, count=None):
    layers, rows, cols = w.shape
    count = layers - first if count is None else count
    tr = max(r for r in range(BF16_ROWS, rows + 1, BF16_ROWS)
             if rows % r == 0 and r * cols * 4 <= CAST_BLOCK_BYTES)
    return pl.pallas_call(
        _cast_kernel,
        grid=(count, rows // tr),
        in_specs=[pl.BlockSpec((None, tr, cols), lambda a, b: (a + first, b, 0))],
        out_specs=pl.BlockSpec((None, tr, cols), lambda a, b: (a, b, 0)),
        out_shape=jax.ShapeDtypeStruct((count, rows, cols), BF16),
        compiler_params=pltpu.CompilerParams(
            dimension_semantics=("arbitrary", "arbitrary"), vmem_limit_bytes=VMEM_LIMIT),
        name="cast_bf16",
    )(w)


def _cast_mix_in_kernel(wt_ref, gate_t_ref, spread_ref, o_ref, gate_ref):
    o_ref[...] = jnp.transpose(wt_ref[...]).astype(BF16)

    @pl.when(pl.program_id(1) == 0)
    def _():
        d = gate_t_ref.shape[1]
        g = jnp.concatenate([gate_t_ref[...], jnp.zeros((LANES - N_HEADS, d), F32)], axis=0)
        rep = jnp.dot(spread_ref[...], g.astype(BF16), preferred_element_type=F32)
        gate_ref[...] = jnp.transpose(rep).astype(BF16)


def _cast_mix_in(w):
    layers, d, cols = w.shape
    main = cols - N_HEADS
    cb = ROW_TILE
    assert main % cb == 0 and main % SUBLANES == 0
    wt = jnp.transpose(w, (0, 2, 1))
    spread = np.zeros((LANES, LANES), np.float32)
    for h in range(N_HEADS):
        spread[h * SUBLANES:(h + 1) * SUBLANES, h] = 1.0
    return pl.pallas_call(
        _cast_mix_in_kernel,
        grid=(layers, main // cb),
        in_specs=[pl.BlockSpec((None, cb, d), lambda a, b: (a, b, 0)),
                  pl.BlockSpec((None, N_HEADS, d), lambda a, b: (a, main // N_HEADS, 0)),
                  pl.BlockSpec((LANES, LANES), lambda a, b: (0, 0))],
        out_specs=[pl.BlockSpec((None, d, cb), lambda a, b: (a, 0, b)),
                   pl.BlockSpec((None, d, LANES), lambda a, b: (a, 0, 0))],
        out_shape=[jax.ShapeDtypeStruct((layers, d, main), BF16),
                   jax.ShapeDtypeStruct((layers, d, LANES), BF16)],
        compiler_params=pltpu.CompilerParams(
            dimension_semantics=("arbitrary", "arbitrary"), vmem_limit_bytes=VMEM_LIMIT),
        name="cast_mix_in",
    )(wt, wt, jnp.asarray(spread, BF16))


def _mix_kernel(x_ref, g_ref, win_ref, wf_ref, bf_ref, wconv_ref, gconv_ref, hsum_ref,
                ycn_ref, qt_ref, k_ref, vt_ref, c_ref, qn_ref, kn_ref, u_scr, carry_scr):
    i = pl.program_id(0)
    tm = x_ref.shape[0]
    cw, aw = CONV_WIDTH, ATT_WIDTH

    @pl.when(i == 0)
    def _():
        u_scr[0:8, :] = jnp.zeros((8, cw), F32)
        carry_scr[...] = jnp.zeros_like(carry_scr)

    h = _rms(x_ref[...], g_ref[...]).astype(BF16)
    zf = jnp.dot(h, wf_ref[...], preferred_element_type=F32) + bf_ref[...]
    qk = jnp.dot(h, win_ref[:, 3 * cw:3 * cw + 2 * aw], preferred_element_type=F32)

    logf = jnp.minimum(zf, 0.0) - jnp.log1p(jnp.exp(-jnp.abs(zf)))
    c = logf
    row_id = lax.broadcasted_iota(jnp.int32, c.shape, 0)
    shift = 1
    while shift < tm:
        c = c + jnp.where(row_id >= shift, pltpu.roll(c, shift, 0), 0.0)
        shift *= 2
    c = c + carry_scr[7:8, :]
    carry_scr[...] = c[tm - 8:tm, :]
    c2 = c * LOG2E
    c_ref[...] = c2

    hi, mid, lo = _split3(c2)
    lane = lax.broadcasted_iota(jnp.int32, c.shape, 1)
    r = lane % SUBLANES
    used = lane < N_HEADS * SUBLANES
    one = jnp.ones_like(c)
    zero = jnp.zeros_like(c)
    q_bias = jnp.where(r == 0, hi, jnp.where(r == 1, mid, jnp.where(r == 2, lo, jnp.where(r < 6, one, zero))))
    k_bias = jnp.where(r < 3, one, jnp.where(r == 3, -hi, jnp.where(r == 4, -mid, jnp.where(r == 5, -lo, zero))))
    q_bias_t = jnp.transpose(jnp.where(used, q_bias, zero))
    k_bias = jnp.where(used, k_bias, zero)

    z_cv = jnp.dot(h, win_ref[:, cw:3 * cw], preferred_element_type=F32)
    v = jnp.dot(h, win_ref[:, 3 * cw + 2 * aw:3 * cw + 3 * aw], preferred_element_type=F32)
    z_b = jnp.dot(h, win_ref[:, 0:cw], preferred_element_type=F32)

    q_t = jnp.transpose(qk[:, 0:aw] * (HEAD_DIM ** -0.5 * LOG2E))
    k = qk[:, aw:2 * aw]
    pad_rows = jnp.zeros((LANES - HEAD_DIM - SUBLANES, tm), F32)
    head_lane = lane < HEAD_DIM
    qn = []
    for hh in range(N_HEADS):
        rows = slice(hh * HEAD_DIM, (hh + 1) * HEAD_DIM)
        blk = slice(hh * LANES, (hh + 1) * LANES)
        q_h = q_t[rows]
        qt_ref[blk, :] = jnp.concatenate(
            [q_h, q_bias_t[hh * SUBLANES:(hh + 1) * SUBLANES], pad_rows], axis=0).astype(BF16)

        pair = k[:, (hh // 2) * LANES:(hh // 2 + 1) * LANES]
        k_pos = pair if hh % 2 == 0 else pltpu.roll(pair, HEAD_DIM, 1)
        bias = pltpu.roll(k_bias, (HEAD_DIM - hh * SUBLANES) % LANES, 1)
        k_h = jnp.where(head_lane, k_pos, bias).astype(BF16)
        k_ref[:, blk] = k_h

        qf = q_h.astype(BF16).astype(F32)
        qn.append(jnp.max(jnp.sum(qf * qf, axis=0, keepdims=True), axis=1, keepdims=True))
    qn_ref[...] = jnp.concatenate([jnp.broadcast_to(n, (1, LANES)) for n in qn], axis=0)

    kf = k.astype(BF16).astype(F32)
    kn = jnp.dot((kf * kf).astype(BF16), hsum_ref[...], preferred_element_type=F32)
    kn_ref[...] = jnp.max(kn.reshape(tm // SUBLANES, SUBLANES, LANES), axis=0)

    u = z_cv[:, 0:cw] * z_cv[:, cw:2 * cw]
    u_scr[8:tm + 8, :] = u
    u1 = u_scr[7:tm + 7, :]
    u2 = u_scr[6:tm + 6, :]
    w = wconv_ref[...]
    conv = w[2:3, :] * u + w[1:2, :] * u1 + w[0:1, :] * u2
    u_scr[0:8, :] = u[tm - 8:tm, :]

    v_t = jnp.transpose(v.astype(BF16).astype(F32))
    ones_rows = jnp.where(lax.broadcasted_iota(jnp.int32, (SUBLANES, tm), 0) == 0, 1.0, 0.0)
    for hh in range(N_HEADS):
        vt_ref[hh * LANES:(hh + 1) * LANES, :] = jnp.concatenate(
            [v_t[hh * HEAD_DIM:(hh + 1) * HEAD_DIM], ones_rows, pad_rows], axis=0).astype(BF16)

    ycn_ref[...] = _rms(z_b * conv, gconv_ref[...]).astype(BF16)


def _mix(x, g, w_in, layer, wf, bfr, wconv, gconv):
    s, d = x.shape
    tm = ROW_TILE
    hp = N_HEADS * LANES
    nt = s // tm
    hsum =np.zeros((ATT_WIDTH, LANES), np.float32)
    hsum[np.arange(ATT_WIDTH), np.arange(ATT_WIDTH) // HEAD_DIM] = 1.0
    row = lambda width: pl.BlockSpec((tm, width), lambda i: (i, 0))
    col = lambda height: pl.BlockSpec((height, tm), lambda i: (0, i))
    return pl.pallas_call(
        _mix_kernel,
        grid=(nt,),
        in_specs=[
            row(d), _const((1, d)), _layer_weight(w_in, layer), _layer_weight(wf, layer),
            _const((1, LANES)), _const((CONV_K, CONV_WIDTH)), _const((1, CONV_WIDTH)),
            _const((ATT_WIDTH, LANES)),
        ],
        out_specs=[row(CONV_WIDTH), col(hp), row(hp), col(hp), row(LANES),
                   pl.BlockSpec((N_HEADS, LANES), lambda i: (i, 0)),
                   pl.BlockSpec((N_HEADS, LANES), lambda i: (i, 0))],
        out_shape=[
            jax.ShapeDtypeStruct((s, CONV_WIDTH), BF16),
            jax.ShapeDtypeStruct((hp, s), BF16),
            jax.ShapeDtypeStruct((s, hp), BF16),
            jax.ShapeDtypeStruct((hp, s), BF16),
            jax.ShapeDtypeStruct((s, LANES), F32),
            jax.ShapeDtypeStruct((nt * N_HEADS, LANES), F32),
            jax.ShapeDtypeStruct((nt * N_HEADS, LANES), F32),
        ],
        scratch_shapes=[pltpu.VMEM((tm + 8, CONV_WIDTH), F32), pltpu.VMEM((8, LANES), F32)],
        compiler_params=pltpu.CompilerParams(
            dimension_semantics=("arbitrary",), vmem_limit_bytes=VMEM_LIMIT),
        name="mix_in",
    )(x, g, w_in, wf, bfr, wconv, gconv, jnp.asarray(hsum, BF16))


def _att_kernel(cst_ref, cen_ref, ub_ref, qt_ref, k_ref, vt_ref, tri_ref, o_ref, m_scr, acc_scr):
    group = pl.program_id(0)
    ib = pl.program_id(1)
    t = tri_ref.shape[0]
    chains = [(hh, qs) for hh in range(ATT_HEADS) for qs in range(ATT_QBLOCKS)]

    def block_index(qs):
        return ib * ATT_QBLOCKS + qs

    def blocks_past_first_two(hh, qs):
        hd = group * ATT_HEADS + hh
        i = block_index(qs)
        floor = -((F32_EXP_UNDERFLOW + BOUND_SLACK) * LOG2E + 2.0 * ub_ref[hd, i])
        c_first = cst_ref[hd, i]

        def needed(j):
            return jnp.logical_and(j > 0, c_first - cen_ref[hd, jnp.maximum(j - 1, 0)] >= floor)

        j_lo = lax.while_loop(needed, lambda j: j - 1, jnp.maximum(i - 1, 0))
        return jnp.maximum(i - 1, 0) - j_lo

    n_rest = [blocks_past_first_two(hh, qs) for hh, qs in chains]

    def scores(chain, start, size):
        hh, qs = chain
        return jnp.dot(k_ref[pl.ds(start, size), hh * LANES:(hh + 1) * LANES],
                       qt_ref[hh * LANES:(hh + 1) * LANES, qs * t:(qs + 1) * t],
                       preferred_element_type=F32)

    def values(chain, start, size, p):
        hh, _ = chain
        return jnp.dot(vt_ref[hh * LANES:hh * LANES + V_ROWS, pl.ds(start, size)], p,
                       preferred_element_type=F32)

    def run(first, rest):
        def diagonal_only(c):
            first(c, scores(chains[c], 0, t) + tri_ref[...], 0, t)

        def previous_and_diagonal(c):
            start = pl.multiple_of((block_index(chains[c][1]) - 1) * t, t)
            s = scores(chains[c], start, 2 * t)
            first(c, jnp.concatenate([s[:t], s[t:] + tri_ref[...]], axis=0), start, 2 * t)

        @pl.when(ib == 0)
        def _():
            for c, (_, qs) in enumerate(chains):
                (diagonal_only if qs == 0 else previous_and_diagonal)(c)

        @pl.when(ib > 0)
        def _():
            for c in range(len(chains)):
                previous_and_diagonal(c)

        for c, (_, qs) in enumerate(chains):
            def body(n, carry, c=c, qs=qs):
                start = pl.multiple_of((block_index(qs) - 2 - n) * t, t)
                rest(c, scores(chains[c], start, t), start, t)
                return carry

            lax.fori_loop(0, n_rest[c], body, 0)

    unshifted = None
    for hh, qs in chains:
        ok = ub_ref[group * ATT_HEADS + hh, block_index(qs)] <= NO_SHIFT_BOUND
        unshifted = ok if unshifted is None else jnp.logical_and(unshifted, ok)

    @pl.when(unshifted)
    def _():
        def first(c, s, start, size):
            acc_scr[c] = values(chains[c], start, size, jnp.exp2(s).astype(BF16))

        def rest(c, s, start, size):
            acc_scr[c] += values(chains[c], start, size, jnp.exp2(s).astype(BF16))

        run(first, rest)

    @pl.when(jnp.logical_not(unshifted))
    def _():
        def first(c, s, start, size):
            m = jnp.max(s, axis=0, keepdims=True)
            m_scr[c] = m
            acc_scr[c] = values(chains[c], start, size, jnp.exp2(s - m).astype(BF16))

        def rest(c, s, start, size):
            m_old = m_scr[c]
            m_new = jnp.maximum(m_old, jnp.max(s, axis=0, keepdims=True))
            m_scr[c] = m_new
            acc_scr[c] = jnp.exp2(m_old - m_new) * acc_scr[c] + values(
                chains[c], start, size, jnp.exp2(s - m_new).astype(BF16))

        run(first, rest)

    for c, (hh, qs) in enumerate(chains):
        acc = acc_scr[c]
        o_ref[hh * HEAD_DIM:(hh + 1) * HEAD_DIM, qs * t:(qs + 1) * t] = (
            acc[0:HEAD_DIM, :] / acc[HEAD_DIM:HEAD_DIM + 1, :])


def _attention(qt, k, vt, c, qn, kn):
    s = k.shape[0]
    t = ATT_TILE
    nq = s // t
    c_heads = c[:, 0:N_HEADS * SUBLANES:SUBLANES]
    c_first = c_heads[0::t].T
    c_last = c_heads[t - 1::t].T
    qn = qn.reshape(nq, N_HEADS, LANES)[:, :, 0].T
    kn = jnp.max(kn, axis=0)[:N_HEADS] * NORM_SLACK
    qk_bound = jnp.sqrt(qn * kn[:, None])
    tri = jnp.asarray(np.where(np.arange(t)[:, None] <= np.arange(t)[None, :], 0.0, -np.inf), F32)
    grid_spec = pltpu.PrefetchScalarGridSpec(
        num_scalar_prefetch=3,
        grid=(N_HEADS // ATT_HEADS, nq // ATT_QBLOCKS),
        in_specs=[
            pl.BlockSpec((ATT_HEADS * LANES, ATT_QBLOCKS * t), lambda g, i, *_: (g, i)),
            pl.BlockSpec((s, ATT_HEADS * LANES), lambda g, i, *_: (0, g)),
            pl.BlockSpec((ATT_HEADS * LANES, s), lambda g, i, *_: (g, 0)),
            pl.BlockSpec((t, t), lambda g, i, *_: (0, 0)),
        ],
        out_specs=pl.BlockSpec((ATT_HEADS * HEAD_DIM, ATT_QBLOCKS * t), lambda g, i, *_: (g, i)),
        scratch_shapes=[pltpu.VMEM((ATT_HEADS * ATT_QBLOCKS, 1, t), F32),
                        pltpu.VMEM((ATT_HEADS * ATT_QBLOCKS, V_ROWS, t), F32)],
    )
    return pl.pallas_call(
        _att_kernel,
        grid_spec=grid_spec,
        out_shape=jax.ShapeDtypeStruct((ATT_WIDTH, s), F32),
        compiler_params=pltpu.CompilerParams(
            dimension_semantics=("arbitrary", "arbitrary"), vmem_limit_bytes=VMEM_LIMIT),
        name="fox_attention",
    )(c_first, c_last, qk_bound, qt, k, vt, tri)


def _memkv_kernel(mem_ref, g_ref, w_ref, k_ref, v_ref):
    d = mem_ref.shape[1]
    m = _rms(mem_ref[...], g_ref[...]).astype(BF16)
    kv = jnp.dot(m, w_ref[...], preferred_element_type=F32)
    k_ref[...] = kv[:, :d].astype(BF16)
    v_ref[...] = kv[:, d:].astype(BF16)


def _memkv(mem, g, w_kv, layer):
    n, d = mem.shape
    whole = lambda shape: pl.BlockSpec(shape, lambda i: (0, 0))
    return pl.pallas_call(
        _memkv_kernel,
        grid=(1,),
        in_specs=[whole((n, d)), whole((1, d)), _layer_weight(w_kv, layer)],
        out_specs=[whole((n, d)), whole((n, d))],
        out_shape=[jax.ShapeDtypeStruct((n, d), BF16), jax.ShapeDtypeStruct((n, d), BF16)],
        compiler_params=pltpu.CompilerParams(
            dimension_semantics=("arbitrary",), vmem_limit_bytes=VMEM_LIMIT),
        name="mem_kv",
    )(mem, g, w_kv)


def _out_xattn_kernel(x_ref, ycn_ref, yat_ref, gatt_ref, wout_ref, gx_ref, wxq_ref,
                      kx_ref, vx_ref, wxo_ref, o_ref):
    d = x_ref.shape[1]
    xd = d // N_XHEADS
    ya_t = yat_ref[...]
    ms = jnp.mean(ya_t * ya_t, axis=0, keepdims=True)
    ya = jnp.transpose(ya_t * lax.rsqrt(ms + EPS))
    y = jnp.concatenate([ycn_ref[...], (ya * gatt_ref[...]).astype(BF16)], axis=-1)
    x1 = x_ref[...] + jnp.dot(y, wout_ref[...], preferred_element_type=F32)

    hx = _rms(x1, gx_ref[...]).astype(BF16)
    q = (jnp.dot(hx, wxq_ref[...], preferred_element_type=F32) * (xd ** -0.5)).astype(BF16)
    nt = (((1,), (1,)), ((), ()))
    heads = []
    for hh in range(N_XHEADS):
        sl = slice(hh * xd, (hh + 1) * xd)
        s = lax.dot_general(q[:, sl], kx_ref[:, sl], nt, preferred_element_type=F32)
        p = jnp.exp(s - jnp.max(s, axis=-1, keepdims=True))
        denom = jnp.sum(p, axis=-1, keepdims=True)
        o = jnp.dot(p.astype(BF16), vx_ref[:, sl], preferred_element_type=F32) / denom
        heads.append(o.astype(BF16))
    o = jnp.concatenate(heads, axis=-1)
    o_ref[...] = x1 + jnp.dot(o, wxo_ref[...], preferred_element_type=F32)


def _out_xattn(x, ycn, ya_t, gatt, wout, gx, wxq, kx, vx, wxo, layer):
    s, d = x.shape
    tm = ROW_TILE
    n_mem = kx.shape[0]
    row = lambda width: pl.BlockSpec((tm, width), lambda i: (i, 0))
    return pl.pallas_call(
        _out_xattn_kernel,
        grid=(s // tm,),
        in_specs=[
            row(d), row(CONV_WIDTH), pl.BlockSpec((ATT_WIDTH, tm), lambda i: (0, i)),
            _const((1, ATT_WIDTH)), _layer_weight(wout, layer), _const((1, d)),
            _layer_weight(wxq, layer), _const((n_mem, d)), _const((n_mem, d)),
            _layer_weight(wxo, layer),
        ],
        out_specs=row(d),
        out_shape=jax.ShapeDtypeStruct((s, d), F32),
        compiler_params=pltpu.CompilerParams(
            dimension_semantics=("arbitrary",), vmem_limit_bytes=VMEM_LIMIT),
        name="out_xattn",
    )(x, ycn, ya_t, gatt, wout, gx, wxq, kx, vx, wxo)


def kernel(x, mem, g_ffn1, w_ffn1_gu, w_ffn1_down, g_mix, w_mix_in, w_conv, b_f, g_conv_out,
           g_att_out, w_mix_out, g_xattn, g_mem, w_xq, w_xkv, w_xo, g_ffn2, w_ffn2_gu,
           w_ffn2_down, g_final):
    b, s, d = x.shape
    assert b == 1 and w_mix_in.shape[2] == 3 * CONV_WIDTH + 3 * ATT_WIDTH + N_HEADS
    depth = g_ffn1.shape[0]
    xs = x.reshape(s, d)
    mem2 = mem.reshape(mem.shape[1], d)
    row = lambda a: a.reshape(1, -1).astype(F32)
    gfin = row(g_final)
    w1gu, w1d = _to_bf16(w_ffn1_gu, 0, 1), _to_bf16(w_ffn1_down, 0, 1)
    win, wf = _cast_mix_in(w_mix_in)
    later = (w_ffn2_gu, w_ffn2_down, w_xq, w_xkv, w_xo, w_mix_out)
    gate_lanes = (0, LANES - N_HEADS * SUBLANES)
    w1_layer0 = 0

    for l in range(depth):
        if l == 0:
            xs, w2gu, w2d, wxq, wxkv, wxo, wout = _ffn(
                xs, row(g_ffn1[l]), w1gu, w1d, 0, gfin, final_norm=False, tm=FFN_SIDE_TILE,
                side=tuple((w, 0, depth) for w in later))
        else:
            (xs,) = _ffn(xs, row(g_ffn1[l]), w1gu, w1d, l - w1_layer0, gfin, final_norm=False,
                         tm=FFN_TILE)

        bfr = jnp.pad(jnp.repeat(b_f[l], SUBLANES), gate_lanes).reshape(1, LANES).astype(F32)
        ycn, qt, k, vt, c, qn, kn = _mix(xs, row(g_mix[l]), win, l, wf, bfr,
                                         w_conv[l].astype(F32), row(g_conv_out[l]))

        ya_t = _attention(qt, k, vt, c, qn, kn)

        kx, vx = _memkv(mem2, row(g_mem[l]), wxkv, l)
        xs = _out_xattn(xs, ycn, ya_t, row(g_att_out[l]), wout, row(g_xattn[l]), wxq, kx, vx,
                        wxo, l)

        last = l == depth - 1
        if l == 0 and not last:
            xs, w1gu, w1d = _ffn(xs, row(g_ffn2[l]), w2gu, w2d, l, gfin, final_norm=last,
                                 tm=FFN_TILE,
                                 side=((w_ffn1_gu, 1, depth - 1), (w_ffn1_down, 1, depth - 1)))
            w1_layer0 = 1
        else:
            (xs,) = _ffn(xs, row(g_ffn2[l]), w2gu, w2d, l, gfin, final_norm=last, tm=FFN_TILE)
    return xs.reshape(b, s, d)
```

```python
import functools

import numpy as np
import jax
import jax.numpy as jnp
from jax import lax
from jax.experimental import pallas as pl
from jax.experimental.pallas import tpu as pltpu

F32 = jnp.float32
BF16 = jnp.bfloat16

EPS = 1e-6
LANES = 128
SUBLANES = 8
BF16_ROWS = 16
MXU_DEPTH = 256
HEAD_DIM = 64
N_HEADS = 8
CONV_WIDTH = 512
ATT_WIDTH = N_HEADS * HEAD_DIM
CONV_K = 3
N_XHEADS = 4
N_SPLIT = 3
F32_EXP_UNDERFLOW = 104.0
BOUND_SLACK = 1.0
NORM_SLACK = 1.01
LOG2E = 1.4426950408889634
NO_SHIFT_BOUND = 40.0 * LOG2E
VMEM_LIMIT = 52 * 1024 * 1024
CAST_BLOCK_BYTES = 6 * 1024 * 1024

ROW_TILE = 512
FFN_TILE = 1024
FFN_SIDE_TILE = 512
ATT_TILE = ROW_TILE
ATT_HEADS = 2
ATT_QBLOCKS = 4
V_ROWS = 80


def _rms(x, g):
    ms = jnp.mean(x * x, axis=-1, keepdims=True)
    return x * lax.rsqrt(ms + EPS) * g


def _split3(x):
    hi = x.astype(BF16).astype(F32)
    r = x - hi
    mid = r.astype(BF16).astype(F32)
    lo = (r - mid).astype(BF16).astype(F32)
    return hi, mid, lo


def _layer_weight(w, layer):
    return pl.BlockSpec((None,) + w.shape[1:], lambda i: (layer, 0, 0), pipeline_mode=pl.Buffered(1))


def _const(shape):
    return pl.BlockSpec(shape, lambda i: (0, 0), pipeline_mode=pl.Buffered(1))


def _ffn_kernel(x_ref, g_ref, wgu_ref, wd_ref, gf_ref, *refs, chunks, final_norm, n_side):
    side_in, o_ref, side_out = refs[:n_side], refs[n_side], refs[n_side + 1:]
    for src, dst in zip(side_in, side_out):
        dst[...] = src[...].astype(BF16)

    d_ff = wd_ref.shape[0]
    x = x_ref[...]
    h = _rms(x, g_ref[...]).astype(BF16)
    acc = None
    for lo, hi in chunks:
        gate = jnp.dot(h, wgu_ref[:, lo:hi], preferred_element_type=F32)
        up = jnp.dot(h, wgu_ref[:, d_ff + lo:d_ff + hi], preferred_element_type=F32)
        act = (gate * (1.0 / (1.0 + jnp.exp(-gate))) * up).astype(BF16)
        part = jnp.dot(act, wd_ref[lo:hi, :], preferred_element_type=F32)
        acc = part if acc is None else acc + part
    y = x + 0.5 * acc
    if final_norm:
        y = _rms(y, gf_ref[...])
    o_ref[...] = y


def _ffn(x, g, w_gu, w_down, layer, g_final, *, final_norm, tm, side=()):
    s, d = x.shape
    d_ff = w_down.shape[1]
    n_steps = s // tm
    assert s % tm == 0 and d_ff % MXU_DEPTH == 0
    split = (d_ff // MXU_DEPTH + 1) // 2 * MXU_DEPTH
    chunks = ((0, split), (split, d_ff))
    side_args, side_in, side_out, side_shapes = [], [], [], []
    for w, first, count in side:
        _, rows, cols = w.shape
        slab = count * rows // n_steps
        assert slab * n_steps == count * rows and slab % BF16_ROWS == 0 and (first * rows) % slab == 0
        side_args.append(w.reshape(-1, cols))
        side_in.append(pl.BlockSpec((slab, cols), lambda i, off=first * rows // slab: (i + off, 0)))
        side_out.append(pl.BlockSpec((slab, cols), lambda i: (i, 0)))
        side_shapes.append(jax.ShapeDtypeStruct((count * rows, cols), BF16))
    out = pl.pallas_call(
        functools.partial(_ffn_kernel, chunks=chunks, final_norm=final_norm, n_side=len(side)),
        grid=(n_steps,),
        in_specs=[
            pl.BlockSpec((tm, d), lambda i: (i, 0)),
            _const((1, d)),
            _layer_weight(w_gu, layer),
            _layer_weight(w_down, layer),
            _const((1, d)),
        ] + side_in,
        out_specs=[pl.BlockSpec((tm, d), lambda i: (i, 0))] + side_out,
        out_shape=[jax.ShapeDtypeStruct((s, d), F32)] + side_shapes,
        compiler_params=pltpu.CompilerParams(
            dimension_semantics=("arbitrary",), vmem_limit_bytes=VMEM_LIMIT),
        name="ffn",
    )(x, g, w_gu, w_down, g_final, *side_args)
    converted = [o.reshape(count, w.shape[1], w.shape[2]) for o, (w, _, count) in zip(out[1:], side)]
    return (out[0], *converted)


def _cast_kernel(w_ref, o_ref):
    o_ref[...] = w_ref[...].astype(o_ref.dtype)


def _to_bf16(w, first=0, count=None):
    layers, rows, cols = w.shape
    count = layers - first if count is None else count
    tr = max(r for r in range(BF16_ROWS, rows + 1, BF16_ROWS)
             if rows % r == 0 and r * cols * 4 <= CAST_BLOCK_BYTES)
    return pl.pallas_call(
        _cast_kernel,
        grid=(count, rows // tr),
        in_specs=[pl.BlockSpec((None, tr, cols), lambda a, b: (a + first, b, 0))],
        out_specs=pl.BlockSpec((None, tr, cols), lambda a, b: (a, b, 0)),
        out_shape=jax.ShapeDtypeStruct((count, rows, cols), BF16),
        compiler_params=pltpu.CompilerParams(
            dimension_semantics=("arbitrary", "arbitrary"), vmem_limit_bytes=VMEM_LIMIT),
        name="cast_bf16",
    )(w)


def _cast_mix_in_kernel(wt_ref, gate_t_ref, spread_ref, o_ref, gate_ref):
    o_ref[...] = jnp.transpose(wt_ref[...]).astype(BF16)

    @pl.when(pl.program_id(1) == 0)
    def _():
        d = gate_t_ref.shape[1]
        g = jnp.concatenate([gate_t_ref[...], jnp.zeros((LANES - N_HEADS, d), F32)], axis=0)
        rep = jnp.dot(spread_ref[...], g.astype(BF16), preferred_element_type=F32)
        gate_ref[...] = jnp.transpose(rep.astype(BF16).astype(F32)).astype(BF16)


def _cast_mix_in(w):
    layers, d, cols = w.shape
    main = cols - N_HEADS
    cb = ROW_TILE
    assert main % cb == 0 and main % SUBLANES == 0
    wt = jnp.transpose(w, (0, 2, 1))
    spread = np.zeros((LANES, LANES), np.float32)
    for h in range(N_HEADS):
        spread[h * SUBLANES:(h + 1) * SUBLANES, h] = 1.0
    return pl.pallas_call(
        _cast_mix_in_kernel,
        grid=(layers, main // cb),
        in_specs=[pl.BlockSpec((None, cb, d), lambda a, b: (a, b, 0)),
                  pl.BlockSpec((None, N_HEADS, d), lambda a, b: (a, main // N_HEADS, 0)),
                  pl.BlockSpec((LANES, LANES), lambda a, b: (0, 0))],
        out_specs=[pl.BlockSpec((None, d, cb), lambda a, b: (a, 0, b)),
                   pl.BlockSpec((None, d, LANES), lambda a, b: (a, 0, 0))],
        out_shape=[jax.ShapeDtypeStruct((layers, d, main), BF16),
                   jax.ShapeDtypeStruct((layers, d, LANES), BF16)],
        compiler_params=pltpu.CompilerParams(
            dimension_semantics=("arbitrary", "arbitrary"), vmem_limit_bytes=VMEM_LIMIT),
        name="cast_mix_in",
    )(wt, wt, jnp.asarray(spread, BF16))


def _mix_kernel(x_ref, g_ref, win_ref, wf_ref, bf_ref, wconv_ref, gconv_ref, hsum_ref,
                ycn_ref, qt_ref, k_ref, vt_ref, c_ref, qn_ref, kn_ref, u_scr, carry_scr):
    i = pl.program_id(0)
    tm = x_ref.shape[0]
    cw, aw = CONV_WIDTH, ATT_WIDTH

    @pl.when(i == 0)
    def _():
        u_scr[0:8, :] = jnp.zeros((8, cw), F32)
        carry_scr[...] = jnp.zeros_like(carry_scr)

    h = _rms(x_ref[...], g_ref[...]).astype(BF16)
    zf = jnp.dot(h, wf_ref[...], preferred_element_type=F32) + bf_ref[...]
    qk = jnp.dot(h, win_ref[:, 3 * cw:3 * cw + 2 * aw], preferred_element_type=F32)

    logf = jnp.minimum(zf, 0.0) - jnp.log1p(jnp.exp(-jnp.abs(zf)))
    c = logf
    row_id = lax.broadcasted_iota(jnp.int32, c.shape, 0)
    shift = 1
    while shift < tm:
        c = c + jnp.where(row_id >= shift, pltpu.roll(c, shift, 0), 0.0)
        shift *= 2
    c = c + carry_scr[7:8, :]
    carry_scr[...] = c[tm - 8:tm, :]
    c2 = c * LOG2E
    c_ref[...] = c2

    hi, mid, lo = _split3(c2)
    lane = lax.broadcasted_iota(jnp.int32, c.shape, 1)
    r = lane % SUBLANES
    used = lane < N_HEADS * SUBLANES
    one = jnp.ones_like(c)
    zero = jnp.zeros_like(c)
    q_bias = jnp.where(r == 0, hi, jnp.where(r == 1, mid, jnp.where(r == 2, lo, jnp.where(r < 6, one, zero))))
    k_bias = jnp.where(r < 3, one, jnp.where(r == 3, -hi, jnp.where(r == 4, -mid, jnp.where(r == 5, -lo, zero))))
    q_bias_t = jnp.transpose(jnp.where(used, q_bias, zero))
    k_bias = jnp.where(used, k_bias, zero)

    z_cv = jnp.dot(h, win_ref[:, cw:3 * cw], preferred_element_type=F32)
    v = jnp.dot(h, win_ref[:, 3 * cw + 2 * aw:3 * cw + 3 * aw], preferred_element_type=F32)
    z_b = jnp.dot(h, win_ref[:, 0:cw], preferred_element_type=F32)

    q_t = jnp.transpose(qk[:, 0:aw] * (HEAD_DIM ** -0.5 * LOG2E))
    k = qk[:, aw:2 * aw]
    pad_rows = jnp.zeros((LANES - HEAD_DIM - SUBLANES, tm), F32)
    head_lane = lane < HEAD_DIM
    qn = []
    for hh in range(N_HEADS):
        rows = slice(hh * HEAD_DIM, (hh + 1) * HEAD_DIM)
        blk = slice(hh * LANES, (hh + 1) * LANES)
        q_h = q_t[rows]
        qt_ref[blk, :] = jnp.concatenate(
            [q_h, q_bias_t[hh * SUBLANES:(hh + 1) * SUBLANES], pad_rows], axis=0).astype(BF16)

        pair = k[:, (hh // 2) * LANES:(hh // 2 + 1) * LANES]
        k_pos = pair if hh % 2 == 0 else pltpu.roll(pair, HEAD_DIM, 1)
        bias = pltpu.roll(k_bias, (HEAD_DIM - hh * SUBLANES) % LANES, 1)
        k_h = jnp.where(head_lane, k_pos, bias).astype(BF16)
        k_ref[:, blk] = k_h

        qf = q_h.astype(BF16).astype(F32)
        qn.append(jnp.max(jnp.sum(qf * qf, axis=0, keepdims=True), axis=1, keepdims=True))
    qn_ref[...] = jnp.concatenate([jnp.broadcast_to(n, (1, LANES)) for n in qn], axis=0)

    kf = k.astype(BF16).astype(F32)
    kn = jnp.dot((kf * kf).astype(BF16), hsum_ref[...], preferred_element_type=F32)
    kn_ref[...] = jnp.max(kn.reshape(tm // SUBLANES, SUBLANES, LANES), axis=0)

    u = z_cv[:, 0:cw] * z_cv[:, cw:2 * cw]
    u_scr[8:tm + 8, :] = u
    u1 = u_scr[7:tm + 7, :]
    u2 = u_scr[6:tm + 6, :]
    w = wconv_ref[...]
    conv = w[2:3, :] * u + w[1:2, :] * u1 + w[0:1, :] * u2
    u_scr[0:8, :] = u[tm - 8:tm, :]

    v_t = jnp.transpose(v.astype(BF16).astype(F32))
    ones_rows = jnp.where(lax.broadcasted_iota(jnp.int32, (SUBLANES, tm), 0) == 0, 1.0, 0.0)
    for hh in range(N_HEADS):
        vt_ref[hh * LANES:(hh + 1) * LANES, :] = jnp.concatenate(
            [v_t[hh * HEAD_DIM:(hh + 1) * HEAD_DIM], ones_rows, pad_rows], axis=0).astype(BF16)

    ycn_ref[...] = _rms(z_b * conv, gconv_ref[...]).astype(BF16)


def _mix(x, g, w_in, layer, wf, bfr, wconv, gconv):
    s, d = x.shape
    tm = ROW_TILE
    hp = N_HEADS * LANES
    nt = s // tm
    hsum =np.zeros((ATT_WIDTH, LANES), np.float32)
    hsum[np.arange(ATT_WIDTH), np.arange(ATT_WIDTH) // HEAD_DIM] = 1.0
    row = lambda width: pl.BlockSpec((tm, width), lambda i: (i, 0))
    col = lambda height: pl.BlockSpec((height, tm), lambda i: (0, i))
    return pl.pallas_call(
        _mix_kernel,
        grid=(nt,),
        in_specs=[
            row(d), _const((1, d)), _layer_weight(w_in, layer), _layer_weight(wf, layer),
            _const((1, LANES)), _const((CONV_K, CONV_WIDTH)), _const((1, CONV_WIDTH)),
            _const((ATT_WIDTH, LANES)),
        ],
        out_specs=[row(CONV_WIDTH), col(hp), row(hp), col(hp), row(LANES),
                   pl.BlockSpec((N_HEADS, LANES), lambda i: (i, 0)),
                   pl.BlockSpec((N_HEADS, LANES), lambda i: (i, 0))],
        out_shape=[
            jax.ShapeDtypeStruct((s, CONV_WIDTH), BF16),
            jax.ShapeDtypeStruct((hp, s), BF16),
            jax.ShapeDtypeStruct((s, hp), BF16),
            jax.ShapeDtypeStruct((hp, s), BF16),
            jax.ShapeDtypeStruct((s, LANES), F32),
            jax.ShapeDtypeStruct((nt * N_HEADS, LANES), F32),
            jax.ShapeDtypeStruct((nt * N_HEADS, LANES), F32),
        ],
        scratch_shapes=[pltpu.VMEM((tm + 8, CONV_WIDTH), F32), pltpu.VMEM((8, LANES), F32)],
        compiler_params=pltpu.CompilerParams(
            dimension_semantics=("arbitrary",), vmem_limit_bytes=VMEM_LIMIT),
        name="mix_in",
    )(x, g, w_in, wf, bfr, wconv, gconv, jnp.asarray(hsum, BF16))


def _att_kernel(cst_ref, cen_ref, ub_ref, qt_ref, k_ref, vt_ref, tri_ref, o_ref, m_scr, acc_scr):
    group = pl.program_id(0)
    ib = pl.program_id(1)
    t = tri_ref.shape[0]
    chains = [(hh, qs) for hh in range(ATT_HEADS) for qs in range(ATT_QBLOCKS)]

    def block_index(qs):
        return ib * ATT_QBLOCKS + qs

    def blocks_past_first_two(hh, qs):
        hd = group * ATT_HEADS + hh
        i = block_index(qs)
        floor = -((F32_EXP_UNDERFLOW + BOUND_SLACK) * LOG2E + 2.0 * ub_ref[hd, i])
        c_first = cst_ref[hd, i]

        def needed(j):
            return jnp.logical_and(j > 0, c_first - cen_ref[hd, jnp.maximum(j - 1, 0)] >= floor)

        j_lo = lax.while_loop(needed, lambda j: j - 1, jnp.maximum(i - 1, 0))
        return jnp.maximum(i - 1, 0) - j_lo

    n_rest = [blocks_past_first_two(hh, qs) for hh, qs in chains]

    def scores(chain, start, size):
        hh, qs = chain
        return jnp.dot(k_ref[pl.ds(start, size), hh * LANES:(hh + 1) * LANES],
                       qt_ref[hh * LANES:(hh + 1) * LANES, qs * t:(qs + 1) * t],
                       preferred_element_type=F32)

    def values(chain, start, size, p):
        hh, _ = chain
        return jnp.dot(vt_ref[hh * LANES:hh * LANES + V_ROWS, pl.ds(start, size)], p,
                       preferred_element_type=F32)

    def run(first, rest):
        def diagonal_only(c):
            first(c, scores(chains[c], 0, t) + tri_ref[...], 0, t)

        def previous_and_diagonal(c):
            start = pl.multiple_of((block_index(chains[c][1]) - 1) * t, t)
            s = scores(chains[c], start, 2 * t)
            first(c, jnp.concatenate([s[:t], s[t:] + tri_ref[...]], axis=0), start, 2 * t)

        @pl.when(ib == 0)
        def _():
            for c, (_, qs) in enumerate(chains):
                (diagonal_only if qs == 0 else previous_and_diagonal)(c)

        @pl.when(ib > 0)
        def _():
            for c in range(len(chains)):
                previous_and_diagonal(c)

        for c, (_, qs) in enumerate(chains):
            def body(n, carry, c=c, qs=qs):
                start = pl.multiple_of((block_index(qs) - 2 - n) * t, t)
                rest(c, scores(chains[c], start, t), start, t)
                return carry

            lax.fori_loop(0, n_rest[c], body, 0)

    unshifted = None
    for hh, qs in chains:
        ok = ub_ref[group * ATT_HEADS + hh, block_index(qs)] <= NO_SHIFT_BOUND
        unshifted = ok if unshifted is None else jnp.logical_and(unshifted, ok)

    @pl.when(unshifted)
    def _():
        def first(c, s, start, size):
            acc_scr[c] = values(chains[c], start, size, jnp.exp2(s).astype(BF16))

        def rest(c, s, start, size):
            acc_scr[c] += values(chains[c], start, size, jnp.exp2(s).astype(BF16))

        run(first, rest)

    @pl.when(jnp.logical_not(unshifted))
    def _():
        def first(c, s, start, size):
            m = jnp.max(s, axis=0, keepdims=True)
            m_scr[c] = m
            acc_scr[c] = values(chains[c], start, size, jnp.exp2(s - m).astype(BF16))

        def rest(c, s, start, size):
            m_old = m_scr[c]
            m_new = jnp.maximum(m_old, jnp.max(s, axis=0, keepdims=True))
            m_scr[c] = m_new
            acc_scr[c] = jnp.exp2(m_old - m_new) * acc_scr[c] + values(
                chains[c], start, size, jnp.exp2(s - m_new).astype(BF16))

        run(first, rest)

    for c, (hh, qs) in enumerate(chains):
        acc = acc_scr[c]
        o_ref[hh * HEAD_DIM:(hh + 1) * HEAD_DIM, qs * t:(qs + 1) * t] = (
            acc[0:HEAD_DIM, :] / acc[HEAD_DIM:HEAD_DIM + 1, :])


def _attention(qt, k, vt, c, qn, kn):
    s = k.shape[0]
    t = ATT_TILE
    nq = s // t
    c_heads = c[:, 0:N_HEADS * SUBLANES:SUBLANES]
    c_first = c_heads[0::t].T
    c_last = c_heads[t - 1::t].T
    qn = qn.reshape(nq, N_HEADS, LANES)[:, :, 0].T
    kn = jnp.max(kn, axis=0)[:N_HEADS] * NORM_SLACK
    qk_bound = jnp.sqrt(qn * kn[:, None])
    tri = jnp.asarray(np.where(np.arange(t)[:, None] <= np.arange(t)[None, :], 0.0, -np.inf), F32)
    grid_spec = pltpu.PrefetchScalarGridSpec(
        num_scalar_prefetch=3,
        grid=(N_HEADS // ATT_HEADS, nq // ATT_QBLOCKS),
        in_specs=[
            pl.BlockSpec((ATT_HEADS * LANES, ATT_QBLOCKS * t), lambda g, i, *_: (g, i)),
            pl.BlockSpec((s, ATT_HEADS * LANES), lambda g, i, *_: (0, g)),
            pl.BlockSpec((ATT_HEADS * LANES, s), lambda g, i, *_: (g, 0)),
            pl.BlockSpec((t, t), lambda g, i, *_: (0, 0)),
        ],
        out_specs=pl.BlockSpec((ATT_HEADS * HEAD_DIM, ATT_QBLOCKS * t), lambda g, i, *_: (g, i)),
        scratch_shapes=[pltpu.VMEM((ATT_HEADS * ATT_QBLOCKS, 1, t), F32),
                        pltpu.VMEM((ATT_HEADS * ATT_QBLOCKS, V_ROWS, t), F32)],
    )
    return pl.pallas_call(
        _att_kernel,
        grid_spec=grid_spec,
        out_shape=jax.ShapeDtypeStruct((ATT_WIDTH, s), F32),
        compiler_params=pltpu.CompilerParams(
            dimension_semantics=("arbitrary", "arbitrary"), vmem_limit_bytes=VMEM_LIMIT),
        name="fox_attention",
    )(c_first, c_last, qk_bound, qt, k, vt, tri)


def _memkv_kernel(mem_ref, g_ref, w_ref, k_ref, v_ref):
    d = mem_ref.shape[1]
    m = _rms(mem_ref[...], g_ref[...]).astype(BF16)
    kv = jnp.dot(m, w_ref[...], preferred_element_type=F32)
    k_ref[...] = kv[:, :d].astype(BF16)
    v_ref[...] = kv[:, d:].astype(BF16)


def _memkv(mem, g, w_kv, layer):
    n, d = mem.shape
    whole = lambda shape: pl.BlockSpec(shape, lambda i: (0, 0))
    return pl.pallas_call(
        _memkv_kernel,
        grid=(1,),
        in_specs=[whole((n, d)), whole((1, d)), _layer_weight(w_kv, layer)],
        out_specs=[whole((n, d)), whole((n, d))],
        out_shape=[jax.ShapeDtypeStruct((n, d), BF16), jax.ShapeDtypeStruct((n, d), BF16)],
        compiler_params=pltpu.CompilerParams(
            dimension_semantics=("arbitrary",), vmem_limit_bytes=VMEM_LIMIT),
        name="mem_kv",
    )(mem, g, w_kv)


def _out_xattn_kernel(x_ref, ycn_ref, yat_ref, gatt_ref, wout_ref, gx_ref, wxq_ref,
                      kx_ref, vx_ref, wxo_ref, o_ref):
    d = x_ref.shape[1]
    xd = d // N_XHEADS
    ya_t = yat_ref[...]
    ms = jnp.mean(ya_t * ya_t, axis=0, keepdims=True)
    ya = jnp.transpose(ya_t * lax.rsqrt(ms + EPS))
    y = jnp.concatenate([ycn_ref[...], (ya * gatt_ref[...]).astype(BF16)], axis=-1)
    x1 = x_ref[...] + jnp.dot(y, wout_ref[...], preferred_element_type=F32)

    hx = _rms(x1, gx_ref[...]).astype(BF16)
    q = (jnp.dot(hx, wxq_ref[...], preferred_element_type=F32) * (xd ** -0.5)).astype(BF16)
    nt = (((1,), (1,)), ((), ()))
    heads = []
    for hh in range(N_XHEADS):
        sl = slice(hh * xd, (hh + 1) * xd)
        s = lax.dot_general(q[:, sl], kx_ref[:, sl], nt, preferred_element_type=F32)
        p = jnp.exp(s - jnp.max(s, axis=-1, keepdims=True))
        denom = jnp.sum(p, axis=-1, keepdims=True)
        o = jnp.dot(p.astype(BF16), vx_ref[:, sl], preferred_element_type=F32) / denom
        heads.append(o.astype(BF16))
    o = jnp.concatenate(heads, axis=-1)
    o_ref[...] = x1 + jnp.dot(o, wxo_ref[...], preferred_element_type=F32)


def _out_xattn(x, ycn, ya_t, gatt, wout, gx, wxq, kx, vx, wxo, layer):
    s, d = x.shape
    tm = ROW_TILE
    n_mem = kx.shape[0]
    row = lambda width: pl.BlockSpec((tm, width), lambda i: (i, 0))
    return pl.pallas_call(
        _out_xattn_kernel,
        grid=(s // tm,),
        in_specs=[
            row(d), row(CONV_WIDTH), pl.BlockSpec((ATT_WIDTH, tm), lambda i: (0, i)),
            _const((1, ATT_WIDTH)), _layer_weight(wout, layer), _const((1, d)),
            _layer_weight(wxq, layer), _const((n_mem, d)), _const((n_mem, d)),
            _layer_weight(wxo, layer),
        ],
        out_specs=row(d),
        out_shape=jax.ShapeDtypeStruct((s, d), F32),
        compiler_params=pltpu.CompilerParams(
            dimension_semantics=("arbitrary",), vmem_limit_bytes=VMEM_LIMIT),
        name="out_xattn",
    )(x, ycn, ya_t, gatt, wout, gx, wxq, kx, vx, wxo)


def kernel(x, mem, g_ffn1, w_ffn1_gu, w_ffn1_down, g_mix, w_mix_in, w_conv, b_f, g_conv_out,
           g_att_out, w_mix_out, g_xattn, g_mem, w_xq, w_xkv, w_xo, g_ffn2, w_ffn2_gu,
           w_ffn2_down, g_final):
    b, s, d = x.shape
    assert b == 1 and w_mix_in.shape[2] == 3 * CONV_WIDTH + 3 * ATT_WIDTH + N_HEADS
    depth = g_ffn1.shape[0]
    xs = x.reshape(s, d)
    mem2 = mem.reshape(mem.shape[1], d)
    row = lambda a: a.reshape(1, -1).astype(F32)
    gfin = row(g_final)
    w1gu, w1d = _to_bf16(w_ffn1_gu, 0, 1), _to_bf16(w_ffn1_down, 0, 1)
    win, wf = _cast_mix_in(w_mix_in)
    later = (w_ffn2_gu, w_ffn2_down, w_xq, w_xkv, w_xo, w_mix_out)
    gate_lanes = (0, LANES - N_HEADS * SUBLANES)
    w1_layer0 = 0

    for l in range(depth):
        if l == 0:
            xs, w2gu, w2d, wxq, wxkv, wxo, wout = _ffn(
                xs, row(g_ffn1[l]), w1gu, w1d, 0, gfin, final_norm=False, tm=FFN_SIDE_TILE,
                side=tuple((w, 0, depth) for w in later))
        else:
            (xs,) = _ffn(xs, row(g_ffn1[l]), w1gu, w1d, l - w1_layer0, gfin, final_norm=False,
                         tm=FFN_TILE)

        bfr = jnp.pad(jnp.repeat(b_f[l], SUBLANES), gate_lanes).reshape(1, LANES).astype(F32)
        ycn, qt, k, vt, c, qn, kn = _mix(xs, row(g_mix[l]), win, l, wf, bfr,
                                         w_conv[l].astype(F32), row(g_conv_out[l]))

        ya_t = _attention(qt, k, vt, c, qn, kn)

        kx, vx = _memkv(mem2, row(g_mem[l]), wxkv, l)
        xs = _out_xattn(xs, ycn, ya_t, row(g_att_out[l]), wout, row(g_xattn[l]), wxq, kx, vx,
                        wxo, l)

        last = l == depth - 1
        if l == 0 and not last:
            xs, w1gu, w1d = _ffn(xs, row(g_ffn2[l]), w2gu, w2d, l, gfin, final_norm=last,
                                 tm=FFN_TILE,
                                 side=((w_ffn1_gu, 1, depth - 1), (w_ffn1_down, 1, depth - 1)))
            w1_layer0 = 1
        else:
            (xs,) = _ffn(xs, row(g_ffn2[l]), w2gu, w2d, l, gfin, final_norm=last, tm=FFN_TILE)
    return xs.reshape(b, s, d)
```

```python
import functools

import numpy as np
import jax
import jax.numpy as jnp
from jax import lax
from jax.experimental import pallas as pl
from jax.experimental.pallas import tpu as pltpu

F32 = jnp.float32
BF16 = jnp.bfloat16

EPS = 1e-6
LANES = 128
SUBLANES = 8
BF16_ROWS = 16
MXU_DEPTH = 256
HEAD_DIM = 64
N_HEADS = 8
CONV_WIDTH = 512
ATT_WIDTH = N_HEADS * HEAD_DIM
CONV_K = 3
N_XHEADS = 4
F32_EXP_UNDERFLOW = 104.0
BOUND_SLACK = 1.0
NORM_SLACK = 1.01
LOG2E = 1.4426950408889634
NO_SHIFT_BOUND = 40.0 * LOG2E
VMEM_LIMIT = 52 * 1024 * 1024
CAST_BLOCK_BYTES = 6 * 1024 * 1024

ROW_TILE = 512
OUT_TILE = 1024
FFN_TILE = 1024
FFN_SIDE_TILE = 512
ATT_TILE = ROW_TILE
ATT_HEADS = 2
ATT_QBLOCKS = 4
V_ROWS = 80


def _rms(x, g):
    ms = jnp.mean(x * x, axis=-1, keepdims=True)
    return x * lax.rsqrt(ms + EPS) * g


def _split3(x):
    hi = x.astype(BF16).astype(F32)
    r = x - hi
    mid = r.astype(BF16).astype(F32)
    lo = (r - mid).astype(BF16).astype(F32)
    return hi, mid, lo


def _layer_weight(w, layer):
    return pl.BlockSpec((None,) + w.shape[1:], lambda i: (layer, 0, 0), pipeline_mode=pl.Buffered(1))


def _const(shape):
    return pl.BlockSpec(shape, lambda i: (0, 0), pipeline_mode=pl.Buffered(1))


def _ffn_kernel(x_ref, g_ref, wgu_ref, wd_ref, gf_ref, *refs, chunks, final_norm, n_side):
    side_in, o_ref, side_out = refs[:n_side], refs[n_side], refs[n_side + 1:]
    for src, dst in zip(side_in, side_out):
        dst[...] = src[...].astype(BF16)

    d_ff = wd_ref.shape[0]
    x = x_ref[...]
    h = _rms(x, g_ref[...]).astype(BF16)
    acc = None
    for lo, hi in chunks:
        gate = jnp.dot(h, wgu_ref[:, lo:hi], preferred_element_type=F32)
        up = jnp.dot(h, wgu_ref[:, d_ff + lo:d_ff + hi], preferred_element_type=F32)
        act = (gate * (1.0 / (1.0 + jnp.exp(-gate))) * up).astype(BF16)
        part = jnp.dot(act, wd_ref[lo:hi, :], preferred_element_type=F32)
        acc = part if acc is None else acc + part
    y = x + 0.5 * acc
    if final_norm:
        y = _rms(y, gf_ref[...])
    o_ref[...] = y


def _ffn(x, g, w_gu, w_down, layer, g_final, *, final_norm, tm, side=()):
    s, d = x.shape
    d_ff = w_down.shape[1]
    n_steps = s // tm
    assert s % tm == 0 and d_ff % MXU_DEPTH == 0
    split = (d_ff // MXU_DEPTH + 1) // 2 * MXU_DEPTH
    chunks = ((0, split), (split, d_ff))
    side_args, side_in, side_out, side_shapes = [], [], [], []
    for w, first, count in side:
        _, rows, cols = w.shape
        slab = count * rows // n_steps
        assert slab * n_steps == count * rows and slab % BF16_ROWS == 0 and (first * rows) % slab == 0
        side_args.append(w.reshape(-1, cols))
        side_in.append(pl.BlockSpec((slab, cols), lambda i, off=first * rows // slab: (i + off, 0)))
        side_out.append(pl.BlockSpec((slab, cols), lambda i: (i, 0)))
        side_shapes.append(jax.ShapeDtypeStruct((count * rows, cols), BF16))
    out = pl.pallas_call(
        functools.partial(_ffn_kernel, chunks=chunks, final_norm=final_norm, n_side=len(side)),
        grid=(n_steps,),
        in_specs=[
            pl.BlockSpec((tm, d), lambda i: (i, 0)),
            _const((1, d)),
            _layer_weight(w_gu, layer),
            _layer_weight(w_down, layer),
            _const((1, d)),
        ] + side_in,
        out_specs=[pl.BlockSpec((tm, d), lambda i: (i, 0))] + side_out,
        out_shape=[jax.ShapeDtypeStruct((s, d), F32)] + side_shapes,
        compiler_params=pltpu.CompilerParams(
            dimension_semantics=("arbitrary",), vmem_limit_bytes=VMEM_LIMIT),
        name="ffn",
    )(x, g, w_gu, w_down, g_final, *side_args)
    converted = [o.reshape(count, w.shape[1], w.shape[2]) for o, (w, _, count) in zip(out[1:], side)]
    return (out[0], *converted)


def _cast_kernel(w_ref, o_ref):
    o_ref[...] = w_ref[...].astype(o_ref.dtype)


def _to_bf16(w, first=0, count=None):
    layers, rows, cols = w.shape
    count = layers - first if count is None else count
    tr = max(r for r in range(BF16_ROWS, rows + 1, BF16_ROWS)
             if rows % r == 0 and r * cols * 4 <= CAST_BLOCK_BYTES)
    return pl.pallas_call(
        _cast_kernel,
        grid=(count, rows // tr),
        in_specs=[pl.BlockSpec((None, tr, cols), lambda a, b: (a + first, b, 0))],
        out_specs=pl.BlockSpec((None, tr, cols), lambda a, b: (a, b, 0)),
        out_shape=jax.ShapeDtypeStruct((count, rows, cols), BF16),
        compiler_params=pltpu.CompilerParams(
            dimension_semantics=("arbitrary", "arbitrary"), vmem_limit_bytes=VMEM_LIMIT),
        name="cast_bf16",
    )(w)


def _cast_mix_in_kernel(wt_ref, gate_t_ref, spread_ref, o_ref, gate_ref):
    o_ref[...] = jnp.transpose(wt_ref[...]).astype(BF16)

    @pl.when(pl.program_id(1) == 0)
    def _():
        d = gate_t_ref.shape[1]
        g = jnp.concatenate([gate_t_ref[...], jnp.zeros((LANES - N_HEADS, d), F32)], axis=0)
        rep = jnp.dot(spread_ref[...], g.astype(BF16), preferred_element_type=F32)
        gate_ref[...] = jnp.transpose(rep.astype(BF16).astype(F32)).astype(BF16)


def _cast_mix_in(w):
    layers, d, cols = w.shape
    main = cols - N_HEADS
    cb = ROW_TILE
    assert main % cb == 0 and main % SUBLANES == 0
    wt = jnp.transpose(w, (0, 2, 1))
    spread = np.zeros((LANES, LANES), np.float32)
    for h in range(N_HEADS):
        spread[h * SUBLANES:(h + 1) * SUBLANES, h] = 1.0
    return pl.pallas_call(
        _cast_mix_in_kernel,
        grid=(layers, main // cb),
        in_specs=[pl.BlockSpec((None, cb, d), lambda a, b: (a, b, 0)),
                  pl.BlockSpec((None, N_HEADS, d), lambda a, b: (a, main // N_HEADS, 0)),
                  pl.BlockSpec((LANES, LANES), lambda a, b: (0, 0))],
        out_specs=[pl.BlockSpec((None, d, cb), lambda a, b: (a, 0, b)),
                   pl.BlockSpec((None, d, LANES), lambda a, b: (a, 0, 0))],
        out_shape=[jax.ShapeDtypeStruct((layers, d, main), BF16),
                   jax.ShapeDtypeStruct((layers, d, LANES), BF16)],
        compiler_params=pltpu.CompilerParams(
            dimension_semantics=("arbitrary", "arbitrary"), vmem_limit_bytes=VMEM_LIMIT),
        name="cast_mix_in",
    )(wt, wt, jnp.asarray(spread, BF16))


def _mix_kernel(x_ref, g_ref, win_ref, wf_ref, bf_ref, wconv_ref, gconv_ref, hsum_ref,
                ycn_ref, qt_ref, k_ref, vt_ref, c_ref, qn_ref, kn_ref, u_scr, carry_scr):
    i = pl.program_id(0)
    tm = x_ref.shape[0]
    cw, aw = CONV_WIDTH, ATT_WIDTH

    @pl.when(i == 0)
    def _():
        u_scr[0:8, :] = jnp.zeros((8, cw), F32)
        carry_scr[...] = jnp.zeros_like(carry_scr)

    h = _rms(x_ref[...], g_ref[...]).astype(BF16)
    zf = jnp.dot(h, wf_ref[...], preferred_element_type=F32) + bf_ref[...]
    qk = jnp.dot(h, win_ref[:, 3 * cw:3 * cw + 2 * aw], preferred_element_type=F32)

    logf = jnp.minimum(zf, 0.0) - jnp.log1p(jnp.exp(-jnp.abs(zf)))
    c = logf
    row_id = lax.broadcasted_iota(jnp.int32, c.shape, 0)
    shift = 1
    while shift < tm:
        c = c + jnp.where(row_id >= shift, pltpu.roll(c, shift, 0), 0.0)
        shift *= 2
    c = c + carry_scr[7:8, :]
    carry_scr[...] = c[tm - 8:tm, :]
    c2 = c * LOG2E
    c_ref[...] = jnp.concatenate(
        [c2[0:1], c2[tm - 1:tm], jnp.zeros((SUBLANES - 2, LANES), F32)], axis=0)

    hi, mid, lo = _split3(c2)
    lane = lax.broadcasted_iota(jnp.int32, c.shape, 1)
    r = lane % SUBLANES
    used = lane < N_HEADS * SUBLANES
    one = jnp.ones_like(c)
    zero = jnp.zeros_like(c)
    q_bias = jnp.where(r == 0, hi, jnp.where(r == 1, mid, jnp.where(r == 2, lo, jnp.where(r < 6, one, zero))))
    k_bias = jnp.where(r < 3, one, jnp.where(r == 3, -hi, jnp.where(r == 4, -mid, jnp.where(r == 5, -lo, zero))))
    q_bias_t = jnp.transpose(jnp.where(used, q_bias, zero))
    k_bias = jnp.where(used, k_bias, zero)

    z_cv = jnp.dot(h, win_ref[:, cw:3 * cw], preferred_element_type=F32)
    v = jnp.dot(h, win_ref[:, 3 * cw + 2 * aw:3 * cw + 3 * aw], preferred_element_type=F32)
    z_b = jnp.dot(h, win_ref[:, 0:cw], preferred_element_type=F32)

    q_t = jnp.transpose(qk[:, 0:aw] * (HEAD_DIM ** -0.5 * LOG2E))
    k = qk[:, aw:2 * aw]
    pad_rows = jnp.zeros((LANES - HEAD_DIM - SUBLANES, tm), F32)
    head_lane = lane < HEAD_DIM
    qn = []
    for hh in range(N_HEADS):
        rows = slice(hh * HEAD_DIM, (hh + 1) * HEAD_DIM)
        blk = slice(hh * LANES, (hh + 1) * LANES)
        q_h = q_t[rows]
        qt_ref[blk, :] = jnp.concatenate(
            [q_h, q_bias_t[hh * SUBLANES:(hh + 1) * SUBLANES], pad_rows], axis=0).astype(BF16)

        pair = k[:, (hh // 2) * LANES:(hh // 2 + 1) * LANES]
        k_pos = pair if hh % 2 == 0 else pltpu.roll(pair, HEAD_DIM, 1)
        bias = pltpu.roll(k_bias, (HEAD_DIM - hh * SUBLANES) % LANES, 1)
        k_h = jnp.where(head_lane, k_pos, bias).astype(BF16)
        k_ref[:, blk] = k_h

        qf = q_h.astype(BF16).astype(F32)
        qn.append(jnp.max(jnp.sum(qf * qf, axis=0, keepdims=True), axis=1, keepdims=True))
    qn_ref[...] = jnp.concatenate([jnp.broadcast_to(n, (1, LANES)) for n in qn], axis=0)

    kf = k.astype(BF16).astype(F32)
    kn = jnp.dot((kf * kf).astype(BF16), hsum_ref[...], preferred_element_type=F32)
    kn_ref[...] = jnp.max(kn.reshape(tm // SUBLANES, SUBLANES, LANES), axis=0)

    u = z_cv[:, 0:cw] * z_cv[:, cw:2 * cw]
    u_scr[8:tm + 8, :] = u
    u1 = u_scr[7:tm + 7, :]
    u2 = u_scr[6:tm + 6, :]
    w = wconv_ref[...]
    conv = w[2:3, :] * u + w[1:2, :] * u1 + w[0:1, :] * u2
    u_scr[0:8, :] = u[tm - 8:tm, :]

    v_t = jnp.transpose(v.astype(BF16).astype(F32))
    ones_rows = jnp.where(lax.broadcasted_iota(jnp.int32, (SUBLANES, tm), 0) == 0, 1.0, 0.0)
    for hh in range(N_HEADS):
        vt_ref[hh * LANES:(hh + 1) * LANES, :] = jnp.concatenate(
            [v_t[hh * HEAD_DIM:(hh + 1) * HEAD_DIM], ones_rows, pad_rows], axis=0).astype(BF16)

    ycn_ref[...] = _rms(z_b * conv, gconv_ref[...]).astype(BF16)


def _mix(x, g, w_in, layer, wf, bfr, wconv, gconv):
    s, d = x.shape
    tm = ROW_TILE
    hp = N_HEADS * LANES
    nt = s // tm
    hsum = np.zeros((ATT_WIDTH, LANES), np.float32)
    hsum[np.arange(ATT_WIDTH), np.arange(ATT_WIDTH) // HEAD_DIM] = 1.0
    row = lambda width: pl.BlockSpec((tm, width), lambda i: (i, 0))
    col = lambda height: pl.BlockSpec((height, tm), lambda i: (0, i))
    per_tile = pl.BlockSpec((SUBLANES, LANES), lambda i: (i, 0))
    return pl.pallas_call(
        _mix_kernel,
        grid=(nt,),
        in_specs=[
            row(d), _const((1, d)), _layer_weight(w_in, layer), _layer_weight(wf, layer),
            _const((1, LANES)), _const((CONV_K, CONV_WIDTH)), _const((1, CONV_WIDTH)),
            _const((ATT_WIDTH, LANES)),
        ],
        out_specs=[row(CONV_WIDTH), col(hp), row(hp), col(hp), per_tile, per_tile, per_tile],
        out_shape=[
            jax.ShapeDtypeStruct((s, CONV_WIDTH), BF16),
            jax.ShapeDtypeStruct((hp, s), BF16),
            jax.ShapeDtypeStruct((s, hp), BF16),
            jax.ShapeDtypeStruct((hp, s), BF16),
        ] + [jax.ShapeDtypeStruct((nt * SUBLANES, LANES), F32)] * 3,
        scratch_shapes=[pltpu.VMEM((tm + 8, CONV_WIDTH), F32), pltpu.VMEM((8, LANES), F32)],
        compiler_params=pltpu.CompilerParams(
            dimension_semantics=("arbitrary",), vmem_limit_bytes=VMEM_LIMIT),
        name="mix_in",
    )(x, g, w_in, wf, bfr, wconv, gconv, jnp.asarray(hsum, BF16))


def _att_kernel(cst_ref, cen_ref, ub_ref, qt_ref, k_ref, vt_ref, tri_ref, o_ref, m_scr, acc_scr):
    group = pl.program_id(0)
    ib = pl.program_id(1)
    t = tri_ref.shape[0]
    chains = [(hh, qs) for hh in range(ATT_HEADS) for qs in range(ATT_QBLOCKS)]

    def block_index(qs):
        return ib * ATT_QBLOCKS + qs

    def blocks_past_first_two(hh, qs):
        hd = group * ATT_HEADS + hh
        i = block_index(qs)
        floor = -((F32_EXP_UNDERFLOW + BOUND_SLACK) * LOG2E + 2.0 * ub_ref[hd, i])
        c_first = cst_ref[hd, i]

        def needed(j):
            return jnp.logical_and(j > 0, c_first - cen_ref[hd, jnp.maximum(j - 1, 0)] >= floor)

        j_lo = lax.while_loop(needed, lambda j: j - 1, jnp.maximum(i - 1, 0))
        return jnp.maximum(i - 1, 0) - j_lo

    n_rest = [blocks_past_first_two(hh, qs) for hh, qs in chains]

    def scores(chain, start, size):
        hh, qs = chain
        return jnp.dot(k_ref[pl.ds(start, size), hh * LANES:(hh + 1) * LANES],
                       qt_ref[hh * LANES:(hh + 1) * LANES, qs * t:(qs + 1) * t],
                       preferred_element_type=F32)

    def values(chain, start, size, p):
        hh, _ = chain
        return jnp.dot(vt_ref[hh * LANES:hh * LANES + V_ROWS, pl.ds(start, size)], p,
                       preferred_element_type=F32)

    def run(first, rest):
        def diagonal_only(c):
            first(c, scores(chains[c], 0, t) + tri_ref[...], 0, t)

        def previous_and_diagonal(c):
            start = pl.multiple_of((block_index(chains[c][1]) - 1) * t, t)
            s = scores(chains[c], start, 2 * t)
            first(c, jnp.concatenate([s[:t], s[t:] + tri_ref[...]], axis=0), start, 2 * t)

        @pl.when(ib == 0)
        def _():
            for c, (_, qs) in enumerate(chains):
                (diagonal_only if qs == 0 else previous_and_diagonal)(c)

        @pl.when(ib > 0)
        def _():
            for c in range(len(chains)):
                previous_and_diagonal(c)

        for c, (_, qs) in enumerate(chains):
            def body(n, carry, c=c, qs=qs):
                start = pl.multiple_of((block_index(qs) - 2 - n) * t, t)
                rest(c, scores(chains[c], start, t), start, t)
                return carry

            lax.fori_loop(0, n_rest[c], body, 0)

    unshifted = None
    for hh, qs in chains:
        ok = ub_ref[group * ATT_HEADS + hh, block_index(qs)] <= NO_SHIFT_BOUND
        unshifted = ok if unshifted is None else jnp.logical_and(unshifted, ok)

    @pl.when(unshifted)
    def _():
        def first(c, s, start, size):
            acc_scr[c] = values(chains[c], start, size, jnp.exp2(s).astype(BF16))

        def rest(c, s, start, size):
            acc_scr[c] += values(chains[c], start, size, jnp.exp2(s).astype(BF16))

        run(first, rest)

    @pl.when(jnp.logical_not(unshifted))
    def _():
        def first(c, s, start, size):
            m = jnp.max(s, axis=0, keepdims=True)
            m_scr[c] = m
            acc_scr[c] = values(chains[c], start, size, jnp.exp2(s - m).astype(BF16))

        def rest(c, s, start, size):
            m_old = m_scr[c]
            m_new = jnp.maximum(m_old, jnp.max(s, axis=0, keepdims=True))
            m_scr[c] = m_new
            acc_scr[c] = jnp.exp2(m_old - m_new) * acc_scr[c] + values(
                chains[c], start, size, jnp.exp2(s - m_new).astype(BF16))

        run(first, rest)

    for c, (hh, qs) in enumerate(chains):
        acc = acc_scr[c]
        o_ref[hh * HEAD_DIM:(hh + 1) * HEAD_DIM, qs * t:(qs + 1) * t] = (
            acc[0:HEAD_DIM, :] / acc[HEAD_DIM:HEAD_DIM + 1, :])


def _attention(qt, k, vt, c, qn, kn):
    s = k.shape[0]
    t = ATT_TILE
    nq = s // t
    c_edges = c.reshape(nq, SUBLANES, LANES)[:, :, 0:N_HEADS * SUBLANES:SUBLANES]
    c_first = c_edges[:, 0].T
    c_last = c_edges[:, 1].T
    qn = qn.reshape(nq, N_HEADS, LANES)[:, :, 0].T
    kn = jnp.max(kn, axis=0)[:N_HEADS] * NORM_SLACK
    qk_bound = jnp.sqrt(qn * kn[:, None])
    tri = jnp.asarray(np.where(np.arange(t)[:, None] <= np.arange(t)[None, :], 0.0, -np.inf), F32)
    grid_spec = pltpu.PrefetchScalarGridSpec(
        num_scalar_prefetch=3,
        grid=(N_HEADS // ATT_HEADS, nq // ATT_QBLOCKS),
        in_specs=[
            pl.BlockSpec((ATT_HEADS * LANES, ATT_QBLOCKS * t), lambda g, i, *_: (g, i)),
            pl.BlockSpec((s, ATT_HEADS * LANES), lambda g, i, *_: (0, g)),
            pl.BlockSpec((ATT_HEADS * LANES, s), lambda g, i, *_: (g, 0)),
            pl.BlockSpec((t, t), lambda g, i, *_: (0, 0)),
        ],
        out_specs=pl.BlockSpec((ATT_HEADS * HEAD_DIM, ATT_QBLOCKS * t), lambda g, i, *_: (g, i)),
        scratch_shapes=[pltpu.VMEM((ATT_HEADS * ATT_QBLOCKS, 1, t), F32),
                        pltpu.VMEM((ATT_HEADS * ATT_QBLOCKS, V_ROWS, t), F32)],
    )
    return pl.pallas_call(
        _att_kernel,
        grid_spec=grid_spec,
        out_shape=jax.ShapeDtypeStruct((ATT_WIDTH, s), F32),
        compiler_params=pltpu.CompilerParams(
            dimension_semantics=("arbitrary", "arbitrary"), vmem_limit_bytes=VMEM_LIMIT),
        name="fox_attention",
    )(c_first, c_last, qk_bound, qt, k, vt, tri)


def _memkv_kernel(mem_ref, g_ref, w_ref, k_ref, v_ref):
    d = mem_ref.shape[1]
    m = _rms(mem_ref[...], g_ref[...]).astype(BF16)
    kv = jnp.dot(m, w_ref[...], preferred_element_type=F32)
    k_ref[...] = kv[:, :d].astype(BF16)
    v_ref[...] = kv[:, d:].astype(BF16)


def _memkv(mem, g, w_kv, layer):
    n, d = mem.shape
    whole = lambda shape: pl.BlockSpec(shape, lambda i: (0, 0))
    return pl.pallas_call(
        _memkv_kernel,
        grid=(1,),
        in_specs=[whole((n, d)), whole((1, d)), _layer_weight(w_kv, layer)],
        out_specs=[whole((n, d)), whole((n, d))],
        out_shape=[jax.ShapeDtypeStruct((n, d), BF16), jax.ShapeDtypeStruct((n, d), BF16)],
        compiler_params=pltpu.CompilerParams(
            dimension_semantics=("arbitrary",), vmem_limit_bytes=VMEM_LIMIT),
        name="mem_kv",
    )(mem, g, w_kv)


def _out_xattn_kernel(x_ref, ycn_ref, yat_ref, gatt_ref, wout_ref, gx_ref, wxq_ref,
                      kx_ref, vx_ref, wxo_ref, o_ref):
    d = x_ref.shape[1]
    xd = d // N_XHEADS
    ya_t = yat_ref[...]
    ms = jnp.mean(ya_t * ya_t, axis=0, keepdims=True)
    ya = jnp.transpose(ya_t * lax.rsqrt(ms + EPS))
    y = jnp.concatenate([ycn_ref[...], (ya * gatt_ref[...]).astype(BF16)], axis=-1)
    x1 = x_ref[...] + jnp.dot(y, wout_ref[...], preferred_element_type=F32)

    hx = _rms(x1, gx_ref[...]).astype(BF16)
    q = (jnp.dot(hx, wxq_ref[...], preferred_element_type=F32) * (xd ** -0.5)).astype(BF16)
    nt = (((1,), (1,)), ((), ()))
    heads = []
    for hh in range(N_XHEADS):
        sl = slice(hh * xd, (hh + 1) * xd)
        s = lax.dot_general(q[:, sl], kx_ref[:, sl], nt, preferred_element_type=F32)
        p = jnp.exp(s - jnp.max(s, axis=-1, keepdims=True))
        denom = jnp.sum(p, axis=-1, keepdims=True)
        o = jnp.dot(p.astype(BF16), vx_ref[:, sl], preferred_element_type=F32) / denom
        heads.append(o.astype(BF16))
    o = jnp.concatenate(heads, axis=-1)
    o_ref[...] = x1 + jnp.dot(o, wxo_ref[...], preferred_element_type=F32)


def _out_xattn(x, ycn, ya_t, gatt, wout, gx, wxq, kx, vx, wxo, layer):
    s, d = x.shape
    tm = OUT_TILE
    n_mem = kx.shape[0]
    row = lambda width: pl.BlockSpec((tm, width), lambda i: (i, 0))
    return pl.pallas_call(
        _out_xattn_kernel,
        grid=(s // tm,),
        in_specs=[
            row(d), row(CONV_WIDTH), pl.BlockSpec((ATT_WIDTH, tm), lambda i: (0, i)),
            _const((1, ATT_WIDTH)), _layer_weight(wout, layer), _const((1, d)),
            _layer_weight(wxq, layer), _const((n_mem, d)), _const((n_mem, d)),
            _layer_weight(wxo, layer),
        ],
        out_specs=row(d),
        out_shape=jax.ShapeDtypeStruct((s, d), F32),
        compiler_params=pltpu.CompilerParams(
            dimension_semantics=("arbitrary",), vmem_limit_bytes=VMEM_LIMIT),
        name="out_xattn",
    )(x, ycn, ya_t, gatt, wout, gx, wxq, kx, vx, wxo)


def kernel(x, mem, g_ffn1, w_ffn1_gu, w_ffn1_down, g_mix, w_mix_in, w_conv, b_f, g_conv_out,
           g_att_out, w_mix_out, g_xattn, g_mem, w_xq, w_xkv, w_xo, g_ffn2, w_ffn2_gu,
           w_ffn2_down, g_final):
    b, s, d = x.shape
    assert b == 1 and w_mix_in.shape[2] == 3 * CONV_WIDTH + 3 * ATT_WIDTH + N_HEADS
    assert N_HEADS == SUBLANES
    depth = g_ffn1.shape[0]
    xs = x.reshape(s, d)
    mem2 = mem.reshape(mem.shape[1], d)
    row = lambda a: a.reshape(1, -1).astype(F32)
    gfin = row(g_final)
    w1gu, w1d = _to_bf16(w_ffn1_gu, 0, 1), _to_bf16(w_ffn1_down, 0, 1)
    win, wf = _cast_mix_in(w_mix_in)
    later = (w_ffn2_gu, w_ffn2_down, w_xq, w_xkv, w_xo, w_mix_out)
    gate_lanes = (0, LANES - N_HEADS * SUBLANES)
    w1_layer0 = 0

    for l in range(depth):
        if l == 0:
            xs, w2gu, w2d, wxq, wxkv, wxo, wout = _ffn(
                xs, row(g_ffn1[l]), w1gu, w1d, 0, gfin, final_norm=False, tm=FFN_SIDE_TILE,
                side=tuple((w, 0, depth) for w in later))
        else:
            (xs,) = _ffn(xs, row(g_ffn1[l]), w1gu, w1d, l - w1_layer0, gfin, final_norm=False,
                         tm=FFN_TILE)

        bfr = jnp.pad(jnp.repeat(b_f[l], SUBLANES), gate_lanes).reshape(1, LANES).astype(F32)
        ycn, qt, k, vt, c, qn, kn = _mix(xs, row(g_mix[l]), win, l, wf, bfr,
                                         w_conv[l].astype(F32), row(g_conv_out[l]))

        ya_t = _attention(qt, k, vt, c, qn, kn)

        kx, vx = _memkv(mem2, row(g_mem[l]), wxkv, l)
        xs = _out_xattn(xs, ycn, ya_t, row(g_att_out[l]), wout, row(g_xattn[l]), wxq, kx, vx,
                        wxo, l)

        last = l == depth - 1
        if l == 0 and not last:
            xs, w1gu, w1d = _ffn(xs, row(g_ffn2[l]), w2gu, w2d, l, gfin, final_norm=last,
                                 tm=FFN_TILE,
                                 side=((w_ffn1_gu, 1, depth - 1), (w_ffn1_down, 1, depth - 1)))
            w1_layer0 = 1
        else:
            (xs,) = _ffn(xs, row(g_ffn2[l]), w2gu, w2d, l, gfin, final_norm=last, tm=FFN_TILE)
    return xs.reshape(b, s, d)
```

```python
import functools

import numpy as np
import jax
import jax.numpy as jnp
from jax import lax
from jax.experimental import pallas as pl
from jax.experimental.pallas import tpu as pltpu

F32 = jnp.float32
BF16 = jnp.bfloat16

EPS = 1e-6
LANES = 128
SUBLANES = 8
BF16_ROWS = 16
MXU_DEPTH = 256
HEAD_DIM = 64
N_HEADS = 8
CONV_WIDTH = 512
ATT_WIDTH = N_HEADS * HEAD_DIM
CONV_K = 3
N_XHEADS = 4
F32_EXP_UNDERFLOW = 104.0
BOUND_SLACK = 1.0
NORM_SLACK = 1.01
LOG2E = 1.4426950408889634
NO_SHIFT_BOUND = 40.0 * LOG2E
VMEM_LIMIT = 52 * 1024 * 1024
CAST_BLOCK_BYTES = 6 * 1024 * 1024

ROW_TILE = 512
OUT_TILE = 1024
FFN_TILE = 1024
FFN_SIDE_TILE = 512
ATT_TILE = ROW_TILE
ATT_HEADS = 2
ATT_QBLOCKS = 4
V_ROWS = 80


def _rms(x, g):
    ms = jnp.mean(x * x, axis=-1, keepdims=True)
    return x * lax.rsqrt(ms + EPS) * g


def _split3(x):
    hi = x.astype(BF16).astype(F32)
    r = x - hi
    mid = r.astype(BF16).astype(F32)
    lo = (r - mid).astype(BF16).astype(F32)
    return hi, mid, lo


def _layer_weight(w, layer):
    return pl.BlockSpec((None,) + w.shape[1:], lambda i: (layer, 0, 0), pipeline_mode=pl.Buffered(1))


def _const(shape):
    return pl.BlockSpec(shape, lambda i: (0, 0), pipeline_mode=pl.Buffered(1))


def _ffn_kernel(x_ref, g_ref, wgu_ref, wd_ref, gf_ref, *refs, chunks, final_norm, n_side):
    side_in, o_ref, side_out = refs[:n_side], refs[n_side], refs[n_side + 1:]
    for src, dst in zip(side_in, side_out):
        dst[...] = src[...].astype(BF16)

    d_ff = wd_ref.shape[0]
    x = x_ref[...]
    h = _rms(x, g_ref[...]).astype(BF16)
    acc = None
    for lo, hi in chunks:
        gate = jnp.dot(h, wgu_ref[:, lo:hi], preferred_element_type=F32)
        up = jnp.dot(h, wgu_ref[:, d_ff + lo:d_ff + hi], preferred_element_type=F32)
        act = (gate * (1.0 / (1.0 + jnp.exp(-gate))) * up).astype(BF16)
        part = jnp.dot(act, wd_ref[lo:hi, :], preferred_element_type=F32)
        acc = part if acc is None else acc + part
    y = x + 0.5 * acc
    if final_norm:
        y = _rms(y, gf_ref[...])
    o_ref[...] = y


def _ffn(x, g, w_gu, w_down, layer, g_final, *, final_norm, tm, side=()):
    s, d = x.shape
    d_ff = w_down.shape[1]
    n_steps = s // tm
    assert s % tm == 0 and d_ff % MXU_DEPTH == 0
    split = (d_ff // MXU_DEPTH + 1) // 2 * MXU_DEPTH
    chunks = ((0, split), (split, d_ff))
    side_args, side_in, side_out, side_shapes = [], [], [], []
    for w, first, count in side:
        _, rows, cols = w.shape
        slab = count * rows // n_steps
        assert slab * n_steps == count * rows and slab % BF16_ROWS == 0 and (first * rows) % slab == 0
        side_args.append(w.reshape(-1, cols))
        side_in.append(pl.BlockSpec((slab, cols), lambda i, off=first * rows // slab: (i + off, 0)))
        side_out.append(pl.BlockSpec((slab, cols), lambda i: (i, 0)))
        side_shapes.append(jax.ShapeDtypeStruct((count * rows, cols), BF16))
    out = pl.pallas_call(
        functools.partial(_ffn_kernel, chunks=chunks, final_norm=final_norm, n_side=len(side)),
        grid=(n_steps,),
        in_specs=[
            pl.BlockSpec((tm, d), lambda i: (i, 0)),
            _const((1, d)),
            _layer_weight(w_gu, layer),
            _layer_weight(w_down, layer),
            _const((1, d)),
        ] + side_in,
        out_specs=[pl.BlockSpec((tm, d), lambda i: (i, 0))] + side_out,
        out_shape=[jax.ShapeDtypeStruct((s, d), F32)] + side_shapes,
        compiler_params=pltpu.CompilerParams(
            dimension_semantics=("arbitrary",), vmem_limit_bytes=VMEM_LIMIT),
        name="ffn",
    )(x, g, w_gu, w_down, g_final, *side_args)
    converted = [o.reshape(count, w.shape[1], w.shape[2]) for o, (w, _, count) in zip(out[1:], side)]
    return (out[0], *converted)


def _cast_kernel(w_ref, o_ref):
    o_ref[...] = w_ref[...].astype(o_ref.dtype)


def _to_bf16(w, first=0, count=None):
    layers, rows, cols = w.shape
    count = layers - first if count is None else count
    tr = max(r for r in range(BF16_ROWS, rows + 1, BF16_ROWS)
             if rows % r == 0 and r * cols * 4 <= CAST_BLOCK_BYTES)
    return pl.pallas_call(
        _cast_kernel,
        grid=(count, rows // tr),
        in_specs=[pl.BlockSpec((None, tr, cols), lambda a, b: (a + first, b, 0))],
        out_specs=pl.BlockSpec((None, tr, cols), lambda a, b: (a, b, 0)),
        out_shape=jax.ShapeDtypeStruct((count, rows, cols), BF16),
        compiler_params=pltpu.CompilerParams(
            dimension_semantics=("arbitrary", "arbitrary"), vmem_limit_bytes=VMEM_LIMIT),
        name="cast_bf16",
    )(w)


def _cast_mix_in_kernel(wt_ref, gate_t_ref, spread_ref, o_ref, gate_ref):
    o_ref[...] = jnp.transpose(wt_ref[...]).astype(BF16)

    @pl.when(pl.program_id(1) == 0)
    def _():
        d = gate_t_ref.shape[1]
        g = jnp.concatenate([gate_t_ref[...], jnp.zeros((LANES - N_HEADS, d), F32)], axis=0)
        rep = jnp.dot(spread_ref[...], g.astype(BF16), preferred_element_type=F32)
        gate_ref[...] = jnp.transpose(rep.astype(BF16).astype(F32)).astype(BF16)


def _cast_mix_in(w):
    layers, d, cols = w.shape
    main = cols - N_HEADS
    cb = ROW_TILE
    assert main % cb == 0 and main % SUBLANES == 0
    wt = jnp.transpose(w, (0, 2, 1))
    spread = np.zeros((LANES, LANES), np.float32)
    for h in range(N_HEADS):
        spread[h * SUBLANES:(h + 1) * SUBLANES, h] = 1.0
    return pl.pallas_call(
        _cast_mix_in_kernel,
        grid=(layers, main // cb),
        in_specs=[pl.BlockSpec((None, cb, d), lambda a, b: (a, b, 0)),
                  pl.BlockSpec((None, N_HEADS, d), lambda a, b: (a, main // N_HEADS, 0)),
                  pl.BlockSpec((LANES, LANES), lambda a, b: (0, 0))],
        out_specs=[pl.BlockSpec((None, d, cb), lambda a, b: (a, 0, b)),
                   pl.BlockSpec((None, d, LANES), lambda a, b: (a, 0, 0))],
        out_shape=[jax.ShapeDtypeStruct((layers, d, main), BF16),
                   jax.ShapeDtypeStruct((layers, d, LANES), BF16)],
        compiler_params=pltpu.CompilerParams(
            dimension_semantics=("arbitrary", "arbitrary"), vmem_limit_bytes=VMEM_LIMIT),
        name="cast_mix_in",
    )(wt, wt, jnp.asarray(spread, BF16))


def _mix_kernel(x_ref, g_ref, win_ref, wf_ref, bf_ref, wconv_ref, gconv_ref, hsum_ref,
                ycn_ref, qt_ref, k_ref, vt_ref, c_ref, qn_ref, kn_ref, u_scr, carry_scr):
    i = pl.program_id(0)
    tm = x_ref.shape[0]
    cw, aw = CONV_WIDTH, ATT_WIDTH

    @pl.when(i == 0)
    def _():
        u_scr[0:8, :] = jnp.zeros((8, cw), F32)
        carry_scr[...] = jnp.zeros_like(carry_scr)

    h = _rms(x_ref[...], g_ref[...]).astype(BF16)
    zf = jnp.dot(h, wf_ref[...], preferred_element_type=F32) + bf_ref[...]
    qk = jnp.dot(h, win_ref[:, 3 * cw:3 * cw + 2 * aw], preferred_element_type=F32)

    logf = jnp.minimum(zf, 0.0) - jnp.log1p(jnp.exp(-jnp.abs(zf)))
    c = logf
    row_id = lax.broadcasted_iota(jnp.int32, c.shape, 0)
    shift = 1
    while shift < tm:
        c = c + jnp.where(row_id >= shift, pltpu.roll(c, shift, 0), 0.0)
        shift *= 2
    c = c + carry_scr[7:8, :]
    carry_scr[...] = c[tm - 8:tm, :]
    c2 = c * LOG2E
    c_ref[...] = jnp.concatenate(
        [c2[0:1], c2[tm - 1:tm], jnp.zeros((SUBLANES - 2, LANES), F32)], axis=0)

    hi, mid, lo = _split3(c2)
    lane = lax.broadcasted_iota(jnp.int32, c.shape, 1)
    r = lane % SUBLANES
    used = lane < N_HEADS * SUBLANES
    one = jnp.ones_like(c)
    zero = jnp.zeros_like(c)
    q_bias = jnp.where(r == 0, hi, jnp.where(r == 1, mid, jnp.where(r == 2, lo, jnp.where(r < 6, one, zero))))
    k_bias = jnp.where(r < 3, one, jnp.where(r == 3, -hi, jnp.where(r == 4, -mid, jnp.where(r == 5, -lo, zero))))
    q_bias_t = jnp.transpose(jnp.where(used, q_bias, zero))
    k_bias = jnp.where(used, k_bias, zero)

    z_cv = jnp.dot(h, win_ref[:, cw:3 * cw], preferred_element_type=F32)
    v = jnp.dot(h, win_ref[:, 3 * cw + 2 * aw:3 * cw + 3 * aw], preferred_element_type=F32)
    z_b = jnp.dot(h, win_ref[:, 0:cw], preferred_element_type=F32)

    q_t = jnp.transpose(qk[:, 0:aw] * (HEAD_DIM ** -0.5 * LOG2E))
    k = qk[:, aw:2 * aw]
    pad_rows = jnp.zeros((LANES - HEAD_DIM - SUBLANES, tm), F32)
    head_lane = lane < HEAD_DIM
    qn = []
    for hh in range(N_HEADS):
        rows = slice(hh * HEAD_DIM, (hh + 1) * HEAD_DIM)
        blk = slice(hh * LANES, (hh + 1) * LANES)
        q_h = q_t[rows]
        qt_ref[blk, :] = jnp.concatenate(
            [q_h, q_bias_t[hh * SUBLANES:(hh + 1) * SUBLANES], pad_rows], axis=0).astype(BF16)

        pair = k[:, (hh // 2) * LANES:(hh // 2 + 1) * LANES]
        k_pos = pair if hh % 2 == 0 else pltpu.roll(pair, HEAD_DIM, 1)
        bias = pltpu.roll(k_bias, (HEAD_DIM - hh * SUBLANES) % LANES, 1)
        k_h = jnp.where(head_lane, k_pos, bias).astype(BF16)
        k_ref[:, blk] = k_h

        qf = q_h.astype(BF16).astype(F32)
        qn.append(jnp.max(jnp.sum(qf * qf, axis=0, keepdims=True), axis=1, keepdims=True))
    qn_ref[...] = jnp.concatenate([jnp.broadcast_to(n, (1, LANES)) for n in qn], axis=0)

    kf = k.astype(BF16).astype(F32)
    kn = jnp.dot((kf * kf).astype(BF16), hsum_ref[...], preferred_element_type=F32)
    kn_ref[...] = jnp.max(kn.reshape(tm // SUBLANES, SUBLANES, LANES), axis=0)

    u = z_cv[:, 0:cw] * z_cv[:, cw:2 * cw]
    u_scr[8:tm + 8, :] = u
    u1 = u_scr[7:tm + 7, :]
    u2 = u_scr[6:tm + 6, :]
    w = wconv_ref[...]
    conv = w[2:3, :] * u + w[1:2, :] * u1 + w[0:1, :] * u2
    u_scr[0:8, :] = u[tm - 8:tm, :]

    v_t = jnp.transpose(v.astype(BF16).astype(F32))
    ones_rows = jnp.where(lax.broadcasted_iota(jnp.int32, (SUBLANES, tm), 0) == 0, 1.0, 0.0)
    for hh in range(N_HEADS):
        vt_ref[hh * LANES:(hh + 1) * LANES, :] = jnp.concatenate(
            [v_t[hh * HEAD_DIM:(hh + 1) * HEAD_DIM], ones_rows, pad_rows], axis=0).astype(BF16)

    ycn_ref[...] = _rms(z_b * conv, gconv_ref[...]).astype(BF16)


def _mix(x, g, w_in, layer, wf, bfr, wconv, gconv):
    s, d = x.shape
    tm = ROW_TILE
    hp = N_HEADS * LANES
    nt = s // tm
    hsum = np.zeros((ATT_WIDTH, LANES), np.float32)
    hsum[np.arange(ATT_WIDTH), np.arange(ATT_WIDTH) // HEAD_DIM] = 1.0
    row = lambda width: pl.BlockSpec((tm, width), lambda i: (i, 0))
    col = lambda height: pl.BlockSpec((height, tm), lambda i: (0, i))
    per_tile = pl.BlockSpec((SUBLANES, LANES), lambda i: (i, 0))
    return pl.pallas_call(
        _mix_kernel,
        grid=(nt,),
        in_specs=[
            row(d), _const((1, d)), _layer_weight(w_in, layer), _layer_weight(wf, layer),
            _const((1, LANES)), _const((CONV_K, CONV_WIDTH)), _const((1, CONV_WIDTH)),
            _const((ATT_WIDTH, LANES)),
        ],
        out_specs=[row(CONV_WIDTH), col(hp), row(hp), col(hp), per_tile, per_tile, per_tile],
        out_shape=[
            jax.ShapeDtypeStruct((s, CONV_WIDTH), BF16),
            jax.ShapeDtypeStruct((hp, s), BF16),
            jax.ShapeDtypeStruct((s, hp), BF16),
            jax.ShapeDtypeStruct((hp, s), BF16),
        ] + [jax.ShapeDtypeStruct((nt * SUBLANES, LANES), F32)] * 3,
        scratch_shapes=[pltpu.VMEM((tm + 8, CONV_WIDTH), F32), pltpu.VMEM((8, LANES), F32)],
        compiler_params=pltpu.CompilerParams(
            dimension_semantics=("arbitrary",), vmem_limit_bytes=VMEM_LIMIT),
        name="mix_in",
    )(x, g, w_in, wf, bfr, wconv, gconv, jnp.asarray(hsum, BF16))


def _att_kernel(cst_ref, cen_ref, ub_ref, qt_ref, k_ref, vt_ref, tri_ref, o_ref, m_scr, acc_scr):
    group = pl.program_id(0)
    ib = pl.program_id(1)
    t = tri_ref.shape[0]
    chains = [(hh, qs) for hh in range(ATT_HEADS) for qs in range(ATT_QBLOCKS)]

    def block_index(qs):
        return ib * ATT_QBLOCKS + qs

    def blocks_past_first_two(hh, qs):
        hd = group * ATT_HEADS + hh
        i = block_index(qs)
        floor = -((F32_EXP_UNDERFLOW + BOUND_SLACK) * LOG2E + 2.0 * ub_ref[hd, i])
        c_first = cst_ref[hd, i]

        def needed(j):
            return jnp.logical_and(j > 0, c_first - cen_ref[hd, jnp.maximum(j - 1, 0)] >= floor)

        j_lo = lax.while_loop(needed, lambda j: j - 1, jnp.maximum(i - 1, 0))
        return jnp.maximum(i - 1, 0) - j_lo

    n_rest = [blocks_past_first_two(hh, qs) for hh, qs in chains]

    def scores(chain, start, size):
        hh, qs = chain
        return jnp.dot(k_ref[pl.ds(start, size), hh * LANES:(hh + 1) * LANES],
                       qt_ref[hh * LANES:(hh + 1) * LANES, qs * t:(qs + 1) * t],
                       preferred_element_type=F32)

    def values(chain, start, size, p):
        hh, _ = chain
        return jnp.dot(vt_ref[hh * LANES:hh * LANES + V_ROWS, pl.ds(start, size)], p,
                       preferred_element_type=F32)

    def run(first, rest):
        def diagonal_only(c):
            first(c, scores(chains[c], 0, t) + tri_ref[...], 0, t)

        def previous_and_diagonal(c):
            start = pl.multiple_of((block_index(chains[c][1]) - 1) * t, t)
            s = scores(chains[c], start, 2 * t)
            first(c, jnp.concatenate([s[:t], s[t:] + tri_ref[...]], axis=0), start, 2 * t)

        @pl.when(ib == 0)
        def _():
            for c, (_, qs) in enumerate(chains):
                (diagonal_only if qs == 0 else previous_and_diagonal)(c)

        @pl.when(ib > 0)
        def _():
            for c in range(len(chains)):
                previous_and_diagonal(c)

        for hh in range(ATT_HEADS):
            members = [c for c, (h2, _) in enumerate(chains) if h2 == hh]
            trips = functools.reduce(jnp.maximum, [n_rest[c] for c in members])

            def body(n, carry, members=members):
                for c in members:
                    start = pl.multiple_of(jnp.maximum(block_index(chains[c][1]) - 2 - n, 0) * t, t)
                    rest(c, scores(chains[c], start, t), start, t, n < n_rest[c])
                return carry

            lax.fori_loop(0, trips, body, 0)

    unshifted = None
    for hh, qs in chains:
        ok = ub_ref[group * ATT_HEADS + hh, block_index(qs)] <= NO_SHIFT_BOUND
        unshifted = ok if unshifted is None else jnp.logical_and(unshifted, ok)

    @pl.when(unshifted)
    def _():
        def first(c, s, start, size):
            acc_scr[c] = values(chains[c], start, size, jnp.exp2(s).astype(BF16))

        def rest(c, s, start, size, live):
            pv = values(chains[c], start, size, jnp.exp2(s).astype(BF16))
            acc_scr[c] += jnp.where(live, pv, 0.0)

        run(first, rest)

    @pl.when(jnp.logical_not(unshifted))
    def _():
        def first(c, s, start, size):
            m = jnp.max(s, axis=0, keepdims=True)
            m_scr[c] = m
            acc_scr[c] = values(chains[c], start, size, jnp.exp2(s - m).astype(BF16))

        def rest(c, s, start, size, live):
            m_old = m_scr[c]
            m_new = jnp.where(live, jnp.maximum(m_old, jnp.max(s, axis=0, keepdims=True)), m_old)
            m_scr[c] = m_new
            pv = values(chains[c], start, size, jnp.exp2(s - m_new).astype(BF16))
            acc_scr[c] = jnp.exp2(m_old - m_new) * acc_scr[c] + jnp.where(live, pv, 0.0)

        run(first, rest)

    for c, (hh, qs) in enumerate(chains):
        acc = acc_scr[c]
        o_ref[hh * HEAD_DIM:(hh + 1) * HEAD_DIM, qs * t:(qs + 1) * t] = (
            acc[0:HEAD_DIM, :] / acc[HEAD_DIM:HEAD_DIM + 1, :])


def _attention(qt, k, vt, c, qn, kn):
    s = k.shape[0]
    t = ATT_TILE
    nq = s // t
    c_edges = c.reshape(nq, SUBLANES, LANES)[:, :, 0:N_HEADS * SUBLANES:SUBLANES]
    c_first = c_edges[:, 0].T
    c_last = c_edges[:, 1].T
    qn = qn.reshape(nq, N_HEADS, LANES)[:, :, 0].T
    kn = jnp.max(kn, axis=0)[:N_HEADS] * NORM_SLACK
    qk_bound = jnp.sqrt(qn * kn[:, None])
    tri = jnp.asarray(np.where(np.arange(t)[:, None] <= np.arange(t)[None, :], 0.0, -np.inf), F32)
    grid_spec = pltpu.PrefetchScalarGridSpec(
        num_scalar_prefetch=3,
        grid=(N_HEADS // ATT_HEADS, nq // ATT_QBLOCKS),
        in_specs=[
            pl.BlockSpec((ATT_HEADS * LANES, ATT_QBLOCKS * t), lambda g, i, *_: (g, i)),
            pl.BlockSpec((s, ATT_HEADS * LANES), lambda g, i, *_: (0, g)),
            pl.BlockSpec((ATT_HEADS * LANES, s), lambda g, i, *_: (g, 0)),
            pl.BlockSpec((t, t), lambda g, i, *_: (0, 0)),
        ],
        out_specs=pl.BlockSpec((ATT_HEADS * HEAD_DIM, ATT_QBLOCKS * t), lambda g, i, *_: (g, i)),
        scratch_shapes=[pltpu.VMEM((ATT_HEADS * ATT_QBLOCKS, 1, t), F32),
                        pltpu.VMEM((ATT_HEADS * ATT_QBLOCKS, V_ROWS, t), F32)],
    )
    return pl.pallas_call(
        _att_kernel,
        grid_spec=grid_spec,
        out_shape=jax.ShapeDtypeStruct((ATT_WIDTH, s), F32),
        compiler_params=pltpu.CompilerParams(
            dimension_semantics=("arbitrary", "arbitrary"), vmem_limit_bytes=VMEM_LIMIT),
        name="fox_attention",
    )(c_first, c_last, qk_bound, qt, k, vt, tri)


def _memkv_kernel(mem_ref, g_ref, w_ref, k_ref, v_ref):
    d = mem_ref.shape[1]
    m = _rms(mem_ref[...], g_ref[...]).astype(BF16)
    kv = jnp.dot(m, w_ref[...], preferred_element_type=F32)
    k_ref[...] = kv[:, :d].astype(BF16)
    v_ref[...] = kv[:, d:].astype(BF16)


def _memkv(mem, g, w_kv, layer):
    n, d = mem.shape
    whole = lambda shape: pl.BlockSpec(shape, lambda i: (0, 0))
    return pl.pallas_call(
        _memkv_kernel,
        grid=(1,),
        in_specs=[whole((n, d)), whole((1, d)), _layer_weight(w_kv, layer)],
        out_specs=[whole((n, d)), whole((n, d))],
        out_shape=[jax.ShapeDtypeStruct((n, d), BF16), jax.ShapeDtypeStruct((n, d), BF16)],
        compiler_params=pltpu.CompilerParams(
            dimension_semantics=("arbitrary",), vmem_limit_bytes=VMEM_LIMIT),
        name="mem_kv",
    )(mem, g, w_kv)


def _out_xattn_kernel(x_ref, ycn_ref, yat_ref, gatt_ref, wout_ref, gx_ref, wxq_ref,
                      kx_ref, vx_ref, wxo_ref, o_ref):
    d = x_ref.shape[1]
    xd = d // N_XHEADS
    ya_t = yat_ref[...]
    ms = jnp.mean(ya_t * ya_t, axis=0, keepdims=True)
    ya = jnp.transpose(ya_t * lax.rsqrt(ms + EPS))
    y = jnp.concatenate([ycn_ref[...], (ya * gatt_ref[...]).astype(BF16)], axis=-1)
    x1 = x_ref[...] + jnp.dot(y, wout_ref[...], preferred_element_type=F32)

    hx = _rms(x1, gx_ref[...]).astype(BF16)
    q = (jnp.dot(hx, wxq_ref[...], preferred_element_type=F32) * (xd ** -0.5)).astype(BF16)
    nt = (((1,), (1,)), ((), ()))
    heads = []
    for hh in range(N_XHEADS):
        sl = slice(hh * xd, (hh + 1) * xd)
        s = lax.dot_general(q[:, sl], kx_ref[:, sl], nt, preferred_element_type=F32)
        p = jnp.exp(s - jnp.max(s, axis=-1, keepdims=True))
        denom = jnp.sum(p, axis=-1, keepdims=True)
        o = jnp.dot(p.astype(BF16), vx_ref[:, sl], preferred_element_type=F32) / denom
        heads.append(o.astype(BF16))
    o = jnp.concatenate(heads, axis=-1)
    o_ref[...] = x1 + jnp.dot(o, wxo_ref[...], preferred_element_type=F32)


def _out_xattn(x, ycn, ya_t, gatt, wout, gx, wxq, kx, vx, wxo, layer):
    s, d = x.shape
    tm = OUT_TILE
    n_mem = kx.shape[0]
    row = lambda width: pl.BlockSpec((tm, width), lambda i: (i, 0))
    return pl.pallas_call(
        _out_xattn_kernel,
        grid=(s // tm,),
        in_specs=[
            row(d), row(CONV_WIDTH), pl.BlockSpec((ATT_WIDTH, tm), lambda i: (0, i)),
            _const((1, ATT_WIDTH)), _layer_weight(wout, layer), _const((1, d)),
            _layer_weight(wxq, layer), _const((n_mem, d)), _const((n_mem, d)),
            _layer_weight(wxo, layer),
        ],
        out_specs=row(d),
        out_shape=jax.ShapeDtypeStruct((s, d), F32),
        compiler_params=pltpu.CompilerParams(
            dimension_semantics=("arbitrary",), vmem_limit_bytes=VMEM_LIMIT),
        name="out_xattn",
    )(x, ycn, ya_t, gatt, wout, gx, wxq, kx, vx, wxo)


def kernel(x, mem, g_ffn1, w_ffn1_gu, w_ffn1_down, g_mix, w_mix_in, w_conv, b_f, g_conv_out,
           g_att_out, w_mix_out, g_xattn, g_mem, w_xq, w_xkv, w_xo, g_ffn2, w_ffn2_gu,
           w_ffn2_down, g_final):
    b, s, d = x.shape
    assert b == 1 and w_mix_in.shape[2] == 3 * CONV_WIDTH + 3 * ATT_WIDTH + N_HEADS
    assert N_HEADS == SUBLANES
    depth = g_ffn1.shape[0]
    xs = x.reshape(s, d)
    mem2 = mem.reshape(mem.shape[1], d)
    row = lambda a: a.reshape(1, -1).astype(F32)
    gfin = row(g_final)
    w1gu, w1d = _to_bf16(w_ffn1_gu, 0, 1), _to_bf16(w_ffn1_down, 0, 1)
    win, wf = _cast_mix_in(w_mix_in)
    later = (w_ffn2_gu, w_ffn2_down, w_xq, w_xkv, w_xo, w_mix_out)
    gate_lanes = (0, LANES - N_HEADS * SUBLANES)
    w1_layer0 = 0

    for l in range(depth):
        if l == 0:
            xs, w2gu, w2d, wxq, wxkv, wxo, wout = _ffn(
                xs, row(g_ffn1[l]), w1gu, w1d, 0, gfin, final_norm=False, tm=FFN_SIDE_TILE,
                side=tuple((w, 0, depth) for w in later))
        else:
            (xs,) = _ffn(xs, row(g_ffn1[l]), w1gu, w1d, l - w1_layer0, gfin, final_norm=False,
                         tm=FFN_TILE)

        bfr = jnp.pad(jnp.repeat(b_f[l], SUBLANES), gate_lanes).reshape(1, LANES).astype(F32)
        ycn, qt, k, vt, c, qn, kn = _mix(xs, row(g_mix[l]), win, l, wf, bfr,
                                         w_conv[l].astype(F32), row(g_conv_out[l]))

        ya_t = _attention(qt, k, vt, c, qn, kn)

        kx, vx = _memkv(mem2, row(g_mem[l]), wxkv, l)
        xs = _out_xattn(xs, ycn, ya_t, row(g_att_out[l]), wout, row(g_xattn[l]), wxq, kx, vx,
                        wxo, l)

        last = l == depth - 1
        if l == 0 and not last:
            xs, w1gu, w1d = _ffn(xs, row(g_ffn2[l]), w2gu, w2d, l, gfin, final_norm=last,
                                 tm=FFN_TILE,
                                 side=((w_ffn1_gu, 1, depth - 1), (w_ffn1_down, 1, depth - 1)))
            w1_layer0 = 1
        else:
            (xs,) = _ffn(xs, row(g_ffn2[l]), w2gu, w2d, l, gfin, final_norm=last, tm=FFN_TILE)
    return xs.reshape(b, s, d)
```

```python
import functools

import numpy as np
import jax
import jax.numpy as jnp
from jax import lax
from jax.experimental import pallas as pl
from jax.experimental.pallas import tpu as pltpu

F32 = jnp.float32
BF16 = jnp.bfloat16

EPS = 1e-6
LANES = 128
SUBLANES = 8
BF16_ROWS = 16
MXU_DEPTH = 256
HEAD_DIM = 64
N_HEADS = 8
CONV_WIDTH = 512
ATT_WIDTH = N_HEADS * HEAD_DIM
CONV_K = 3
N_XHEADS = 4
F32_EXP_UNDERFLOW = 104.0
BOUND_SLACK = 1.0
NORM_SLACK = 1.01
LOG2E = 1.4426950408889634
NO_SHIFT_BOUND = 40.0 * LOG2E
VMEM_LIMIT = 52 * 1024 * 1024
CAST_BLOCK_BYTES = 6 * 1024 * 1024

ROW_TILE = 512
OUT_TILE = 1024
FFN_TILE = 1024
FFN_SIDE_TILE = 512
ATT_TILE = ROW_TILE
ATT_HEADS = 2
ATT_QBLOCKS = 4
SCORE_GROUP = 4
V_ROWS = 80


def _rms(x, g):
    ms = jnp.mean(x * x, axis=-1, keepdims=True)
    return x * lax.rsqrt(ms + EPS) * g


def _split3(x):
    hi = x.astype(BF16).astype(F32)
    r = x - hi
    mid = r.astype(BF16).astype(F32)
    lo = (r - mid).astype(BF16).astype(F32)
    return hi, mid, lo


def _layer_weight(w, layer):
    return pl.BlockSpec((None,) + w.shape[1:], lambda i: (layer, 0, 0), pipeline_mode=pl.Buffered(1))


def _const(shape):
    return pl.BlockSpec(shape, lambda i: (0, 0), pipeline_mode=pl.Buffered(1))


def _ffn_kernel(x_ref, g_ref, wgu_ref, wd_ref, gf_ref, *refs, chunks, final_norm, n_side):
    side_in, o_ref, side_out = refs[:n_side], refs[n_side], refs[n_side + 1:]
    for src, dst in zip(side_in, side_out):
        dst[...] = src[...].astype(BF16)

    d_ff = wd_ref.shape[0]
    x = x_ref[...]
    h = _rms(x, g_ref[...]).astype(BF16)
    acc = None
    for lo, hi in chunks:
        gate = jnp.dot(h, wgu_ref[:, lo:hi], preferred_element_type=F32)
        up = jnp.dot(h, wgu_ref[:, d_ff + lo:d_ff + hi], preferred_element_type=F32)
        act = (gate * (1.0 / (1.0 + jnp.exp(-gate))) * up).astype(BF16)
        part = jnp.dot(act, wd_ref[lo:hi, :], preferred_element_type=F32)
        acc = part if acc is None else acc + part
    y = x + 0.5 * acc
    if final_norm:
        y = _rms(y, gf_ref[...])
    o_ref[...] = y


def _ffn(x, g, w_gu, w_down, layer, g_final, *, final_norm, tm, side=()):
    s, d = x.shape
    d_ff = w_down.shape[1]
    n_steps = s // tm
    assert s % tm == 0 and d_ff % MXU_DEPTH == 0
    split = (d_ff // MXU_DEPTH + 1) // 2 * MXU_DEPTH
    chunks = ((0, split), (split, d_ff))
    side_args, side_in, side_out, side_shapes = [], [], [], []
    for w, first, count in side:
        _, rows, cols = w.shape
        slab = count * rows // n_steps
        assert slab * n_steps == count * rows and slab % BF16_ROWS == 0 and (first * rows) % slab == 0
        side_args.append(w.reshape(-1, cols))
        side_in.append(pl.BlockSpec((slab, cols), lambda i, off=first * rows // slab: (i + off, 0)))
        side_out.append(pl.BlockSpec((slab, cols), lambda i: (i, 0)))
        side_shapes.append(jax.ShapeDtypeStruct((count * rows, cols), BF16))
    out = pl.pallas_call(
        functools.partial(_ffn_kernel, chunks=chunks, final_norm=final_norm, n_side=len(side)),
        grid=(n_steps,),
        in_specs=[
            pl.BlockSpec((tm, d), lambda i: (i, 0)),
            _const((1, d)),
            _layer_weight(w_gu, layer),
            _layer_weight(w_down, layer),
            _const((1, d)),
        ] + side_in,
        out_specs=[pl.BlockSpec((tm, d), lambda i: (i, 0))] + side_out,
        out_shape=[jax.ShapeDtypeStruct((s, d), F32)] + side_shapes,
        compiler_params=pltpu.CompilerParams(
            dimension_semantics=("arbitrary",), vmem_limit_bytes=VMEM_LIMIT),
        name="ffn",
    )(x, g, w_gu, w_down, g_final, *side_args)
    converted = [o.reshape(count, w.shape[1], w.shape[2]) for o, (w, _, count) in zip(out[1:], side)]
    return (out[0], *converted)


def _cast_kernel(w_ref, o_ref):
    o_ref[...] = w_ref[...].astype(o_ref.dtype)


def _to_bf16(w, first=0, count=None):
    layers, rows, cols = w.shape
    count = layers - first if count is None else count
    tr = max(r for r in range(BF16_ROWS, rows + 1, BF16_ROWS)
             if rows % r == 0 and r * cols * 4 <= CAST_BLOCK_BYTES)
    return pl.pallas_call(
        _cast_kernel,
        grid=(count, rows // tr),
        in_specs=[pl.BlockSpec((None, tr, cols), lambda a, b: (a + first, b, 0))],
        out_specs=pl.BlockSpec((None, tr, cols), lambda a, b: (a, b, 0)),
        out_shape=jax.ShapeDtypeStruct((count, rows, cols), BF16),
        compiler_params=pltpu.CompilerParams(
            dimension_semantics=("arbitrary", "arbitrary"), vmem_limit_bytes=VMEM_LIMIT),
        name="cast_bf16",
    )(w)


def _cast_mix_in_kernel(wt_ref, gate_t_ref, spread_ref, o_ref, gate_ref):
    o_ref[...] = jnp.transpose(wt_ref[...]).astype(BF16)

    @pl.when(pl.program_id(1) == 0)
    def _():
        d = gate_t_ref.shape[1]
        g = jnp.concatenate([gate_t_ref[...], jnp.zeros((LANES - N_HEADS, d), F32)], axis=0)
        rep = jnp.dot(spread_ref[...], g.astype(BF16), preferred_element_type=F32)
        gate_ref[...] = jnp.transpose(rep.astype(BF16).astype(F32)).astype(BF16)


def _cast_mix_in(w):
    layers, d, cols = w.shape
    main = cols - N_HEADS
    cb = ROW_TILE
    assert main % cb == 0 and main % SUBLANES == 0
    wt = jnp.transpose(w, (0, 2, 1))
    spread = np.zeros((LANES, LANES), np.float32)
    for h in range(N_HEADS):
        spread[h * SUBLANES:(h + 1) * SUBLANES, h] = 1.0
    return pl.pallas_call(
        _cast_mix_in_kernel,
        grid=(layers, main // cb),
        in_specs=[pl.BlockSpec((None, cb, d), lambda a, b: (a, b, 0)),
                  pl.BlockSpec((None, N_HEADS, d), lambda a, b: (a, main // N_HEADS, 0)),
                  pl.BlockSpec((LANES, LANES), lambda a, b: (0, 0))],
        out_specs=[pl.BlockSpec((None, d, cb), lambda a, b: (a, 0, b)),
                   pl.BlockSpec((None, d, LANES), lambda a, b: (a, 0, 0))],
        out_shape=[jax.ShapeDtypeStruct((layers, d, main), BF16),
                   jax.ShapeDtypeStruct((layers, d, LANES), BF16)],
        compiler_params=pltpu.CompilerParams(
            dimension_semantics=("arbitrary", "arbitrary"), vmem_limit_bytes=VMEM_LIMIT),
        name="cast_mix_in",
    )(wt, wt, jnp.asarray(spread, BF16))


def _mix_kernel(x_ref, g_ref, win_ref, wf_ref, bf_ref, wconv_ref, gconv_ref, hsum_ref,
                ycn_ref, qt_ref, k_ref, vt_ref, c_ref, qn_ref, kn_ref, u_scr, carry_scr):
    i = pl.program_id(0)
    tm = x_ref.shape[0]
    cw, aw = CONV_WIDTH, ATT_WIDTH

    @pl.when(i == 0)
    def _():
        u_scr[0:8, :] = jnp.zeros((8, cw), F32)
        carry_scr[...] = jnp.zeros_like(carry_scr)

    h = _rms(x_ref[...], g_ref[...]).astype(BF16)
    zf = jnp.dot(h, wf_ref[...], preferred_element_type=F32) + bf_ref[...]
    qk = jnp.dot(h, win_ref[:, 3 * cw:3 * cw + 2 * aw], preferred_element_type=F32)

    logf = jnp.minimum(zf, 0.0) - jnp.log1p(jnp.exp(-jnp.abs(zf)))
    c = logf
    row_id = lax.broadcasted_iota(jnp.int32, c.shape, 0)
    shift = 1
    while shift < tm:
        c = c + jnp.where(row_id >= shift, pltpu.roll(c, shift, 0), 0.0)
        shift *= 2
    c = c + carry_scr[7:8, :]
    carry_scr[...] = c[tm - 8:tm, :]
    c2 = c * LOG2E
    c_ref[...] = jnp.concatenate(
        [c2[0:1], c2[tm - 1:tm], jnp.zeros((SUBLANES - 2, LANES), F32)], axis=0)

    hi, mid, lo = _split3(c2)
    lane = lax.broadcasted_iota(jnp.int32, c.shape, 1)
    r = lane % SUBLANES
    used = lane < N_HEADS * SUBLANES
    one = jnp.ones_like(c)
    zero = jnp.zeros_like(c)
    q_bias = jnp.where(r == 0, hi, jnp.where(r == 1, mid, jnp.where(r == 2, lo, jnp.where(r < 6, one, zero))))
    k_bias = jnp.where(r < 3, one, jnp.where(r == 3, -hi, jnp.where(r == 4, -mid, jnp.where(r == 5, -lo, zero))))
    q_bias_t = jnp.transpose(jnp.where(used, q_bias, zero))
    k_bias = jnp.where(used, k_bias, zero)

    z_cv = jnp.dot(h, win_ref[:, cw:3 * cw], preferred_element_type=F32)
    v = jnp.dot(h, win_ref[:, 3 * cw + 2 * aw:3 * cw + 3 * aw], preferred_element_type=F32)
    z_b = jnp.dot(h, win_ref[:, 0:cw], preferred_element_type=F32)

    q_t = jnp.transpose(qk[:, 0:aw] * (HEAD_DIM ** -0.5 * LOG2E))
    k = qk[:, aw:2 * aw]
    pad_rows = jnp.zeros((LANES - HEAD_DIM - SUBLANES, tm), F32)
    head_lane = lane < HEAD_DIM
    qn = []
    for hh in range(N_HEADS):
        rows = slice(hh * HEAD_DIM, (hh + 1) * HEAD_DIM)
        blk = slice(hh * LANES, (hh + 1) * LANES)
        q_h = q_t[rows]
        qt_ref[blk, :] = jnp.concatenate(
            [q_h, q_bias_t[hh * SUBLANES:(hh + 1) * SUBLANES], pad_rows], axis=0).astype(BF16)

        pair = k[:, (hh // 2) * LANES:(hh // 2 + 1) * LANES]
        k_pos = pair if hh % 2 == 0 else pltpu.roll(pair, HEAD_DIM, 1)
        bias = pltpu.roll(k_bias, (HEAD_DIM - hh * SUBLANES) % LANES, 1)
        k_h = jnp.where(head_lane, k_pos, bias).astype(BF16)
        k_ref[:, blk] = k_h

        qf = q_h.astype(BF16).astype(F32)
        qn.append(jnp.max(jnp.sum(qf * qf, axis=0, keepdims=True), axis=1, keepdims=True))
    qn_ref[...] = jnp.concatenate([jnp.broadcast_to(n, (1, LANES)) for n in qn], axis=0)

    kf = k.astype(BF16).astype(F32)
    kn = jnp.dot((kf * kf).astype(BF16), hsum_ref[...], preferred_element_type=F32)
    kn_ref[...] = jnp.max(kn.reshape(tm // SUBLANES, SUBLANES, LANES), axis=0)

    u = z_cv[:, 0:cw] * z_cv[:, cw:2 * cw]
    u_scr[8:tm + 8, :] = u
    u1 = u_scr[7:tm + 7, :]
    u2 = u_scr[6:tm + 6, :]
    w = wconv_ref[...]
    conv = w[2:3, :] * u + w[1:2, :] * u1 + w[0:1, :] * u2
    u_scr[0:8, :] = u[tm - 8:tm, :]

    v_t = jnp.transpose(v.astype(BF16).astype(F32))
    ones_rows = jnp.where(lax.broadcasted_iota(jnp.int32, (SUBLANES, tm), 0) == 0, 1.0, 0.0)
    for hh in range(N_HEADS):
        vt_ref[hh * LANES:(hh + 1) * LANES, :] = jnp.concatenate(
            [v_t[hh * HEAD_DIM:(hh + 1) * HEAD_DIM], ones_rows, pad_rows], axis=0).astype(BF16)

    ycn_ref[...] = _rms(z_b * conv, gconv_ref[...]).astype(BF16)


def _mix(x, g, w_in, layer, wf, bfr, wconv, gconv):
    s, d = x.shape
    tm = ROW_TILE
    hp = N_HEADS * LANES
    nt = s // tm
    hsum = np.zeros((ATT_WIDTH, LANES), np.float32)
    hsum[np.arange(ATT_WIDTH), np.arange(ATT_WIDTH) // HEAD_DIM] = 1.0
    row = lambda width: pl.BlockSpec((tm, width), lambda i: (i, 0))
    col = lambda height: pl.BlockSpec((height, tm), lambda i: (0, i))
    per_tile = pl.BlockSpec((SUBLANES, LANES), lambda i: (i, 0))
    return pl.pallas_call(
        _mix_kernel,
        grid=(nt,),
        in_specs=[
            row(d), _const((1, d)), _layer_weight(w_in, layer), _layer_weight(wf, layer),
            _const((1, LANES)), _const((CONV_K, CONV_WIDTH)), _const((1, CONV_WIDTH)),
            _const((ATT_WIDTH, LANES)),
        ],
        out_specs=[row(CONV_WIDTH), col(hp), row(hp), col(hp), per_tile, per_tile, per_tile],
        out_shape=[
            jax.ShapeDtypeStruct((s, CONV_WIDTH), BF16),
            jax.ShapeDtypeStruct((hp, s), BF16),
            jax.ShapeDtypeStruct((s, hp), BF16),
            jax.ShapeDtypeStruct((hp, s), BF16),
        ] + [jax.ShapeDtypeStruct((nt * SUBLANES, LANES), F32)] * 3,
        scratch_shapes=[pltpu.VMEM((tm + 8, CONV_WIDTH), F32), pltpu.VMEM((8, LANES), F32)],
        compiler_params=pltpu.CompilerParams(
            dimension_semantics=("arbitrary",), vmem_limit_bytes=VMEM_LIMIT),
        name="mix_in",
    )(x, g, w_in, wf, bfr, wconv, gconv, jnp.asarray(hsum, BF16))


def _att_kernel(cst_ref, cen_ref, ub_ref, qt_ref, k_ref, vt_ref, tri_ref, o_ref, m_scr, acc_scr):
    group = pl.program_id(0)
    ib = pl.program_id(1)
    t = tri_ref.shape[0]
    chains = [(hh, qs) for hh in range(ATT_HEADS) for qs in range(ATT_QBLOCKS)]

    def block_index(qs):
        return ib * ATT_QBLOCKS + qs

    def blocks_past_first_two(hh, qs):
        hd = group * ATT_HEADS + hh
        i = block_index(qs)
        floor = -((F32_EXP_UNDERFLOW + BOUND_SLACK) * LOG2E + 2.0 * ub_ref[hd, i])
        c_first = cst_ref[hd, i]

        def needed(j):
            return jnp.logical_and(j > 0, c_first - cen_ref[hd, jnp.maximum(j - 1, 0)] >= floor)

        j_lo = lax.while_loop(needed, lambda j: j - 1, jnp.maximum(i - 1, 0))
        return jnp.maximum(i - 1, 0) - j_lo

    n_rest = [blocks_past_first_two(hh, qs) for hh, qs in chains]

    def scores(chain, start, size):
        hh, qs = chain
        return jnp.dot(k_ref[pl.ds(start, size), hh * LANES:(hh + 1) * LANES],
                       qt_ref[hh * LANES:(hh + 1) * LANES, qs * t:(qs + 1) * t],
                       preferred_element_type=F32)

    def values(chain, start, size, p):
        hh, _ = chain
        return jnp.dot(vt_ref[hh * LANES:hh * LANES + V_ROWS, pl.ds(start, size)], p,
                       preferred_element_type=F32)

    def run(first, rest):
        def diagonal_only(c):
            first(c, scores(chains[c], 0, t) + tri_ref[...], 0, t)

        def previous_and_diagonal(c):
            start = pl.multiple_of((block_index(chains[c][1]) - 1) * t, t)
            s = scores(chains[c], start, 2 * t)
            first(c, jnp.concatenate([s[:t], s[t:] + tri_ref[...]], axis=0), start, 2 * t)

        @pl.when(ib == 0)
        def _():
            for c, (_, qs) in enumerate(chains):
                (diagonal_only if qs == 0 else previous_and_diagonal)(c)

        @pl.when(ib > 0)
        def _():
            for c0 in range(0, len(chains), SCORE_GROUP):
                group_chains = range(c0, c0 + SCORE_GROUP)
                starts = [pl.multiple_of((block_index(chains[c][1]) - 1) * t, t) for c in group_chains]
                ss = [scores(chains[c], st, 2 * t) for c, st in zip(group_chains, starts)]
                for c, st, s in zip(group_chains, starts, ss):
                    first(c, jnp.concatenate([s[:t], s[t:] + tri_ref[...]], axis=0), st, 2 * t)

        for hh in range(ATT_HEADS):
            members = [c for c, (h2, _) in enumerate(chains) if h2 == hh]
            trips = functools.reduce(jnp.maximum, [n_rest[c] for c in members])

            def body(n, carry, members=members):
                starts = [pl.multiple_of(jnp.maximum(block_index(chains[c][1]) - 2 - n, 0) * t, t)
                          for c in members]
                ss = [scores(chains[c], st, t) for c, st in zip(members, starts)]
                for c, st, s in zip(members, starts, ss):
                    rest(c, s, st, t, n < n_rest[c])
                return carry

            lax.fori_loop(0, trips, body, 0)

    unshifted = None
    for hh, qs in chains:
        ok = ub_ref[group * ATT_HEADS + hh, block_index(qs)] <= NO_SHIFT_BOUND
        unshifted = ok if unshifted is None else jnp.logical_and(unshifted, ok)

    @pl.when(unshifted)
    def _():
        def first(c, s, start, size):
            acc_scr[c] = values(chains[c], start, size, jnp.exp2(s).astype(BF16))

        def rest(c, s, start, size, live):
            pv = values(chains[c], start, size, jnp.exp2(s).astype(BF16))
            acc_scr[c] += jnp.where(live, pv, 0.0)

        run(first, rest)

    @pl.when(jnp.logical_not(unshifted))
    def _():
        def first(c, s, start, size):
            m = jnp.max(s, axis=0, keepdims=True)
            m_scr[c] = m
            acc_scr[c] = values(chains[c], start, size, jnp.exp2(s - m).astype(BF16))

        def rest(c, s, start, size, live):
            m_old = m_scr[c]
            m_new = jnp.where(live, jnp.maximum(m_old, jnp.max(s, axis=0, keepdims=True)), m_old)
            m_scr[c] = m_new
            pv = values(chains[c], start, size, jnp.exp2(s - m_new).astype(BF16))
            acc_scr[c] = jnp.exp2(m_old - m_new) * acc_scr[c] + jnp.where(live, pv, 0.0)

        run(first, rest)

    for c, (hh, qs) in enumerate(chains):
        acc = acc_scr[c]
        o_ref[hh * HEAD_DIM:(hh + 1) * HEAD_DIM, qs * t:(qs + 1) * t] = (
            acc[0:HEAD_DIM, :] / acc[HEAD_DIM:HEAD_DIM + 1, :])


def _attention(qt, k, vt, c, qn, kn):
    s = k.shape[0]
    t = ATT_TILE
    nq = s // t
    c_edges = c.reshape(nq, SUBLANES, LANES)[:, :, 0:N_HEADS * SUBLANES:SUBLANES]
    c_first = c_edges[:, 0].T
    c_last = c_edges[:, 1].T
    qn = qn.reshape(nq, N_HEADS, LANES)[:, :, 0].T
    kn = jnp.max(kn, axis=0)[:N_HEADS] * NORM_SLACK
    qk_bound = jnp.sqrt(qn * kn[:, None])
    tri = jnp.asarray(np.where(np.arange(t)[:, None] <= np.arange(t)[None, :], 0.0, -np.inf), F32)
    grid_spec = pltpu.PrefetchScalarGridSpec(
        num_scalar_prefetch=3,
        grid=(N_HEADS // ATT_HEADS, nq // ATT_QBLOCKS),
        in_specs=[
            pl.BlockSpec((ATT_HEADS * LANES, ATT_QBLOCKS * t), lambda g, i, *_: (g, i)),
            pl.BlockSpec((s, ATT_HEADS * LANES), lambda g, i, *_: (0, g)),
            pl.BlockSpec((ATT_HEADS * LANES, s), lambda g, i, *_: (g, 0)),
            pl.BlockSpec((t, t), lambda g, i, *_: (0, 0)),
        ],
        out_specs=pl.BlockSpec((ATT_HEADS * HEAD_DIM, ATT_QBLOCKS * t), lambda g, i, *_: (g, i)),
        scratch_shapes=[pltpu.VMEM((ATT_HEADS * ATT_QBLOCKS, 1, t), F32),
                        pltpu.VMEM((ATT_HEADS * ATT_QBLOCKS, V_ROWS, t), F32)],
    )
    return pl.pallas_call(
        _att_kernel,
        grid_spec=grid_spec,
        out_shape=jax.ShapeDtypeStruct((ATT_WIDTH, s), F32),
        compiler_params=pltpu.CompilerParams(
            dimension_semantics=("arbitrary", "arbitrary"), vmem_limit_bytes=VMEM_LIMIT),
        name="fox_attention",
    )(c_first, c_last, qk_bound, qt, k, vt, tri)


def _memkv_kernel(mem_ref, g_ref, w_ref, k_ref, v_ref):
    d = mem_ref.shape[1]
    m = _rms(mem_ref[...], g_ref[...]).astype(BF16)
    kv = jnp.dot(m, w_ref[...], preferred_element_type=F32)
    k_ref[...] = kv[:, :d].astype(BF16)
    v_ref[...] = kv[:, d:].astype(BF16)


def _memkv(mem, g, w_kv, layer):
    n, d = mem.shape
    whole = lambda shape: pl.BlockSpec(shape, lambda i: (0, 0))
    return pl.pallas_call(
        _memkv_kernel,
        grid=(1,),
        in_specs=[whole((n, d)), whole((1, d)), _layer_weight(w_kv, layer)],
        out_specs=[whole((n, d)), whole((n, d))],
        out_shape=[jax.ShapeDtypeStruct((n, d), BF16), jax.ShapeDtypeStruct((n, d), BF16)],
        compiler_params=pltpu.CompilerParams(
            dimension_semantics=("arbitrary",), vmem_limit_bytes=VMEM_LIMIT),
        name="mem_kv",
    )(mem, g, w_kv)


def _out_xattn_kernel(x_ref, ycn_ref, yat_ref, gatt_ref, wout_ref, gx_ref, wxq_ref,
                      kx_ref, vx_ref, wxo_ref, o_ref):
    d = x_ref.shape[1]
    xd = d // N_XHEADS
    ya_t = yat_ref[...]
    ms = jnp.mean(ya_t * ya_t, axis=0, keepdims=True)
    ya = jnp.transpose(ya_t * lax.rsqrt(ms + EPS))
    y = jnp.concatenate([ycn_ref[...], (ya * gatt_ref[...]).astype(BF16)], axis=-1)
    x1 = x_ref[...] + jnp.dot(y, wout_ref[...], preferred_element_type=F32)

    hx = _rms(x1, gx_ref[...]).astype(BF16)
    q = (jnp.dot(hx, wxq_ref[...], preferred_element_type=F32) * (xd ** -0.5)).astype(BF16)
    nt = (((1,), (1,)), ((), ()))
    heads = []
    for hh in range(N_XHEADS):
        sl = slice(hh * xd, (hh + 1) * xd)
        s = lax.dot_general(q[:, sl], kx_ref[:, sl], nt, preferred_element_type=F32)
        p = jnp.exp(s - jnp.max(s, axis=-1, keepdims=True))
        denom = jnp.sum(p, axis=-1, keepdims=True)
        o = jnp.dot(p.astype(BF16), vx_ref[:, sl], preferred_element_type=F32) / denom
        heads.append(o.astype(BF16))
    o = jnp.concatenate(heads, axis=-1)
    o_ref[...] = x1 + jnp.dot(o, wxo_ref[...], preferred_element_type=F32)


def _out_xattn(x, ycn, ya_t, gatt, wout, gx, wxq, kx, vx, wxo, layer):
    s, d = x.shape
    tm = OUT_TILE
    n_mem = kx.shape[0]
    row = lambda width: pl.BlockSpec((tm, width), lambda i: (i, 0))
    return pl.pallas_call(
        _out_xattn_kernel,
        grid=(s // tm,),
        in_specs=[
            row(d), row(CONV_WIDTH), pl.BlockSpec((ATT_WIDTH, tm), lambda i: (0, i)),
            _const((1, ATT_WIDTH)), _layer_weight(wout, layer), _const((1, d)),
            _layer_weight(wxq, layer), _const((n_mem, d)), _const((n_mem, d)),
            _layer_weight(wxo, layer),
        ],
        out_specs=row(d),
        out_shape=jax.ShapeDtypeStruct((s, d), F32),
        compiler_params=pltpu.CompilerParams(
            dimension_semantics=("arbitrary",), vmem_limit_bytes=VMEM_LIMIT),
        name="out_xattn",
    )(x, ycn, ya_t, gatt, wout, gx, wxq, kx, vx, wxo)


def kernel(x, mem, g_ffn1, w_ffn1_gu, w_ffn1_down, g_mix, w_mix_in, w_conv, b_f, g_conv_out,
           g_att_out, w_mix_out, g_xattn, g_mem, w_xq, w_xkv, w_xo, g_ffn2, w_ffn2_gu,
           w_ffn2_down, g_final):
    b, s, d = x.shape
    assert b == 1 and w_mix_in.shape[2] == 3 * CONV_WIDTH + 3 * ATT_WIDTH + N_HEADS
    assert N_HEADS == SUBLANES
    depth = g_ffn1.shape[0]
    xs = x.reshape(s, d)
    mem2 = mem.reshape(mem.shape[1], d)
    row = lambda a: a.reshape(1, -1).astype(F32)
    gfin = row(g_final)
    w1gu, w1d = _to_bf16(w_ffn1_gu, 0, 1), _to_bf16(w_ffn1_down, 0, 1)
    win, wf = _cast_mix_in(w_mix_in)
    later = (w_ffn2_gu, w_ffn2_down, w_xq, w_xkv, w_xo, w_mix_out)
    gate_lanes = (0, LANES - N_HEADS * SUBLANES)
    w1_layer0 = 0

    for l in range(depth):
        if l == 0:
            xs, w2gu, w2d, wxq, wxkv, wxo, wout = _ffn(
                xs, row(g_ffn1[l]), w1gu, w1d, 0, gfin, final_norm=False, tm=FFN_SIDE_TILE,
                side=tuple((w, 0, depth) for w in later))
        else:
            (xs,) = _ffn(xs, row(g_ffn1[l]), w1gu, w1d, l - w1_layer0, gfin, final_norm=False,
                         tm=FFN_TILE)

        bfr = jnp.pad(jnp.repeat(b_f[l], SUBLANES), gate_lanes).reshape(1, LANES).astype(F32)
        ycn, qt, k, vt, c, qn, kn = _mix(xs, row(g_mix[l]), win, l, wf, bfr,
                                         w_conv[l].astype(F32), row(g_conv_out[l]))

        ya_t = _attention(qt, k, vt, c, qn, kn)

        kx, vx = _memkv(mem2, row(g_mem[l]), wxkv, l)
        xs = _out_xattn(xs, ycn, ya_t, row(g_att_out[l]), wout, row(g_xattn[l]), wxq, kx, vx,
                        wxo, l)

        last = l == depth - 1
        if l == 0 and not last:
            xs, w1gu, w1d = _ffn(xs, row(g_ffn2[l]), w2gu, w2d, l, gfin, final_norm=last,
                                 tm=FFN_TILE,
                                 side=((w_ffn1_gu, 1, depth - 1), (w_ffn1_down, 1, depth - 1)))
            w1_layer0 = 1
        else:
            (xs,) = _ffn(xs, row(g_ffn2[l]), w2gu, w2d, l, gfin, final_norm=last, tm=FFN_TILE)
    return xs.reshape(b, s, d)
```
